```python
import jax, jax.numpy as jnp
from jax import lax
import numpy as np

D_MODEL = 1024
BATCH = 1
SEQ = 16384
DEPTH = 2

HEAD_DIM = 64
MOBA_HEADS = 8
MOBA_BLOCK = 256
MOBA_TOPK = 3
SWA_Q_HEADS = 8
SWA_KV_HEADS = 2
SWA_WINDOW = 128
N_REL_BUCKETS = 32
REL_MAX_DISTANCE = 2048
D_FF = 4 * D_MODEL
Q_CHUNK = 128
NORM_EPS = 1e-6

MOBA_WIDTH = MOBA_HEADS * HEAD_DIM
SWA_Q_WIDTH = SWA_Q_HEADS * HEAD_DIM
SWA_KV_WIDTH = SWA_KV_HEADS * HEAD_DIM
N_ATTN_HEADS = MOBA_HEADS + SWA_Q_HEADS
IN_WIDTH = 3 * MOBA_WIDTH + SWA_Q_WIDTH + 2 * SWA_KV_WIDTH + 2 * D_MODEL

kernel_name = "hybrid_moba_swa_gated_block"


def rms_norm(x, g):
    xf = x.astype(jnp.float32)
    y = xf * lax.rsqrt(jnp.mean(xf * xf, axis=-1, keepdims=True) + NORM_EPS)
    return (y * g.astype(jnp.float32)).astype(x.dtype)


def rel_bucket(dist):
    n = jnp.maximum(dist, 0)
    exact = N_REL_BUCKETS // 2
    nf = jnp.maximum(n, 1).astype(jnp.float32)
    large = exact + (jnp.log(nf / exact) / np.float32(np.log(REL_MAX_DISTANCE / exact))
                     * (N_REL_BUCKETS - exact)).astype(jnp.int32)
    large = jnp.minimum(large, N_REL_BUCKETS - 1)
    return jnp.where(n < exact, n, large)


def moba_attention(q, k, v, bias_table):
    B, H, S, hd = q.shape
    nb = -(-S // MOBA_BLOCK)
    pad = nb * MOBA_BLOCK - S
    kp = jnp.pad(k, ((0, 0), (0, 0), (0, pad), (0, 0)))
    vp = jnp.pad(v, ((0, 0), (0, 0), (0, pad), (0, 0)))
    kb = kp.reshape(B, H, nb, MOBA_BLOCK, hd)
    vb = vp.reshape(B, H, nb, MOBA_BLOCK, hd)
    kmean = jnp.mean(kb.astype(jnp.float32), axis=3)
    gate = jnp.einsum('bhsd,bhnd->bhsn', q.astype(jnp.float32), kmean)
    qblk = jnp.arange(S) // MOBA_BLOCK
    past = jnp.arange(nb)[None, :] < qblk[:, None]
    gate = jnp.where(past[None, None], gate, -jnp.inf)
    n_sel = min(MOBA_TOPK, nb)
    _, sel = lax.top_k(gate, n_sel)
    sel_valid = sel < qblk[None, None, :, None]
    scale = HEAD_DIM ** -0.5
    n_chunks = S // Q_CHUNK
    q_c = q.reshape(B, H, n_chunks, Q_CHUNK, hd).transpose(2, 0, 1, 3, 4)
    sel_c = sel.reshape(B, H, n_chunks, Q_CHUNK, n_sel).transpose(2, 0, 1, 3, 4)
    val_c = sel_valid.reshape(B, H, n_chunks, Q_CHUNK, n_sel).transpose(2, 0, 1, 3, 4)
    b_ix = jnp.arange(B)[:, None, None, None]
    h_ix = jnp.arange(H)[None, :, None, None]
    h5 = jnp.arange(H)[None, :, None, None, None]
    offs = jnp.arange(MOBA_BLOCK)
    bias_t = bias_table.T.astype(jnp.float32)
    n_past = n_sel * MOBA_BLOCK

    def chunk(args):
        c, qc, sc, vc = args
        qpos = c * Q_CHUNK + jnp.arange(Q_CHUNK)
        kg = kb[b_ix, h_ix, sc]
        vg = vb[b_ix, h_ix, sc]
        lp = jnp.einsum('bhqd,bhqnkd->bhqnk', qc, kg).astype(jnp.float32) * scale
        kpos_p = sc[..., None] * MOBA_BLOCK + offs
        lp = lp + bias_t[h5, rel_bucket(qpos[None, None, :, None, None] - kpos_p)]
        lp = jnp.where(vc[..., None], lp, -jnp.inf)
        ob = (c * Q_CHUNK) // MOBA_BLOCK
        ko = lax.dynamic_index_in_dim(kb, ob, axis=2, keepdims=False)
        vo = lax.dynamic_index_in_dim(vb, ob, axis=2, keepdims=False)
        lo = jnp.einsum('bhqd,bhkd->bhqk', qc, ko).astype(jnp.float32) * scale
        d_o = qpos[:, None] - (ob * MOBA_BLOCK + offs)[None, :]
        lo = lo + bias_t[:, rel_bucket(d_o)][None]
        lo = jnp.where((d_o >= 0)[None, None], lo, -jnp.inf)
        logits = jnp.concatenate([lp.reshape(B, H, Q_CHUNK, n_past), lo], axis=-1)
        p = jax.nn.softmax(logits, axis=-1)
        out = (jnp.einsum('bhqk,bhqkd->bhqd', p[..., :n_past].astype(vg.dtype),
                          vg.reshape(B, H, Q_CHUNK, n_past, hd))
               + jnp.einsum('bhqk,bhkd->bhqd', p[..., n_past:].astype(vo.dtype), vo))
        return out.astype(q.dtype)

    outs = lax.map(chunk, (jnp.arange(n_chunks), q_c, sel_c, val_c))
    return outs.transpose(1, 2, 0, 3, 4).reshape(B, H, S, hd)


def swa_attention(q, k, v, sinks, bias_table):
    B, HQ, S, hd = q.shape
    HKV = k.shape[1]
    G = HQ // HKV
    W = SWA_WINDOW
    nb = S // W
    qb = q.reshape(B, HKV, G, nb, W, hd)
    kb = k.reshape(B, HKV, nb, W, hd)
    vb = v.reshape(B, HKV, nb, W, hd)
    zpad = ((0, 0), (0, 0), (1, 0), (0, 0), (0, 0))
    kband = jnp.concatenate([jnp.pad(kb, zpad)[:, :, :-1], kb], axis=3)
    vband = jnp.concatenate([jnp.pad(vb, zpad)[:, :, :-1], vb], axis=3)
    scale = HEAD_DIM ** -0.5
    logits = jnp.einsum('bkgnqd,bkncd->bkgnqc', qb, kband).astype(jnp.float32) * scale
    kj = jnp.arange(2 * W) - W
    dist = jnp.arange(W)[:, None] - kj[None, :]
    bias = bias_table.T.astype(jnp.float32)[:, rel_bucket(dist)].reshape(HKV, G, 1, W, 2 * W)
    in_win = (dist >= 0) & (dist < W)
    real = (jnp.arange(nb)[:, None] * W + kj[None, :]) >= 0
    mask = in_win[None] & real[:, None, :]
    logits = jnp.where(mask, logits + bias, -jnp.inf)
    sink = jnp.broadcast_to(sinks.astype(jnp.float32).reshape(1, HKV, G, 1, 1, 1),
                            logits.shape[:-1] + (1,))
    p = jax.nn.softmax(jnp.concatenate([logits, sink], axis=-1), axis=-1)[..., :-1]
    out = jnp.einsum('bkgnqc,bkncd->bkgnqd', p.astype(vband.dtype), vband)
    return out.reshape(B, HQ, S, hd).astype(q.dtype)


def _heads(t, n):
    B, S, _ = t.shape
    return t.reshape(B, S, n, HEAD_DIM).transpose(0, 2, 1, 3)


def _merge_heads(t):
    B, H, S, hd = t.shape
    return t.transpose(0, 2, 1, 3).reshape(B, S, H * hd)


def setup_inputs(seed: int = 0) -> dict:
    key = jax.random.key(seed)
    ks = jax.random.split(key, 20)
    f32 = jnp.float32
    nrm = lambda k, shape, s: jax.random.normal(k, shape, f32) * s
    return {
        "x": nrm(ks[0], (BATCH, SEQ, D_MODEL), 1.0),
        "rel_bias": nrm(ks[1], (N_REL_BUCKETS, N_ATTN_HEADS), 0.5),
        "g_mix": 1.0 + nrm(ks[2], (DEPTH, D_MODEL), 0.02),
        "w_in": nrm(ks[3], (DEPTH, D_MODEL, IN_WIDTH), D_MODEL ** -0.5),
        "b_gate": nrm(ks[4], (DEPTH, 2 * D_MODEL), 0.02),
        "q_norm_a": 1.0 + nrm(ks[5], (DEPTH, HEAD_DIM), 0.02),
        "k_norm_a": 1.0 + nrm(ks[6], (DEPTH, HEAD_DIM), 0.02),
        "q_norm_b": 1.0 + nrm(ks[7], (DEPTH, HEAD_DIM), 0.02),
        "k_norm_b": 1.0 + nrm(ks[8], (DEPTH, HEAD_DIM), 0.02),
        "sinks": nrm(ks[9], (DEPTH, SWA_Q_HEADS), 0.5),
        "w_branch_a": nrm(ks[10], (DEPTH, MOBA_WIDTH, D_MODEL), MOBA_WIDTH ** -0.5),
        "w_branch_b": nrm(ks[11], (DEPTH, SWA_Q_WIDTH, D_MODEL), SWA_Q_WIDTH ** -0.5),
        "w_out": nrm(ks[12], (DEPTH, D_MODEL, D_MODEL), D_MODEL ** -0.5),
        "g_ffn": 1.0 + nrm(ks[13], (DEPTH, D_MODEL), 0.02),
        "w_ff1": nrm(ks[14], (DEPTH, D_MODEL, D_FF), D_MODEL ** -0.5),
        "w_ff2": nrm(ks[15], (DEPTH, D_FF, D_MODEL), D_FF ** -0.5),
    }


def reference(x, rel_bias, g_mix, w_in, b_gate, q_norm_a, k_norm_a, q_norm_b, k_norm_b,
              sinks, w_branch_a, w_branch_b, w_out, g_ffn, w_ff1, w_ff2):
    bias_a = rel_bias[:, :MOBA_HEADS]
    bias_b = rel_bias[:, MOBA_HEADS:]
    splits = np.cumsum([MOBA_WIDTH, MOBA_WIDTH, MOBA_WIDTH, SWA_Q_WIDTH,
                        SWA_KV_WIDTH, SWA_KV_WIDTH, D_MODEL]).tolist()
    for l in range(DEPTH):
        h = rms_norm(x, g_mix[l])
        proj = h @ w_in[l]
        qa, ka, va, qb, kb, vb, ga, gb = jnp.split(proj, splits, axis=-1)
        qa = rms_norm(_heads(qa, MOBA_HEADS), q_norm_a[l])
        ka = rms_norm(_heads(ka, MOBA_HEADS), k_norm_a[l])
        va = _heads(va, MOBA_HEADS)
        qb = rms_norm(_heads(qb, SWA_Q_HEADS), q_norm_b[l])
        kb = rms_norm(_heads(kb, SWA_KV_HEADS), k_norm_b[l])
        vb = _heads(vb, SWA_KV_HEADS)
        ya = _merge_heads(moba_attention(qa, ka, va, bias_a))
        yb = _merge_heads(swa_attention(qb, kb, vb, sinks[l], bias_b))
        gate_a = jax.nn.sigmoid((ga + b_gate[l, :D_MODEL]).astype(jnp.float32)).astype(x.dtype)
        gate_b = jax.nn.sigmoid((gb + b_gate[l, D_MODEL:]).astype(jnp.float32)).astype(x.dtype)
        merged = gate_a * (ya @ w_branch_a[l]) + gate_b * (yb @ w_branch_b[l])
        x = x + merged @ w_out[l]
        h2 = rms_norm(x, g_ffn[l])
        x = x + jnp.square(jax.nn.relu(h2 @ w_ff1[l])) @ w_ff2[l]
    return x
```

```python
import functools

import jax
import jax.numpy as jnp
import numpy as np
from jax import lax
from jax.experimental import pallas as pl
from jax.experimental.pallas import tpu as pltpu

HEAD_DIM = 64
MOBA_HEADS = 8
MOBA_BLOCK = 256
MOBA_TOPK = 3
SWA_Q_HEADS = 8
SWA_KV_HEADS = 2
SWA_WINDOW = 128
N_REL_BUCKETS = 32
REL_MAX_DISTANCE = 2048
NORM_EPS = 1e-6
ATTN_SCALE = HEAD_DIM ** -0.5

LANES = 128
HEAD_PAIRS = MOBA_HEADS // 2
VT_ROWS = 80
MOBA_NEAR = 7
VMEM_LIMIT = 56 * 1024 * 1024

BF16 = jnp.bfloat16
F32 = jnp.float32
NEG_INF = float("-inf")
M_INIT = -1e30


def _rel_bucket_np(dist):
    n = np.maximum(dist, 0)
    exact = N_REL_BUCKETS // 2
    nf = np.maximum(n, 1).astype(np.float32)
    large = exact + (np.log(nf / np.float32(exact)) / np.float32(np.log(REL_MAX_DISTANCE / exact))
                     * np.float32(N_REL_BUCKETS - exact)).astype(np.int32)
    large = np.minimum(large, N_REL_BUCKETS - 1)
    return np.where(n < exact, n, large)


def _nt_dot(a, b):
    return lax.dot_general(a, b, (((1,), (1,)), ((), ())), preferred_element_type=F32)


def _dot(a, b):
    return jnp.dot(a, b, preferred_element_type=F32)


def _rms_norm_rows(xf, g):
    ms = jnp.mean(xf * xf, axis=-1, keepdims=True)
    return xf * lax.rsqrt(ms + NORM_EPS) * g


def _head_norm(acc, g2):
    w = acc.shape[1]
    lane = lax.broadcasted_iota(jnp.int32, (1, LANES), 1)
    lo = lane < HEAD_DIM
    outs = []
    for b in range(w // LANES):
        xb = acc[:, b * LANES:(b + 1) * LANES]
        x2 = xb * xb
        s_lo = jnp.sum(jnp.where(lo, x2, 0.0), axis=-1, keepdims=True)
        s_hi = jnp.sum(jnp.where(lo, 0.0, x2), axis=-1, keepdims=True)
        r = jnp.where(lo, lax.rsqrt(s_lo / HEAD_DIM + NORM_EPS), lax.rsqrt(s_hi / HEAD_DIM + NORM_EPS))
        outs.append(xb * r * g2[:, b * LANES:(b + 1) * LANES])
    return outs[0] if len(outs) == 1 else jnp.concatenate(outs, axis=1)


def _v_transposed(v, n_heads):
    rows = v.shape[0]
    vt = v.T
    r = lax.broadcasted_iota(jnp.int32, (VT_ROWS - HEAD_DIM, rows), 0)
    aug = jnp.where(r == 0, 1.0, 0.0).astype(F32)
    parts = []
    for h in range(n_heads):
        parts.append(vt[h * HEAD_DIM:(h + 1) * HEAD_DIM])
        parts.append(aug)
    return jnp.concatenate(parts, axis=0).astype(BF16)


_W_QA, _W_KA, _W_VA, _W_QB, _W_KB, _W_VB, _W_GA, _W_GB = 512, 512, 512, 512, 256, 128, 1024, 1024
_PROJ_COLS = np.cumsum([0, _W_QA, _W_KA, _W_VA, _W_QB, _W_KB, _W_VB, _W_GA, _W_GB]).tolist()


def _proj_kernel(x_ref, gmix_ref, w_ref, bgate_ref, gqa_ref, gka_ref, gqb_ref, gkb_ref,
                 qa_ref, ka_ref, kmean_ref, vat_ref, qb_ref, kb_ref, vbt_ref, ga_ref, gb_ref):
    h = _rms_norm_rows(x_ref[...], gmix_ref[...]).astype(BF16)
    c = _PROJ_COLS

    def seg(k):
        return _dot(h, w_ref[:, c[k]:c[k + 1]])

    qa_ref[...] = (_head_norm(seg(0), gqa_ref[...]) * ATTN_SCALE).astype(BF16)
    ka = _head_norm(seg(1), gka_ref[...])
    ka_ref[...] = ka.astype(BF16)
    kmean_ref[0] = jnp.mean(ka, axis=0, keepdims=True)
    vat_ref[...] = _v_transposed(seg(2), MOBA_HEADS)
    qb_ref[...] = (_head_norm(seg(3), gqb_ref[...]) * ATTN_SCALE).astype(BF16)
    kb_ref[...] = _head_norm(seg(4), gkb_ref[...]).astype(BF16)
    vbt_ref[...] = _v_transposed(seg(5), SWA_KV_HEADS)
    bg = bgate_ref[...]
    d = _W_GA
    ga_ref[...] = 1.0 / (1.0 + jnp.exp(-(seg(6) + bg[:, :d])))
    gb_ref[...] = 1.0 / (1.0 + jnp.exp(-(seg(7) + bg[:, d:])))


def _proj_call(x, gmix, w, bgate, gqa, gka, gqb, gkb):
    s, d = x.shape
    tm = MOBA_BLOCK
    nblk = s // tm
    full = lambda shape: pl.BlockSpec(shape, lambda i: (0,) * len(shape))
    rows = lambda width: pl.BlockSpec((tm, width), lambda i: (i, 0))
    cols = lambda height: pl.BlockSpec((height, tm), lambda i: (0, i))
    out_shape = (
        jax.ShapeDtypeStruct((s, _W_QA), BF16),
        jax.ShapeDtypeStruct((s, _W_KA), BF16),
        jax.ShapeDtypeStruct((nblk, 1, _W_KA), F32),
        jax.ShapeDtypeStruct((MOBA_HEADS * VT_ROWS, s), BF16),
        jax.ShapeDtypeStruct((s, _W_QB), BF16),
        jax.ShapeDtypeStruct((s, _W_KB), BF16),
        jax.ShapeDtypeStruct((SWA_KV_HEADS * VT_ROWS, s), BF16),
        jax.ShapeDtypeStruct((s, _W_GA), F32),
        jax.ShapeDtypeStruct((s, _W_GB), F32),
    )
    out_specs = (
        rows(_W_QA), rows(_W_KA), pl.BlockSpec((1, 1, _W_KA), lambda i: (i, 0, 0)),
        cols(MOBA_HEADS * VT_ROWS), rows(_W_QB), rows(_W_KB), cols(SWA_KV_HEADS * VT_ROWS),
        rows(_W_GA), rows(_W_GB),
    )
    return pl.pallas_call(
        _proj_kernel,
        grid=(nblk,),
        in_specs=[rows(d), full(gmix.shape), full(w.shape), full(bgate.shape),
                  full(gqa.shape), full(gka.shape), full(gqb.shape), full(gkb.shape)],
        out_specs=out_specs,
        out_shape=out_shape,
        compiler_params=pltpu.CompilerParams(dimension_semantics=("arbitrary",), vmem_limit_bytes=VMEM_LIMIT),
        name="proj",
    )(x, gmix, w, bgate, gqa, gka, gqb, gkb)


def _moba_kernel(cfar_ref, q_ref, k_ref, vt_ref, kmean_ref, bias_ref, o_ref,
                 qm_scr, sel_scr, m_scr, acc_scr, *, nblk):
    p = pl.program_id(0)
    i = pl.program_id(1)
    blk = MOBA_BLOCK
    q = q_ref[...]
    lane = lax.broadcasted_iota(jnp.int32, (1, LANES), 1)
    km = kmean_ref[...]
    km_hi = km.astype(BF16)
    km_lo = (km - km_hi.astype(F32)).astype(BF16)
    n_iota = lax.broadcasted_iota(jnp.int32, (nblk, blk), 0)

    for hh in range(2):
        hmask = (lane < HEAD_DIM) if hh == 0 else (lane >= HEAD_DIM)
        qm = jnp.where(hmask, q, jnp.zeros_like(q))
        qm_scr[hh] = qm
        g = _nt_dot(km_hi, qm) + _nt_dot(km_lo, qm)
        g = jnp.where(n_iota < i, g, NEG_INF)
        sel = jnp.zeros((nblk, blk), F32)
        for _ in range(MOBA_TOPK):
            mx = jnp.max(g, axis=0, keepdims=True)
            idx = jnp.min(jnp.where(g == mx, n_iota, nblk), axis=0, keepdims=True)
            hit = n_iota == idx
            valid = jnp.where(mx > NEG_INF, 1.0, 0.0)
            sel = jnp.maximum(sel, jnp.where(hit, valid, 0.0))
            g = jnp.where(hit, NEG_INF, g)
        sel_scr[hh] = jnp.where(n_iota == i, 1.0, sel)
        m_scr[hh] = jnp.full((1, blk), M_INIT, F32)
        acc_scr[hh] = jnp.zeros((VT_ROWS, blk), F32)

    def attend(j, near):
        off = pl.multiple_of(j * blk, blk)
        kj = k_ref[pl.ds(off, blk), :]
        vjt = vt_ref[:, pl.ds(off, blk)]
        for hh in range(2):
            st = _nt_dot(kj, qm_scr[hh])
            on = sel_scr[hh, pl.ds(j, 1), :] > 0.5
            m_old = m_scr[hh]
            if near:
                st = st + bias_ref[i - j, hh]
                m_new = jnp.where(on, jnp.maximum(m_old, jnp.max(st, axis=0, keepdims=True)), m_old)
                shift = jnp.where(on, m_new, jnp.inf)
            else:
                c = cfar_ref[2 * p + hh]
                m_new = jnp.where(on, jnp.maximum(m_old, jnp.max(st, axis=0, keepdims=True) + c), m_old)
                shift = jnp.where(on, m_new - c, jnp.inf)
            pt = jnp.exp(st - shift).astype(BF16)
            alpha = jnp.exp(m_old - m_new)
            acc_scr[hh] = acc_scr[hh] * alpha + _dot(vjt[hh * VT_ROWS:(hh + 1) * VT_ROWS], pt)
            m_scr[hh] = m_new

    n_far = jnp.maximum(i - (MOBA_NEAR - 1), 0)

    def far_body(j, carry):
        attend(j, False)
        return carry

    def near_body(j, carry):
        attend(j, True)
        return carry

    lax.fori_loop(0, n_far, far_body, 0)
    lax.fori_loop(n_far, i + 1, near_body, 0)

    outs = []
    for hh in range(2):
        a = acc_scr[hh]
        outs.append(a[:HEAD_DIM] / a[HEAD_DIM:HEAD_DIM + 1])
    o_ref[...] = jnp.concatenate(outs, axis=0).T.astype(BF16)


def _moba_call(cfar, qa, ka, vat, kmean, bias_t):
    s = qa.shape[0]
    blk = MOBA_BLOCK
    nblk = s // blk
    grid_spec = pltpu.PrefetchScalarGridSpec(
        num_scalar_prefetch=1,
        grid=(HEAD_PAIRS, nblk),
        in_specs=[
            pl.BlockSpec((blk, LANES), lambda p, i, c: (i, p)),
            pl.BlockSpec((s, LANES), lambda p, i, c: (0, p)),
            pl.BlockSpec((2 * VT_ROWS, s), lambda p, i, c: (p, 0)),
            pl.BlockSpec((nblk, LANES), lambda p, i, c: (0, p)),
            pl.BlockSpec((MOBA_NEAR, 2, blk, blk), lambda p, i, c: (0, p, 0, 0)),
        ],
        out_specs=pl.BlockSpec((blk, LANES), lambda p, i, c: (i, p)),
        scratch_shapes=[
            pltpu.VMEM((2, blk, LANES), BF16),
            pltpu.VMEM((2, nblk, blk), F32),
            pltpu.VMEM((2, 1, blk), F32),
            pltpu.VMEM((2, VT_ROWS, blk), F32),
        ],
    )
    return pl.pallas_call(
        functools.partial(_moba_kernel, nblk=nblk),
        grid_spec=grid_spec,
        out_shape=jax.ShapeDtypeStruct((s, MOBA_HEADS * HEAD_DIM), BF16),
        compiler_params=pltpu.CompilerParams(dimension_semantics=("arbitrary", "arbitrary"),
                                             vmem_limit_bytes=VMEM_LIMIT),
        name="moba",
    )(cfar, qa, ka, vat, kmean, bias_t)


def _swa_kernel(sink_ref, q_ref, kprev_ref, kcur_ref, vprev_ref, vcur_ref, bias_ref, o_ref):
    nb = pl.program_id(0)
    w = SWA_WINDOW
    kband = jnp.concatenate([kprev_ref[...], kcur_ref[...]], axis=0)
    vband = jnp.concatenate([vprev_ref[...], vcur_ref[...]], axis=1)
    row = lax.broadcasted_iota(jnp.int32, (2 * w, 1), 0)
    keep = row >= jnp.where(nb > 0, 0, w)
    lane = lax.broadcasted_iota(jnp.int32, (1, LANES), 1)
    col = lax.broadcasted_iota(jnp.int32, (1, 2 * w), 1)
    group = SWA_Q_HEADS // SWA_KV_HEADS
    for pr in range(SWA_Q_HEADS // 2):
        g = (2 * pr) // group
        qp = q_ref[:, pr * LANES:(pr + 1) * LANES]
        zero = jnp.zeros_like(qp)
        qq = jnp.concatenate([jnp.where(lane < HEAD_DIM, qp, zero),
                              jnp.where(lane >= HEAD_DIM, qp, zero)], axis=0)
        kd = kband[:, g * LANES:(g + 1) * LANES]
        st = _nt_dot(kd, qq)
        st = jnp.where(keep, st + bias_ref[pr], NEG_INF)
        sink = jnp.where(col < w, sink_ref[2 * pr], sink_ref[2 * pr + 1])
        m = jnp.maximum(jnp.max(st, axis=0, keepdims=True), sink)
        pt = jnp.exp(st - m).astype(BF16)
        acc = _dot(vband[g * VT_ROWS:(g + 1) * VT_ROWS], pt)
        denom = acc[HEAD_DIM:HEAD_DIM + 1] + jnp.exp(sink - m)
        o = acc[:HEAD_DIM] / denom
        oo = jnp.concatenate([o[:, :w], o[:, w:]], axis=0)
        o_ref[:, pr * LANES:(pr + 1) * LANES] = oo.T.astype(BF16)


def _swa_call(sinks, qb, kbd, vbt, bias_t):
    s = qb.shape[0]
    w = SWA_WINDOW
    nb = s // w
    prev = lambda i, c: (jnp.maximum(i - 1, 0), 0)
    grid_spec = pltpu.PrefetchScalarGridSpec(
        num_scalar_prefetch=1,
        grid=(nb,),
        in_specs=[
            pl.BlockSpec((w, _W_QB), lambda i, c: (i, 0)),
            pl.BlockSpec((w, _W_KB), prev),
            pl.BlockSpec((w, _W_KB), lambda i, c: (i, 0)),
            pl.BlockSpec((SWA_KV_HEADS * VT_ROWS, w), lambda i, c: (0, jnp.maximum(i - 1, 0))),
            pl.BlockSpec((SWA_KV_HEADS * VT_ROWS, w), lambda i, c: (0, i)),
            pl.BlockSpec(bias_t.shape, lambda i, c: (0, 0, 0)),
        ],
        out_specs=pl.BlockSpec((w, _W_QB), lambda i, c: (i, 0)),
    )
    return pl.pallas_call(
        _swa_kernel,
        grid_spec=grid_spec,
        out_shape=jax.ShapeDtypeStruct((s, _W_QB), BF16),
        compiler_params=pltpu.CompilerParams(dimension_semantics=("arbitrary",), vmem_limit_bytes=VMEM_LIMIT),
        name="swa",
    )(sinks, qb, kbd, kbd, vbt, vbt, bias_t)


def _merge_kernel(x_ref, ya_ref, yb_ref, ga_ref, gb_ref, wa_ref, wb_ref, wo_ref, gffn_ref, xo_ref, h2_ref):
    merged = ga_ref[...] * _dot(ya_ref[...], wa_ref[...]) + gb_ref[...] * _dot(yb_ref[...], wb_ref[...])
    xn = x_ref[...] + _dot(merged.astype(BF16), wo_ref[...])
    xo_ref[...] = xn
    h2_ref[...] = _rms_norm_rows(xn, gffn_ref[...]).astype(BF16)


def _merge_call(x, ya, yb, ga, gb, wa, wb, wo, gffn, tm=512):
    s, d = x.shape
    full = lambda a: pl.BlockSpec(a.shape, lambda i: (0,) * a.ndim)
    rows = lambda a: pl.BlockSpec((tm, a.shape[1]), lambda i: (i, 0))
    return pl.pallas_call(
        _merge_kernel,
        grid=(s // tm,),
        in_specs=[rows(x), rows(ya), rows(yb), rows(ga), rows(gb), full(wa), full(wb), full(wo), full(gffn)],
        out_specs=(pl.BlockSpec((tm, d), lambda i: (i, 0)), pl.BlockSpec((tm, d), lambda i: (i, 0))),
        out_shape=(jax.ShapeDtypeStruct((s, d), F32), jax.ShapeDtypeStruct((s, d), BF16)),
        compiler_params=pltpu.CompilerParams(dimension_semantics=("arbitrary",), vmem_limit_bytes=VMEM_LIMIT),
        name="merge",
    )(x, ya, yb, ga, gb, wa, wb, wo, gffn)


def _ffn_kernel(x_ref, h_ref, w1_ref, w2_ref, o_ref, *, tf):
    h = h_ref[...]
    acc = x_ref[...]
    for c in range(w1_ref.shape[1] // tf):
        u = jnp.maximum(_dot(h, w1_ref[:, c * tf:(c + 1) * tf]), 0.0)
        acc = acc + _dot((u * u).astype(BF16), w2_ref[c * tf:(c + 1) * tf, :])
    o_ref[...] = acc


def _ffn_call(x, h2, w1, w2, tm=512, tf=1024):
    s, d = x.shape
    full = lambda a: pl.BlockSpec(a.shape, lambda i: (0,) * a.ndim, pipeline_mode=pl.Buffered(1))
    rows = pl.BlockSpec((tm, d), lambda i: (i, 0))
    return pl.pallas_call(
        functools.partial(_ffn_kernel, tf=tf),
        grid=(s // tm,),
        in_specs=[rows, rows, full(w1), full(w2)],
        out_specs=rows,
        out_shape=jax.ShapeDtypeStruct((s, d), F32),
        compiler_params=pltpu.CompilerParams(dimension_semantics=("arbitrary",), vmem_limit_bytes=VMEM_LIMIT),
        name="ffn",
    )(x, h2, w1, w2)


def _bias_tables(rel_bias):
    blk, w = MOBA_BLOCK, SWA_WINDOW
    ck = np.arange(blk)[:, None]
    rq = np.arange(blk)[None, :]
    dist = np.stack([d * blk + rq - ck for d in range(MOBA_NEAR)])
    moba = rel_bias[:, :MOBA_HEADS].T[:, _rel_bucket_np(dist)]
    moba = jnp.where(dist[None] >= 0, moba, NEG_INF).transpose(1, 0, 2, 3)
    cfar = rel_bias[N_REL_BUCKETS - 1, :MOBA_HEADS]

    ck = np.arange(2 * w)[:, None]
    rq = np.arange(w)[None, :]
    dist = rq - (ck - w)
    swa = rel_bias[:, MOBA_HEADS:].T[:, _rel_bucket_np(dist)]
    swa = jnp.where((dist >= 0) & (dist < w), swa, NEG_INF)
    swa = swa.reshape(SWA_Q_HEADS // 2, 2, 2 * w, w).transpose(0, 2, 1, 3).reshape(SWA_Q_HEADS // 2, 2 * w, 2 * w)
    return moba.astype(F32), cfar.astype(F32), swa.astype(F32)


def _layer_weights(w_in_l):
    c = np.cumsum([0, 512, 512, 512, 512, 128, 128, 1024, 1024]).tolist()
    kb = w_in_l[:, c[4]:c[5]]
    kb_dup = jnp.concatenate([kb[:, :HEAD_DIM], kb[:, :HEAD_DIM], kb[:, HEAD_DIM:], kb[:, HEAD_DIM:]], axis=1)
    return jnp.concatenate([w_in_l[:, :c[4]], kb_dup, w_in_l[:, c[5]:]], axis=1).astype(BF16)


def kernel(x, rel_bias, g_mix, w_in, b_gate, q_norm_a, k_norm_a, q_norm_b, k_norm_b, sinks,
           w_branch_a, w_branch_b, w_out, g_ffn, w_ff1, w_ff2):
    b, s, d = x.shape
    assert b == 1 and s % MOBA_BLOCK == 0
    depth = w_in.shape[0]
    bias_moba, cfar, bias_swa = _bias_tables(rel_bias)
    tile = lambda g, n: jnp.tile(g, n)[None, :]
    xs = x[0]
    for l in range(depth):
        qa, ka, kmean, vat, qb, kbd, vbt, ga, gb = _proj_call(
            xs, g_mix[l][None, :], _layer_weights(w_in[l]), b_gate[l][None, :],
            tile(q_norm_a[l], _W_QA // HEAD_DIM), tile(k_norm_a[l], _W_KA // HEAD_DIM),
            tile(q_norm_b[l], _W_QB // HEAD_DIM), tile(k_norm_b[l], _W_KB // HEAD_DIM))
        ya = _moba_call(cfar, qa, ka, vat, kmean[:, 0, :], bias_moba)
        yb = _swa_call(sinks[l], qb, kbd, vbt, bias_swa)
        xs, h2 = _merge_call(xs, ya, yb, ga, gb, w_branch_a[l].astype(BF16), w_branch_b[l].astype(BF16),
                             w_out[l].astype(BF16), g_ffn[l][None, :])
        xs = _ffn_call(xs, h2, w_ff1[l].astype(BF16), w_ff2[l].astype(BF16))
    return xs[None]
```

```python
import functools

import jax
import jax.numpy as jnp
import numpy as np
from jax import lax
from jax.experimental import pallas as pl
from jax.experimental.pallas import tpu as pltpu

HEAD_DIM = 64
MOBA_HEADS = 8
MOBA_BLOCK = 256
MOBA_TOPK = 3
SWA_Q_HEADS = 8
SWA_KV_HEADS = 2
SWA_WINDOW = 128
N_REL_BUCKETS = 32
REL_MAX_DISTANCE = 2048
NORM_EPS = 1e-6
ATTN_SCALE = HEAD_DIM ** -0.5

LANES = 128
HEAD_PAIRS = MOBA_HEADS // 2
VT_ROWS = 80
MOBA_NEAR = 7
MOBA_FAR_UNROLL = 4
assert MOBA_FAR_UNROLL <= MOBA_NEAR
VMEM_LIMIT = 56 * 1024 * 1024

BF16 = jnp.bfloat16
F32 = jnp.float32
NEG_INF = float("-inf")
M_INIT = -1e30


def _rel_bucket_np(dist):
    n = np.maximum(dist, 0)
    exact = N_REL_BUCKETS // 2
    nf = np.maximum(n, 1).astype(np.float32)
    large = exact + (np.log(nf / np.float32(exact)) / np.float32(np.log(REL_MAX_DISTANCE / exact))
                     * np.float32(N_REL_BUCKETS - exact)).astype(np.int32)
    large = np.minimum(large, N_REL_BUCKETS - 1)
    return np.where(n < exact, n, large)


def _nt_dot(a, b):
    return lax.dot_general(a, b, (((1,), (1,)), ((), ())), preferred_element_type=F32)


def _dot(a, b):
    return jnp.dot(a, b, preferred_element_type=F32)


def _rms_norm_rows(xf, g):
    ms = jnp.mean(xf * xf, axis=-1, keepdims=True)
    return xf * lax.rsqrt(ms + NORM_EPS) * g


def _head_norm(acc, g2):
    w = acc.shape[1]
    lane = lax.broadcasted_iota(jnp.int32, (1, LANES), 1)
    lo = lane < HEAD_DIM
    outs = []
    for b in range(w // LANES):
        xb = acc[:, b * LANES:(b + 1) * LANES]
        x2 = xb * xb
        s_lo = jnp.sum(jnp.where(lo, x2, 0.0), axis=-1, keepdims=True)
        s_hi = jnp.sum(jnp.where(lo, 0.0, x2), axis=-1, keepdims=True)
        r = jnp.where(lo, lax.rsqrt(s_lo / HEAD_DIM + NORM_EPS), lax.rsqrt(s_hi / HEAD_DIM + NORM_EPS))
        outs.append(xb * r * g2[:, b * LANES:(b + 1) * LANES])
    return outs[0] if len(outs) == 1 else jnp.concatenate(outs, axis=1)


def _v_transposed(v, n_heads):
    rows = v.shape[0]
    vt = v.T
    r = lax.broadcasted_iota(jnp.int32, (VT_ROWS - HEAD_DIM, rows), 0)
    aug = jnp.where(r == 0, 1.0, 0.0).astype(F32)
    parts = []
    for h in range(n_heads):
        parts.append(vt[h * HEAD_DIM:(h + 1) * HEAD_DIM])
        parts.append(aug)
    return jnp.concatenate(parts, axis=0).astype(BF16)


_W_QA, _W_KA, _W_VA, _W_QB, _W_KB, _W_VB, _W_GA, _W_GB = 512, 512, 512, 512, 256, 128, 1024, 1024
_PROJ_COLS = np.cumsum([0, _W_QA, _W_KA, _W_VA, _W_QB, _W_KB, _W_VB, _W_GA, _W_GB]).tolist()


def _proj_kernel(x_ref, gmix_ref, w_ref, bgate_ref, gqa_ref, gka_ref, gqb_ref, gkb_ref,
                 qa_ref, ka_ref, kmean_ref, vat_ref, qb_ref, kb_ref, vbt_ref, ga_ref, gb_ref):
    h = _rms_norm_rows(x_ref[...], gmix_ref[...]).astype(BF16)
    c = _PROJ_COLS

    def seg(k):
        return _dot(h, w_ref[:, c[k]:c[k + 1]])

    qa_ref[...] = (_head_norm(seg(0), gqa_ref[...]) * ATTN_SCALE).astype(BF16)
    ka = _head_norm(seg(1), gka_ref[...])
    ka_ref[...] = ka.astype(BF16)
    kmean_ref[0] = jnp.mean(ka, axis=0, keepdims=True)
    vat_ref[...] = _v_transposed(seg(2), MOBA_HEADS)
    qb_ref[...] = (_head_norm(seg(3), gqb_ref[...]) * ATTN_SCALE).astype(BF16)
    kb_ref[...] = _head_norm(seg(4), gkb_ref[...]).astype(BF16)
    vbt_ref[...] = _v_transposed(seg(5), SWA_KV_HEADS)
    bg = bgate_ref[...]
    d = _W_GA
    ga_ref[...] = 1.0 / (1.0 + jnp.exp(-(seg(6) + bg[:, :d])))
    gb_ref[...] = 1.0 / (1.0 + jnp.exp(-(seg(7) + bg[:, d:])))


def _proj_call(x, gmix, w, bgate, gqa, gka, gqb, gkb):
    s, d = x.shape
    tm = MOBA_BLOCK
    nblk = s // tm
    full = lambda shape: pl.BlockSpec(shape, lambda i: (0,) * len(shape))
    rows = lambda width: pl.BlockSpec((tm, width), lambda i: (i, 0))
    cols = lambda height: pl.BlockSpec((height, tm), lambda i: (0, i))
    out_shape = (
        jax.ShapeDtypeStruct((s, _W_QA), BF16),
        jax.ShapeDtypeStruct((s, _W_KA), BF16),
        jax.ShapeDtypeStruct((nblk, 1, _W_KA), F32),
        jax.ShapeDtypeStruct((MOBA_HEADS * VT_ROWS, s), BF16),
        jax.ShapeDtypeStruct((s, _W_QB), BF16),
        jax.ShapeDtypeStruct((s, _W_KB), BF16),
        jax.ShapeDtypeStruct((SWA_KV_HEADS * VT_ROWS, s), BF16),
        jax.ShapeDtypeStruct((s, _W_GA), F32),
        jax.ShapeDtypeStruct((s, _W_GB), F32),
    )
    out_specs = (
        rows(_W_QA), rows(_W_KA), pl.BlockSpec((1, 1, _W_KA), lambda i: (i, 0, 0)),
        cols(MOBA_HEADS * VT_ROWS), rows(_W_QB), rows(_W_KB), cols(SWA_KV_HEADS * VT_ROWS),
        rows(_W_GA), rows(_W_GB),
    )
    return pl.pallas_call(
        _proj_kernel,
        grid=(nblk,),
        in_specs=[rows(d), full(gmix.shape), full(w.shape), full(bgate.shape),
                  full(gqa.shape), full(gka.shape), full(gqb.shape), full(gkb.shape)],
        out_specs=out_specs,
        out_shape=out_shape,
        compiler_params=pltpu.CompilerParams(dimension_semantics=("arbitrary",), vmem_limit_bytes=VMEM_LIMIT),
        name="proj",
    )(x, gmix, w, bgate, gqa, gka, gqb, gkb)


def _moba_kernel(cfar_ref, q_ref, k_ref, vt_ref, kmean_ref, bias_ref, o_ref,
                 qq_scr, sel_scr, m_scr, acc_scr, *, nblk):
    p = pl.program_id(0)
    i = pl.program_id(1)
    blk = MOBA_BLOCK
    q = q_ref[...]
    lane = lax.broadcasted_iota(jnp.int32, (1, LANES), 1)
    km = kmean_ref[...]
    km_hi = km.astype(BF16)
    km_lo = (km - km_hi.astype(F32)).astype(BF16)
    n_iota = lax.broadcasted_iota(jnp.int32, (nblk, blk), 0)

    for hh in range(2):
        cols = slice(hh * blk, (hh + 1) * blk)
        hmask = (lane < HEAD_DIM) if hh == 0 else (lane >= HEAD_DIM)
        qm = jnp.where(hmask, q, jnp.zeros_like(q))
        qq_scr[cols, :] = qm
        g = _nt_dot(km_hi, qm) + _nt_dot(km_lo, qm)
        g = jnp.where(n_iota < i, g, NEG_INF)
        sel = jnp.zeros((nblk, blk), F32)
        for _ in range(MOBA_TOPK):
            mx = jnp.max(g, axis=0, keepdims=True)
            idx = jnp.min(jnp.where(g == mx, n_iota, nblk), axis=0, keepdims=True)
            hit = n_iota == idx
            valid = jnp.where(mx > NEG_INF, 1.0, 0.0)
            sel = jnp.maximum(sel, jnp.where(hit, valid, 0.0))
            g = jnp.where(hit, NEG_INF, g)
        sel_scr[:, cols] = jnp.where(n_iota == i, 1.0, sel)
        acc_scr[hh] = jnp.zeros((VT_ROWS, blk), F32)
    m_scr[...] = jnp.full((1, 2 * blk), M_INIT, F32)

    col = lax.broadcasted_iota(jnp.int32, (1, 2 * blk), 1)
    cfar = jnp.where(col < blk, cfar_ref[2 * p], cfar_ref[2 * p + 1])
    n_far = jnp.maximum(i - (MOBA_NEAR - 1), 0)

    def attend(first, nb, near):
        off = pl.multiple_of(first * blk, blk)
        st = _nt_dot(k_ref[pl.ds(off, nb * blk), :], qq_scr[...])
        m_old = m_scr[...]
        m_new = m_old
        tiles, ons = [], []
        for u in range(nb):
            j = first + u
            s_u = st[u * blk:(u + 1) * blk]
            if near:
                s_u = s_u + bias_ref[jnp.clip(i - j, 0, MOBA_NEAR - 1)]
                on = sel_scr[pl.ds(j, 1), :] > 0.5
                m_u = jnp.max(s_u, axis=0, keepdims=True)
            else:
                on = sel_scr[pl.ds(j, 1), :] > jnp.where(j < n_far, 0.5, 2.0)
                m_u = jnp.max(s_u, axis=0, keepdims=True) + cfar
            m_new = jnp.maximum(m_new, jnp.where(on, m_u, NEG_INF))
            tiles.append(s_u)
            ons.append(on)
        pts = []
        for s_u, on in zip(tiles, ons):
            shift = jnp.where(on, m_new if near else m_new - cfar, jnp.inf)
            pts.append(jnp.exp(s_u - shift).astype(BF16))
        pt = pts[0] if nb == 1 else jnp.concatenate(pts, axis=0)
        alpha = jnp.exp(m_old - m_new)
        for hh in range(2):
            cols = slice(hh * blk, (hh + 1) * blk)
            pv = _dot(vt_ref[hh * VT_ROWS:(hh + 1) * VT_ROWS, pl.ds(off, nb * blk)], pt[:, cols])
            acc_scr[hh] = acc_scr[hh] * alpha[:, cols] + pv
        m_scr[...] = m_new

    def far_body(g, carry):
        attend(g * MOBA_FAR_UNROLL, MOBA_FAR_UNROLL, False)
        return carry

    lax.fori_loop(0, (n_far + MOBA_FAR_UNROLL - 1) // MOBA_FAR_UNROLL, far_body, 0)
    attend(jnp.maximum(i - (MOBA_NEAR - 1), 0), MOBA_NEAR, True)

    outs = []
    for hh in range(2):
        a = acc_scr[hh]
        outs.append(a[:HEAD_DIM] / a[HEAD_DIM:HEAD_DIM + 1])
    o_ref[...] = jnp.concatenate(outs, axis=0).T.astype(BF16)


def _moba_call(cfar, qa, ka, vat, kmean, bias_t):
    s = qa.shape[0]
    blk = MOBA_BLOCK
    nblk = s // blk
    grid_spec = pltpu.PrefetchScalarGridSpec(
        num_scalar_prefetch=1,
        grid=(HEAD_PAIRS, nblk),
        in_specs=[
            pl.BlockSpec((blk, LANES), lambda p, i, c: (i, p)),
            pl.BlockSpec((s, LANES), lambda p, i, c: (0, p)),
            pl.BlockSpec((2 * VT_ROWS, s), lambda p, i, c: (p, 0)),
            pl.BlockSpec((nblk, LANES), lambda p, i, c: (0, p)),
            pl.BlockSpec((MOBA_NEAR, None, blk, 2 * blk), lambda p, i, c: (0, p, 0, 0)),
        ],
        out_specs=pl.BlockSpec((blk, LANES), lambda p, i, c: (i, p)),
        scratch_shapes=[
            pltpu.VMEM((2 * blk, LANES), BF16),
            pltpu.VMEM((nblk, 2 * blk), F32),
            pltpu.VMEM((1, 2 * blk), F32),
            pltpu.VMEM((2, VT_ROWS, blk), F32),
        ],
    )
    return pl.pallas_call(
        functools.partial(_moba_kernel, nblk=nblk),
        grid_spec=grid_spec,
        out_shape=jax.ShapeDtypeStruct((s, MOBA_HEADS * HEAD_DIM), BF16),
        compiler_params=pltpu.CompilerParams(dimension_semantics=("arbitrary", "arbitrary"),
                                             vmem_limit_bytes=VMEM_LIMIT),
        name="moba",
    )(cfar, qa, ka, vat, kmean, bias_t)


def _swa_kernel(sink_ref, q_ref, kprev_ref, kcur_ref, vprev_ref, vcur_ref, bias_ref, o_ref):
    nb = pl.program_id(0)
    w = SWA_WINDOW
    kband = jnp.concatenate([kprev_ref[...], kcur_ref[...]], axis=0)
    vband = jnp.concatenate([vprev_ref[...], vcur_ref[...]], axis=1)
    row = lax.broadcasted_iota(jnp.int32, (2 * w, 1), 0)
    keep = row >= jnp.where(nb > 0, 0, w)
    lane = lax.broadcasted_iota(jnp.int32, (1, LANES), 1)
    col = lax.broadcasted_iota(jnp.int32, (1, 2 * w), 1)
    group = SWA_Q_HEADS // SWA_KV_HEADS
    for pr in range(SWA_Q_HEADS // 2):
        g = (2 * pr) // group
        qp = q_ref[:, pr * LANES:(pr + 1) * LANES]
        zero = jnp.zeros_like(qp)
        qq = jnp.concatenate([jnp.where(lane < HEAD_DIM, qp, zero),
                              jnp.where(lane >= HEAD_DIM, qp, zero)], axis=0)
        kd = kband[:, g * LANES:(g + 1) * LANES]
        st = _nt_dot(kd, qq)
        st = jnp.where(keep, st + bias_ref[pr], NEG_INF)
        sink = jnp.where(col < w, sink_ref[2 * pr], sink_ref[2 * pr + 1])
        m = jnp.maximum(jnp.max(st, axis=0, keepdims=True), sink)
        pt = jnp.exp(st - m).astype(BF16)
        acc = _dot(vband[g * VT_ROWS:(g + 1) * VT_ROWS], pt)
        denom = acc[HEAD_DIM:HEAD_DIM + 1] + jnp.exp(sink - m)
        o = acc[:HEAD_DIM] / denom
        oo = jnp.concatenate([o[:, :w], o[:, w:]], axis=0)
        o_ref[:, pr * LANES:(pr + 1) * LANES] = oo.T.astype(BF16)


def _swa_call(sinks, qb, kbd, vbt, bias_t):
    s = qb.shape[0]
    w = SWA_WINDOW
    nb = s // w
    prev = lambda i, c: (jnp.maximum(i - 1, 0), 0)
    grid_spec = pltpu.PrefetchScalarGridSpec(
        num_scalar_prefetch=1,
        grid=(nb,),
        in_specs=[
            pl.BlockSpec((w, _W_QB), lambda i, c: (i, 0)),
            pl.BlockSpec((w, _W_KB), prev),
            pl.BlockSpec((w, _W_KB), lambda i, c: (i, 0)),
            pl.BlockSpec((SWA_KV_HEADS * VT_ROWS, w), lambda i, c: (0, jnp.maximum(i - 1, 0))),
            pl.BlockSpec((SWA_KV_HEADS * VT_ROWS, w), lambda i, c: (0, i)),
            pl.BlockSpec(bias_t.shape, lambda i, c: (0, 0, 0)),
        ],
        out_specs=pl.BlockSpec((w, _W_QB), lambda i, c: (i, 0)),
    )
    return pl.pallas_call(
        _swa_kernel,
        grid_spec=grid_spec,
        out_shape=jax.ShapeDtypeStruct((s, _W_QB), BF16),
        compiler_params=pltpu.CompilerParams(dimension_semantics=("arbitrary",), vmem_limit_bytes=VMEM_LIMIT),
        name="swa",
    )(sinks, qb, kbd, kbd, vbt, vbt, bias_t)


def _merge_kernel(x_ref, ya_ref, yb_ref, ga_ref, gb_ref, wa_ref, wb_ref, wo_ref, gffn_ref, xo_ref, h2_ref):
    merged = ga_ref[...] * _dot(ya_ref[...], wa_ref[...]) + gb_ref[...] * _dot(yb_ref[...], wb_ref[...])
    xn = x_ref[...] + _dot(merged.astype(BF16), wo_ref[...])
    xo_ref[...] = xn
    h2_ref[...] = _rms_norm_rows(xn, gffn_ref[...]).astype(BF16)


def _merge_call(x, ya, yb, ga, gb, wa, wb, wo, gffn, tm=512):
    s, d = x.shape
    full = lambda a: pl.BlockSpec(a.shape, lambda i: (0,) * a.ndim)
    rows = lambda a: pl.BlockSpec((tm, a.shape[1]), lambda i: (i, 0))
    return pl.pallas_call(
        _merge_kernel,
        grid=(s // tm,),
        in_specs=[rows(x), rows(ya), rows(yb), rows(ga), rows(gb), full(wa), full(wb), full(wo), full(gffn)],
        out_specs=(pl.BlockSpec((tm, d), lambda i: (i, 0)), pl.BlockSpec((tm, d), lambda i: (i, 0))),
        out_shape=(jax.ShapeDtypeStruct((s, d), F32), jax.ShapeDtypeStruct((s, d), BF16)),
        compiler_params=pltpu.CompilerParams(dimension_semantics=("arbitrary",), vmem_limit_bytes=VMEM_LIMIT),
        name="merge",
    )(x, ya, yb, ga, gb, wa, wb, wo, gffn)


def _ffn_kernel(x_ref, h_ref, w1_ref, w2_ref, o_ref, *, tf):
    h = h_ref[...]
    acc = x_ref[...]
    for c in range(w1_ref.shape[1] // tf):
        u = jnp.maximum(_dot(h, w1_ref[:, c * tf:(c + 1) * tf]), 0.0)
        acc = acc + _dot((u * u).astype(BF16), w2_ref[c * tf:(c + 1) * tf, :])
    o_ref[...] = acc


def _ffn_call(x, h2, w1, w2, tm=512, tf=1024):
    s, d = x.shape
    full = lambda a: pl.BlockSpec(a.shape, lambda i: (0,) * a.ndim, pipeline_mode=pl.Buffered(1))
    rows = pl.BlockSpec((tm, d), lambda i: (i, 0))
    return pl.pallas_call(
        functools.partial(_ffn_kernel, tf=tf),
        grid=(s // tm,),
        in_specs=[rows, rows, full(w1), full(w2)],
        out_specs=rows,
        out_shape=jax.ShapeDtypeStruct((s, d), F32),
        compiler_params=pltpu.CompilerParams(dimension_semantics=("arbitrary",), vmem_limit_bytes=VMEM_LIMIT),
        name="ffn",
    )(x, h2, w1, w2)


def _bucket_lookup(table, dist, keep):
    idx = jnp.asarray(np.where(keep, _rel_bucket_np(dist), -1).astype(np.int32))[None]
    out = jnp.full((table.shape[1],) + dist.shape, NEG_INF, F32)
    for b in range(N_REL_BUCKETS):
        out = jnp.where(idx == b, table[b].reshape((-1,) + (1,) * dist.ndim), out)
    return out


def _bias_tables(rel_bias):
    blk, w = MOBA_BLOCK, SWA_WINDOW
    ck = np.arange(blk)[:, None]
    rq = np.arange(blk)[None, :]
    dist = np.stack([d * blk + rq - ck for d in range(MOBA_NEAR)])
    moba = _bucket_lookup(rel_bias[:, :MOBA_HEADS], dist, dist >= 0)
    moba = (moba.reshape(HEAD_PAIRS, 2, MOBA_NEAR, blk, blk).transpose(2, 0, 3, 1, 4)
            .reshape(MOBA_NEAR, HEAD_PAIRS, blk, 2 * blk))
    cfar = rel_bias[N_REL_BUCKETS - 1, :MOBA_HEADS]

    ck = np.arange(2 * w)[:, None]
    rq = np.arange(w)[None, :]
    dist = rq - (ck - w)
    swa = _bucket_lookup(rel_bias[:, MOBA_HEADS:], dist, (dist >= 0) & (dist < w))
    swa = swa.reshape(SWA_Q_HEADS // 2, 2, 2 * w, w).transpose(0, 2, 1, 3).reshape(SWA_Q_HEADS // 2, 2 * w, 2 * w)
    return moba.astype(F32), cfar.astype(F32), swa.astype(F32)


def _layer_weights(w_in_l):
    c = np.cumsum([0, 512, 512, 512, 512, 128, 128, 1024, 1024]).tolist()
    kb = w_in_l[:, c[4]:c[5]]
    kb_dup = jnp.concatenate([kb[:, :HEAD_DIM], kb[:, :HEAD_DIM], kb[:, HEAD_DIM:], kb[:, HEAD_DIM:]], axis=1)
    return jnp.concatenate([w_in_l[:, :c[4]], kb_dup, w_in_l[:, c[5]:]], axis=1).astype(BF16)


def kernel(x, rel_bias, g_mix, w_in, b_gate, q_norm_a, k_norm_a, q_norm_b, k_norm_b, sinks,
           w_branch_a, w_branch_b, w_out, g_ffn, w_ff1, w_ff2):
    b, s, d = x.shape
    assert b == 1 and s % MOBA_BLOCK == 0
    depth = w_in.shape[0]
    bias_moba, cfar, bias_swa = _bias_tables(rel_bias)
    tile = lambda g, n: jnp.tile(g, n)[None, :]
    xs = x[0]
    for l in range(depth):
        qa, ka, kmean, vat, qb, kbd, vbt, ga, gb = _proj_call(
            xs, g_mix[l][None, :], _layer_weights(w_in[l]), b_gate[l][None, :],
            tile(q_norm_a[l], _W_QA // HEAD_DIM), tile(k_norm_a[l], _W_KA // HEAD_DIM),
            tile(q_norm_b[l], _W_QB // HEAD_DIM), tile(k_norm_b[l], _W_KB // HEAD_DIM))
        ya = _moba_call(cfar, qa, ka, vat, kmean[:, 0, :], bias_moba)
        yb = _swa_call(sinks[l], qb, kbd, vbt, bias_swa)
        xs, h2 = _merge_call(xs, ya, yb, ga, gb, w_branch_a[l].astype(BF16), w_branch_b[l].astype(BF16),
                             w_out[l].astype(BF16), g_ffn[l][None, :])
        xs = _ffn_call(xs, h2, w_ff1[l].astype(BF16), w_ff2[l].astype(BF16))
    return xs[None]
```

```python
import functools

import jax
import jax.numpy as jnp
import numpy as np
from jax import lax
from jax.experimental import pallas as pl
from jax.experimental.pallas import tpu as pltpu

HEAD_DIM = 64
MOBA_HEADS = 8
MOBA_BLOCK = 256
MOBA_TOPK = 3
SWA_Q_HEADS = 8
SWA_KV_HEADS = 2
SWA_WINDOW = 128
N_REL_BUCKETS = 32
REL_MAX_DISTANCE = 2048
NORM_EPS = 1e-6
ATTN_SCALE = HEAD_DIM ** -0.5

LANES = 128
HEAD_PAIRS = MOBA_HEADS // 2
VT_ROWS = 80
MOBA_NEAR = 7
MOBA_FAR_UNROLL = 4
assert MOBA_FAR_UNROLL <= MOBA_NEAR
VMEM_LIMIT = 56 * 1024 * 1024

BF16 = jnp.bfloat16
F32 = jnp.float32
NEG_INF = float("-inf")
M_INIT = -1e30


def _rel_bucket_np(dist):
    n = np.maximum(dist, 0)
    exact = N_REL_BUCKETS // 2
    nf = np.maximum(n, 1).astype(np.float32)
    large = exact + (np.log(nf / np.float32(exact)) / np.float32(np.log(REL_MAX_DISTANCE / exact))
                     * np.float32(N_REL_BUCKETS - exact)).astype(np.int32)
    large = np.minimum(large, N_REL_BUCKETS - 1)
    return np.where(n < exact, n, large)


def _nt_dot(a, b):
    return lax.dot_general(a, b, (((1,), (1,)), ((), ())), preferred_element_type=F32)


def _dot(a, b):
    return jnp.dot(a, b, preferred_element_type=F32)


def _rms_norm_rows(xf, g):
    ms = jnp.mean(xf * xf, axis=-1, keepdims=True)
    return xf * lax.rsqrt(ms + NORM_EPS) * g


def _head_norm(acc, g2):
    w = acc.shape[1]
    lane = lax.broadcasted_iota(jnp.int32, (1, LANES), 1)
    lo = lane < HEAD_DIM
    outs = []
    for b in range(w // LANES):
        xb = acc[:, b * LANES:(b + 1) * LANES]
        x2 = xb * xb
        s_lo = jnp.sum(jnp.where(lo, x2, 0.0), axis=-1, keepdims=True)
        s_hi = jnp.sum(jnp.where(lo, 0.0, x2), axis=-1, keepdims=True)
        r = jnp.where(lo, lax.rsqrt(s_lo / HEAD_DIM + NORM_EPS), lax.rsqrt(s_hi / HEAD_DIM + NORM_EPS))
        outs.append(xb * r * g2[:, b * LANES:(b + 1) * LANES])
    return outs[0] if len(outs) == 1 else jnp.concatenate(outs, axis=1)


def _v_transposed(v, n_heads):
    rows = v.shape[0]
    vt = v.T
    r = lax.broadcasted_iota(jnp.int32, (VT_ROWS - HEAD_DIM, rows), 0)
    aug = jnp.where(r == 0, 1.0, 0.0).astype(F32)
    parts = []
    for h in range(n_heads):
        parts.append(vt[h * HEAD_DIM:(h + 1) * HEAD_DIM])
        parts.append(aug)
    return jnp.concatenate(parts, axis=0).astype(BF16)


_W_QA, _W_KA, _W_VA, _W_QB, _W_KB, _W_VB, _W_GA, _W_GB = 512, 512, 512, 512, 256, 128, 1024, 1024
_PROJ_COLS = np.cumsum([0, _W_QA, _W_KA, _W_VA, _W_QB, _W_KB, _W_VB, _W_GA, _W_GB]).tolist()


def _proj_kernel(x_ref, gmix_ref, w_ref, bgate_ref, gqa_ref, gka_ref, gqb_ref, gkb_ref,
                 qa_ref, ka_ref, kmean_ref, vat_ref, qb_ref, kb_ref, vbt_ref, ga_ref, gb_ref):
    h = _rms_norm_rows(x_ref[...], gmix_ref[...]).astype(BF16)
    c = _PROJ_COLS

    def seg(k):
        return _dot(h, w_ref[:, c[k]:c[k + 1]])

    qa_ref[...] = (_head_norm(seg(0), gqa_ref[...]) * ATTN_SCALE).astype(BF16)
    ka = _head_norm(seg(1), gka_ref[...])
    ka_ref[...] = ka.astype(BF16)
    kmean_ref[0] = jnp.mean(ka, axis=0, keepdims=True)
    vat_ref[...] = _v_transposed(seg(2), MOBA_HEADS)
    qb_ref[...] = (_head_norm(seg(3), gqb_ref[...]) * ATTN_SCALE).astype(BF16)
    kb_ref[...] = _head_norm(seg(4), gkb_ref[...]).astype(BF16)
    vbt_ref[...] = _v_transposed(seg(5), SWA_KV_HEADS)
    bg = bgate_ref[...]
    d = _W_GA
    ga_ref[...] = 1.0 / (1.0 + jnp.exp(-(seg(6) + bg[:, :d])))
    gb_ref[...] = 1.0 / (1.0 + jnp.exp(-(seg(7) + bg[:, d:])))


def _proj_call(x, gmix, w, bgate, gqa, gka, gqb, gkb):
    s, d = x.shape
    tm = MOBA_BLOCK
    nblk = s // tm
    full = lambda shape: pl.BlockSpec(shape, lambda i: (0,) * len(shape))
    rows = lambda width: pl.BlockSpec((tm, width), lambda i: (i, 0))
    cols = lambda height: pl.BlockSpec((height, tm), lambda i: (0, i))
    out_shape = (
        jax.ShapeDtypeStruct((s, _W_QA), BF16),
        jax.ShapeDtypeStruct((s, _W_KA), BF16),
        jax.ShapeDtypeStruct((nblk, 1, _W_KA), F32),
        jax.ShapeDtypeStruct((MOBA_HEADS * VT_ROWS, s), BF16),
        jax.ShapeDtypeStruct((s, _W_QB), BF16),
        jax.ShapeDtypeStruct((s, _W_KB), BF16),
        jax.ShapeDtypeStruct((SWA_KV_HEADS * VT_ROWS, s), BF16),
        jax.ShapeDtypeStruct((s, _W_GA), F32),
        jax.ShapeDtypeStruct((s, _W_GB), F32),
    )
    out_specs = (
        rows(_W_QA), rows(_W_KA), pl.BlockSpec((1, 1, _W_KA), lambda i: (i, 0, 0)),
        cols(MOBA_HEADS * VT_ROWS), rows(_W_QB), rows(_W_KB), cols(SWA_KV_HEADS * VT_ROWS),
        rows(_W_GA), rows(_W_GB),
    )
    return pl.pallas_call(
        _proj_kernel,
        grid=(nblk,),
        in_specs=[rows(d), full(gmix.shape), full(w.shape), full(bgate.shape),
                  full(gqa.shape), full(gka.shape), full(gqb.shape), full(gkb.shape)],
        out_specs=out_specs,
        out_shape=out_shape,
        compiler_params=pltpu.CompilerParams(dimension_semantics=("arbitrary",), vmem_limit_bytes=VMEM_LIMIT),
        name="proj",
    )(x, gmix, w, bgate, gqa, gka, gqb, gkb)


def _moba_kernel(cfar_ref, q_ref, k_ref, vt_ref, kmean_ref, bias_ref, o_ref,
                 qq_scr, sel_scr, m_scr, acc_scr, *, nblk):
    p = pl.program_id(0)
    i = pl.program_id(1)
    blk = MOBA_BLOCK
    q = q_ref[...]
    lane = lax.broadcasted_iota(jnp.int32, (1, LANES), 1)
    km = kmean_ref[...]
    km_hi = km.astype(BF16)
    km_lo = (km - km_hi.astype(F32)).astype(BF16)
    n_iota = lax.broadcasted_iota(jnp.int32, (nblk, blk), 0)

    for hh in range(2):
        cols = slice(hh * blk, (hh + 1) * blk)
        hmask = (lane < HEAD_DIM) if hh == 0 else (lane >= HEAD_DIM)
        qm = jnp.where(hmask, q, jnp.zeros_like(q))
        qq_scr[cols, :] = qm
        g = _nt_dot(km_hi, qm) + _nt_dot(km_lo, qm)
        g = jnp.where(n_iota < i, g, NEG_INF)
        sel = jnp.zeros((nblk, blk), F32)
        for _ in range(MOBA_TOPK):
            mx = jnp.max(g, axis=0, keepdims=True)
            idx = jnp.min(jnp.where(g == mx, n_iota, nblk), axis=0, keepdims=True)
            hit = n_iota == idx
            valid = jnp.where(mx > NEG_INF, 1.0, 0.0)
            sel = jnp.maximum(sel, jnp.where(hit, valid, 0.0))
            g = jnp.where(hit, NEG_INF, g)
        sel_scr[:, cols] = jnp.where(n_iota == i, 1.0, sel)
        acc_scr[hh] = jnp.zeros((VT_ROWS, blk), F32)
    m_scr[...] = jnp.full((1, 2 * blk), M_INIT, F32)

    col = lax.broadcasted_iota(jnp.int32, (1, 2 * blk), 1)
    cfar = jnp.where(col < blk, cfar_ref[2 * p], cfar_ref[2 * p + 1])
    n_far = jnp.maximum(i - (MOBA_NEAR - 1), 0)

    def attend(first, nb, near):
        off = pl.multiple_of(first * blk, blk)
        st = _nt_dot(k_ref[pl.ds(off, nb * blk), :], qq_scr[...])
        ms, pvs = [], []
        for u in range(nb):
            j = first + u
            s_u = st[u * blk:(u + 1) * blk]
            if near:
                s_u = s_u + bias_ref[jnp.clip(i - j, 0, MOBA_NEAR - 1)]
                on = sel_scr[pl.ds(j, 1), :] > 0.5
                m_u = jnp.max(s_u, axis=0, keepdims=True)
                ms.append(jnp.where(on, m_u, NEG_INF))
            else:
                on = sel_scr[pl.ds(j, 1), :] > jnp.where(j < n_far, 0.5, 2.0)
                m_u = jnp.max(s_u, axis=0, keepdims=True)
                ms.append(jnp.where(on, m_u + cfar, NEG_INF))
            pt = jnp.exp(s_u - m_u).astype(BF16)
            ko = pl.multiple_of(j * blk, blk)
            pvs.append([_dot(vt_ref[hh * VT_ROWS:(hh + 1) * VT_ROWS, pl.ds(ko, blk)], pt[:, hh * blk:(hh + 1) * blk])
                        for hh in range(2)])
        m_old = m_scr[...]
        m_new = m_old
        for m_u in ms:
            m_new = jnp.maximum(m_new, m_u)
        alpha = jnp.exp(m_old - m_new)
        ws = [jnp.exp(m_u - m_new) for m_u in ms]
        for hh in range(2):
            cols = slice(hh * blk, (hh + 1) * blk)
            acc = acc_scr[hh] * alpha[:, cols]
            for w_u, pv in zip(ws, pvs):
                acc = acc + pv[hh] * w_u[:, cols]
            acc_scr[hh] = acc
        m_scr[...] = m_new

    def far_body(g, carry):
        attend(g * MOBA_FAR_UNROLL, MOBA_FAR_UNROLL, False)
        return carry

    lax.fori_loop(0, (n_far + MOBA_FAR_UNROLL - 1) // MOBA_FAR_UNROLL, far_body, 0)
    attend(jnp.maximum(i - (MOBA_NEAR - 1), 0), MOBA_NEAR, True)

    outs = []
    for hh in range(2):
        a = acc_scr[hh]
        outs.append(a[:HEAD_DIM] / a[HEAD_DIM:HEAD_DIM + 1])
    o_ref[...] = jnp.concatenate(outs, axis=0).T.astype(BF16)


def _moba_call(cfar, qa, ka, vat, kmean, bias_t):
    s = qa.shape[0]
    blk = MOBA_BLOCK
    nblk = s // blk
    grid_spec = pltpu.PrefetchScalarGridSpec(
        num_scalar_prefetch=1,
        grid=(HEAD_PAIRS, nblk),
        in_specs=[
            pl.BlockSpec((blk, LANES), lambda p, i, c: (i, p)),
            pl.BlockSpec((s, LANES), lambda p, i, c: (0, p)),
            pl.BlockSpec((2 * VT_ROWS, s), lambda p, i, c: (p, 0)),
            pl.BlockSpec((nblk, LANES), lambda p, i, c: (0, p)),
            pl.BlockSpec((MOBA_NEAR, None, blk, 2 * blk), lambda p, i, c: (0, p, 0, 0)),
        ],
        out_specs=pl.BlockSpec((blk, LANES), lambda p, i, c: (i, p)),
        scratch_shapes=[
            pltpu.VMEM((2 * blk, LANES), BF16),
            pltpu.VMEM((nblk, 2 * blk), F32),
            pltpu.VMEM((1, 2 * blk), F32),
            pltpu.VMEM((2, VT_ROWS, blk), F32),
        ],
    )
    return pl.pallas_call(
        functools.partial(_moba_kernel, nblk=nblk),
        grid_spec=grid_spec,
        out_shape=jax.ShapeDtypeStruct((s, MOBA_HEADS * HEAD_DIM), BF16),
        compiler_params=pltpu.CompilerParams(dimension_semantics=("arbitrary", "arbitrary"),
                                             vmem_limit_bytes=VMEM_LIMIT),
        name="moba",
    )(cfar, qa, ka, vat, kmean, bias_t)


def _swa_kernel(sink_ref, q_ref, kprev_ref, kcur_ref, vprev_ref, vcur_ref, bias_ref, o_ref):
    nb = pl.program_id(0)
    w = SWA_WINDOW
    kband = jnp.concatenate([kprev_ref[...], kcur_ref[...]], axis=0)
    vband = jnp.concatenate([vprev_ref[...], vcur_ref[...]], axis=1)
    row = lax.broadcasted_iota(jnp.int32, (2 * w, 1), 0)
    keep = row >= jnp.where(nb > 0, 0, w)
    lane = lax.broadcasted_iota(jnp.int32, (1, LANES), 1)
    col = lax.broadcasted_iota(jnp.int32, (1, 2 * w), 1)
    group = SWA_Q_HEADS // SWA_KV_HEADS
    for pr in range(SWA_Q_HEADS // 2):
        g = (2 * pr) // group
        qp = q_ref[:, pr * LANES:(pr + 1) * LANES]
        zero = jnp.zeros_like(qp)
        qq = jnp.concatenate([jnp.where(lane < HEAD_DIM, qp, zero),
                              jnp.where(lane >= HEAD_DIM, qp, zero)], axis=0)
        kd = kband[:, g * LANES:(g + 1) * LANES]
        st = _nt_dot(kd, qq)
        st = jnp.where(keep, st + bias_ref[pr], NEG_INF)
        sink = jnp.where(col < w, sink_ref[2 * pr], sink_ref[2 * pr + 1])
        m = jnp.maximum(jnp.max(st, axis=0, keepdims=True), sink)
        pt = jnp.exp(st - m).astype(BF16)
        acc = _dot(vband[g * VT_ROWS:(g + 1) * VT_ROWS], pt)
        denom = acc[HEAD_DIM:HEAD_DIM + 1] + jnp.exp(sink - m)
        o = acc[:HEAD_DIM] / denom
        oo = jnp.concatenate([o[:, :w], o[:, w:]], axis=0)
        o_ref[:, pr * LANES:(pr + 1) * LANES] = oo.T.astype(BF16)


def _swa_call(sinks, qb, kbd, vbt, bias_t):
    s = qb.shape[0]
    w = SWA_WINDOW
    nb = s // w
    prev = lambda i, c: (jnp.maximum(i - 1, 0), 0)
    grid_spec = pltpu.PrefetchScalarGridSpec(
        num_scalar_prefetch=1,
        grid=(nb,),
        in_specs=[
            pl.BlockSpec((w, _W_QB), lambda i, c: (i, 0)),
            pl.BlockSpec((w, _W_KB), prev),
            pl.BlockSpec((w, _W_KB), lambda i, c: (i, 0)),
            pl.BlockSpec((SWA_KV_HEADS * VT_ROWS, w), lambda i, c: (0, jnp.maximum(i - 1, 0))),
            pl.BlockSpec((SWA_KV_HEADS * VT_ROWS, w), lambda i, c: (0, i)),
            pl.BlockSpec(bias_t.shape, lambda i, c: (0, 0, 0)),
        ],
        out_specs=pl.BlockSpec((w, _W_QB), lambda i, c: (i, 0)),
    )
    return pl.pallas_call(
        _swa_kernel,
        grid_spec=grid_spec,
        out_shape=jax.ShapeDtypeStruct((s, _W_QB), BF16),
        compiler_params=pltpu.CompilerParams(dimension_semantics=("arbitrary",), vmem_limit_bytes=VMEM_LIMIT),
        name="swa",
    )(sinks, qb, kbd, kbd, vbt, vbt, bias_t)


def _merge_kernel(x_ref, ya_ref, yb_ref, ga_ref, gb_ref, wa_ref, wb_ref, wo_ref, gffn_ref, xo_ref, h2_ref):
    merged = ga_ref[...] * _dot(ya_ref[...], wa_ref[...]) + gb_ref[...] * _dot(yb_ref[...], wb_ref[...])
    xn = x_ref[...] + _dot(merged.astype(BF16), wo_ref[...])
    xo_ref[...] = xn
    h2_ref[...] = _rms_norm_rows(xn, gffn_ref[...]).astype(BF16)


def _merge_call(x, ya, yb, ga, gb, wa, wb, wo, gffn, tm=512):
    s, d = x.shape
    full = lambda a: pl.BlockSpec(a.shape, lambda i: (0,) * a.ndim)
    rows = lambda a: pl.BlockSpec((tm, a.shape[1]), lambda i: (i, 0))
    return pl.pallas_call(
        _merge_kernel,
        grid=(s // tm,),
        in_specs=[rows(x), rows(ya), rows(yb), rows(ga), rows(gb), full(wa), full(wb), full(wo), full(gffn)],
        out_specs=(pl.BlockSpec((tm, d), lambda i: (i, 0)), pl.BlockSpec((tm, d), lambda i: (i, 0))),
        out_shape=(jax.ShapeDtypeStruct((s, d), F32), jax.ShapeDtypeStruct((s, d), BF16)),
        compiler_params=pltpu.CompilerParams(dimension_semantics=("arbitrary",), vmem_limit_bytes=VMEM_LIMIT),
        name="merge",
    )(x, ya, yb, ga, gb, wa, wb, wo, gffn)


def _ffn_kernel(x_ref, h_ref, w1_ref, w2_ref, o_ref, *, tf):
    h = h_ref[...]
    acc = x_ref[...]
    for c in range(w1_ref.shape[1] // tf):
        u = jnp.maximum(_dot(h, w1_ref[:, c * tf:(c + 1) * tf]), 0.0)
        acc = acc + _dot((u * u).astype(BF16), w2_ref[c * tf:(c + 1) * tf, :])
    o_ref[...] = acc


def _ffn_call(x, h2, w1, w2, tm=512, tf=1024):
    s, d = x.shape
    full = lambda a: pl.BlockSpec(a.shape, lambda i: (0,) * a.ndim, pipeline_mode=pl.Buffered(1))
    rows = pl.BlockSpec((tm, d), lambda i: (i, 0))
    return pl.pallas_call(
        functools.partial(_ffn_kernel, tf=tf),
        grid=(s // tm,),
        in_specs=[rows, rows, full(w1), full(w2)],
        out_specs=rows,
        out_shape=jax.ShapeDtypeStruct((s, d), F32),
        compiler_params=pltpu.CompilerParams(dimension_semantics=("arbitrary",), vmem_limit_bytes=VMEM_LIMIT),
        name="ffn",
    )(x, h2, w1, w2)


def _bucket_lookup(table, dist, keep):
    idx = jnp.asarray(np.where(keep, _rel_bucket_np(dist), -1).astype(np.int32))[None]
    out = jnp.full((table.shape[1],) + dist.shape, NEG_INF, F32)
    for b in range(N_REL_BUCKETS):
        out = jnp.where(idx == b, table[b].reshape((-1,) + (1,) * dist.ndim), out)
    return out


def _bias_tables(rel_bias):
    blk, w = MOBA_BLOCK, SWA_WINDOW
    ck = np.arange(blk)[:, None]
    rq = np.arange(blk)[None, :]
    dist = np.stack([d * blk + rq - ck for d in range(MOBA_NEAR)])
    moba = _bucket_lookup(rel_bias[:, :MOBA_HEADS], dist, dist >= 0)
    moba = (moba.reshape(HEAD_PAIRS, 2, MOBA_NEAR, blk, blk).transpose(2, 0, 3, 1, 4)
            .reshape(MOBA_NEAR, HEAD_PAIRS, blk, 2 * blk))
    cfar = rel_bias[N_REL_BUCKETS - 1, :MOBA_HEADS]

    ck = np.arange(2 * w)[:, None]
    rq = np.arange(w)[None, :]
    dist = rq - (ck - w)
    swa = _bucket_lookup(rel_bias[:, MOBA_HEADS:], dist, (dist >= 0) & (dist < w))
    swa = swa.reshape(SWA_Q_HEADS // 2, 2, 2 * w, w).transpose(0, 2, 1, 3).reshape(SWA_Q_HEADS // 2, 2 * w, 2 * w)
    return moba.astype(F32), cfar.astype(F32), swa.astype(F32)


def _layer_weights(w_in_l):
    c = np.cumsum([0, 512, 512, 512, 512, 128, 128, 1024, 1024]).tolist()
    kb = w_in_l[:, c[4]:c[5]]
    kb_dup = jnp.concatenate([kb[:, :HEAD_DIM], kb[:, :HEAD_DIM], kb[:, HEAD_DIM:], kb[:, HEAD_DIM:]], axis=1)
    return jnp.concatenate([w_in_l[:, :c[4]], kb_dup, w_in_l[:, c[5]:]], axis=1).astype(BF16)


def kernel(x, rel_bias, g_mix, w_in, b_gate, q_norm_a, k_norm_a, q_norm_b, k_norm_b, sinks,
           w_branch_a, w_branch_b, w_out, g_ffn, w_ff1, w_ff2):
    b, s, d = x.shape
    assert b == 1 and s % MOBA_BLOCK == 0
    depth = w_in.shape[0]
    bias_moba, cfar, bias_swa = _bias_tables(rel_bias)
    tile = lambda g, n: jnp.tile(g, n)[None, :]
    xs = x[0]
    for l in range(depth):
        qa, ka, kmean, vat, qb, kbd, vbt, ga, gb = _proj_call(
            xs, g_mix[l][None, :], _layer_weights(w_in[l]), b_gate[l][None, :],
            tile(q_norm_a[l], _W_QA // HEAD_DIM), tile(k_norm_a[l], _W_KA // HEAD_DIM),
            tile(q_norm_b[l], _W_QB // HEAD_DIM), tile(k_norm_b[l], _W_KB // HEAD_DIM))
        ya = _moba_call(cfar, qa, ka, vat, kmean[:, 0, :], bias_moba)
        yb = _swa_call(sinks[l], qb, kbd, vbt, bias_swa)
        xs, h2 = _merge_call(xs, ya, yb, ga, gb, w_branch_a[l].astype(BF16), w_branch_b[l].astype(BF16),
                             w_out[l].astype(BF16), g_ffn[l][None, :])
        xs = _ffn_call(xs, h2, w_ff1[l].astype(BF16), w_ff2[l].astype(BF16))
    return xs[None]
```

```python
import functools

import jax
import jax.numpy as jnp
import numpy as np
from jax import lax
from jax.experimental import pallas as pl
from jax.experimental.pallas import tpu as pltpu

HEAD_DIM = 64
MOBA_HEADS = 8
MOBA_BLOCK = 256
MOBA_TOPK = 3
SWA_Q_HEADS = 8
SWA_KV_HEADS = 2
SWA_WINDOW = 128
SWA_STEP_BLOCKS = 2
N_REL_BUCKETS = 32
REL_MAX_DISTANCE = 2048
NORM_EPS = 1e-6
LOG2E = 1.4426950408889634
ATTN_SCALE = HEAD_DIM ** -0.5 * LOG2E

LANES = 128
HEAD_PAIRS = MOBA_HEADS // 2
VT_ROWS = 80
MOBA_NEAR = 7
MOBA_FAR_UNROLL = 4
assert MOBA_FAR_UNROLL == MOBA_NEAR // 2 + 1
VMEM_LIMIT = 56 * 1024 * 1024

BF16 = jnp.bfloat16
F32 = jnp.float32
NEG_INF = float("-inf")
M_INIT = -1e30


def _rel_bucket_np(dist):
    n = np.maximum(dist, 0)
    exact = N_REL_BUCKETS // 2
    nf = np.maximum(n, 1).astype(np.float32)
    large = exact + (np.log(nf / np.float32(exact)) / np.float32(np.log(REL_MAX_DISTANCE / exact))
                     * np.float32(N_REL_BUCKETS - exact)).astype(np.int32)
    large = np.minimum(large, N_REL_BUCKETS - 1)
    return np.where(n < exact, n, large)


def _nt_dot(a, b):
    return lax.dot_general(a, b, (((1,), (1,)), ((), ())), preferred_element_type=F32)


def _dot(a, b):
    return jnp.dot(a, b, preferred_element_type=F32)


def _rms_norm_rows(xf, g):
    ms = jnp.mean(xf * xf, axis=-1, keepdims=True)
    return xf * lax.rsqrt(ms + NORM_EPS) * g


def _head_norm(acc, g2):
    w = acc.shape[1]
    lane = lax.broadcasted_iota(jnp.int32, (1, LANES), 1)
    lo = lane < HEAD_DIM
    outs = []
    for b in range(w // LANES):
        xb = acc[:, b * LANES:(b + 1) * LANES]
        x2 = xb * xb
        s_lo = jnp.sum(jnp.where(lo, x2, 0.0), axis=-1, keepdims=True)
        s_hi = jnp.sum(jnp.where(lo, 0.0, x2), axis=-1, keepdims=True)
        r = jnp.where(lo, lax.rsqrt(s_lo / HEAD_DIM + NORM_EPS), lax.rsqrt(s_hi / HEAD_DIM + NORM_EPS))
        outs.append(xb * r * g2[:, b * LANES:(b + 1) * LANES])
    return outs[0] if len(outs) == 1 else jnp.concatenate(outs, axis=1)


def _v_transposed(v, n_heads):
    rows = v.shape[0]
    vt = v.T
    r = lax.broadcasted_iota(jnp.int32, (VT_ROWS - HEAD_DIM, rows), 0)
    aug = jnp.where(r == 0, 1.0, 0.0).astype(F32)
    parts = []
    for h in range(n_heads):
        parts.append(vt[h * HEAD_DIM:(h + 1) * HEAD_DIM])
        parts.append(aug)
    return jnp.concatenate(parts, axis=0).astype(BF16)


_W_QA, _W_KA, _W_VA, _W_QB, _W_KB, _W_VB, _W_GA, _W_GB = 512, 512, 512, 512, 256, 128, 1024, 1024
_PROJ_COLS = np.cumsum([0, _W_QA, _W_KA, _W_VA, _W_QB, _W_KB, _W_VB, _W_GA, _W_GB]).tolist()


def _proj_kernel(x_ref, gmix_ref, w_ref, bgate_ref, gqa_ref, gka_ref, gqb_ref, gkb_ref,
                 qa_ref, ka_ref, kmean_ref, vat_ref, qb_ref, kb_ref, vbt_ref, ga_ref, gb_ref):
    h = _rms_norm_rows(x_ref[...], gmix_ref[...]).astype(BF16)
    c = _PROJ_COLS

    def seg(k):
        return _dot(h, w_ref[:, c[k]:c[k + 1]])

    qa_ref[...] = (_head_norm(seg(0), gqa_ref[...]) * ATTN_SCALE).astype(BF16)
    ka = _head_norm(seg(1), gka_ref[...])
    ka_ref[...] = ka.astype(BF16)
    kmean_ref[0] = jnp.mean(ka, axis=0, keepdims=True)
    vat_ref[...] = _v_transposed(seg(2), MOBA_HEADS)
    qb_ref[...] = (_head_norm(seg(3), gqb_ref[...]) * ATTN_SCALE).astype(BF16)
    kb_ref[...] = _head_norm(seg(4), gkb_ref[...]).astype(BF16)
    vbt_ref[...] = _v_transposed(seg(5), SWA_KV_HEADS)
    bg = bgate_ref[...]
    d = _W_GA
    ga_ref[...] = 1.0 / (1.0 + jnp.exp(-(seg(6) + bg[:, :d])))
    gb_ref[...] = 1.0 / (1.0 + jnp.exp(-(seg(7) + bg[:, d:])))


def _proj_call(x, gmix, w, bgate, gqa, gka, gqb, gkb):
    s, d = x.shape
    tm = MOBA_BLOCK
    nblk = s // tm
    full = lambda shape: pl.BlockSpec(shape, lambda i: (0,) * len(shape))
    rows = lambda width: pl.BlockSpec((tm, width), lambda i: (i, 0))
    cols = lambda height: pl.BlockSpec((height, tm), lambda i: (0, i))
    out_shape = (
        jax.ShapeDtypeStruct((s, _W_QA), BF16),
        jax.ShapeDtypeStruct((s, _W_KA), BF16),
        jax.ShapeDtypeStruct((nblk, 1, _W_KA), F32),
        jax.ShapeDtypeStruct((MOBA_HEADS * VT_ROWS, s), BF16),
        jax.ShapeDtypeStruct((s, _W_QB), BF16),
        jax.ShapeDtypeStruct((s, _W_KB), BF16),
        jax.ShapeDtypeStruct((SWA_KV_HEADS * VT_ROWS, s), BF16),
        jax.ShapeDtypeStruct((s, _W_GA), F32),
        jax.ShapeDtypeStruct((s, _W_GB), F32),
    )
    out_specs = (
        rows(_W_QA), rows(_W_KA), pl.BlockSpec((1, 1, _W_KA), lambda i: (i, 0, 0)),
        cols(MOBA_HEADS * VT_ROWS), rows(_W_QB), rows(_W_KB), cols(SWA_KV_HEADS * VT_ROWS),
        rows(_W_GA), rows(_W_GB),
    )
    return pl.pallas_call(
        _proj_kernel,
        grid=(nblk,),
        in_specs=[rows(d), full(gmix.shape), full(w.shape), full(bgate.shape),
                  full(gqa.shape), full(gka.shape), full(gqb.shape), full(gkb.shape)],
        out_specs=out_specs,
        out_shape=out_shape,
        compiler_params=pltpu.CompilerParams(dimension_semantics=("arbitrary",), vmem_limit_bytes=VMEM_LIMIT),
        name="proj",
    )(x, gmix, w, bgate, gqa, gka, gqb, gkb)


def _moba_kernel(cfar_ref, q_ref, k_ref, vt_ref, kmean_ref, bias_ref, o_ref,
                 qq_scr, sel_scr, m_scr, acc_scr, sta_scr, stb_scr, *, nblk):
    p = pl.program_id(0)
    i = pl.program_id(1)
    blk = MOBA_BLOCK
    q = q_ref[...]
    lane = lax.broadcasted_iota(jnp.int32, (1, LANES), 1)
    km = kmean_ref[...]
    km_hi = km.astype(BF16)
    km_lo = (km - km_hi.astype(F32)).astype(BF16)
    n_iota = lax.broadcasted_iota(jnp.int32, (nblk, blk), 0)

    for hh in range(2):
        cols = slice(hh * blk, (hh + 1) * blk)
        hmask = (lane < HEAD_DIM) if hh == 0 else (lane >= HEAD_DIM)
        qm = jnp.where(hmask, q, jnp.zeros_like(q))
        qq_scr[cols, :] = qm
        g = _nt_dot(km_hi, qm) + _nt_dot(km_lo, qm)
        g = jnp.where(n_iota < i, g, NEG_INF)
        sel = jnp.zeros((nblk, blk), F32)
        for _ in range(MOBA_TOPK):
            mx = jnp.max(g, axis=0, keepdims=True)
            idx = jnp.min(jnp.where(g == mx, n_iota, nblk), axis=0, keepdims=True)
            hit = n_iota == idx
            valid = jnp.where(mx > NEG_INF, 1.0, 0.0)
            sel = jnp.maximum(sel, jnp.where(hit, valid, 0.0))
            g = jnp.where(hit, NEG_INF, g)
        sel_scr[:, cols] = jnp.where(n_iota == i, 1.0, sel)
        acc_scr[hh] = jnp.zeros((VT_ROWS, blk), F32)
    m_scr[...] = jnp.full((1, 2 * blk), M_INIT, F32)

    col = lax.broadcasted_iota(jnp.int32, (1, 2 * blk), 1)
    cfar = jnp.where(col < blk, cfar_ref[2 * p], cfar_ref[2 * p + 1])
    n_far = jnp.maximum(i - (MOBA_NEAR - 1), 0)

    def scores(first, nb, st_ref):
        off = pl.multiple_of(first * blk, blk)
        st_ref[:nb * blk, :] = _nt_dot(k_ref[pl.ds(off, nb * blk), :], qq_scr[...])

    def attend(first, nb, st_ref, near):
        ms, pvs = [], []
        for u in range(nb):
            j = first + u
            s_u = st_ref[u * blk:(u + 1) * blk, :]
            if near:
                s_u = s_u + bias_ref[jnp.clip(i - j, 0, MOBA_NEAR - 1)]
                on = sel_scr[pl.ds(j, 1), :] > 0.5
                m_u = jnp.max(s_u, axis=0, keepdims=True)
                ms.append(jnp.where(on, m_u, NEG_INF))
            else:
                on = sel_scr[pl.ds(j, 1), :] > jnp.where(j < n_far, 0.5, 2.0)
                m_u = jnp.max(s_u, axis=0, keepdims=True)
                ms.append(jnp.where(on, m_u + cfar, NEG_INF))
            pt = jnp.exp2(s_u - m_u).astype(BF16)
            ko = pl.multiple_of(j * blk, blk)
            pvs.append([_dot(vt_ref[hh * VT_ROWS:(hh + 1) * VT_ROWS, pl.ds(ko, blk)], pt[:, hh * blk:(hh + 1) * blk])
                        for hh in range(2)])
        m_old = m_scr[...]
        m_new = m_old
        for m_u in ms:
            m_new = jnp.maximum(m_new, m_u)
        alpha = jnp.exp2(m_old - m_new)
        ws = [jnp.exp2(m_u - m_new) for m_u in ms]
        for hh in range(2):
            cols = slice(hh * blk, (hh + 1) * blk)
            acc = acc_scr[hh] * alpha[:, cols]
            for w_u, pv in zip(ws, pvs):
                acc = acc + pv[hh] * w_u[:, cols]
            acc_scr[hh] = acc
        m_scr[...] = m_new

    fu = MOBA_FAR_UNROLL
    n_groups = (n_far + fu - 1) // fu
    near0 = jnp.maximum(i - (MOBA_NEAR - 1), 0)
    nb1 = MOBA_NEAR - fu

    def item_start(g):
        return jnp.where(g < n_groups, g * fu, near0)

    def near_tail(st_x, st_y):
        scores(near0 + fu, nb1, st_y)
        attend(near0, fu, st_x, True)
        attend(near0 + fu, nb1, st_y, True)

    scores(item_start(0), fu, sta_scr)

    def pair_body(t, carry):
        scores((2 * t + 1) * fu, fu, stb_scr)
        attend(2 * t * fu, fu, sta_scr, False)
        scores(item_start(2 * t + 2), fu, sta_scr)
        attend((2 * t + 1) * fu, fu, stb_scr, False)
        return carry

    lax.fori_loop(0, n_groups // 2, pair_body, 0)

    @pl.when(n_groups % 2 == 0)
    def _():
        near_tail(sta_scr, stb_scr)

    @pl.when(n_groups % 2 == 1)
    def _():
        scores(near0, fu, stb_scr)
        attend((n_groups - 1) * fu, fu, sta_scr, False)
        near_tail(stb_scr, sta_scr)

    outs = []
    for hh in range(2):
        a = acc_scr[hh]
        outs.append(a[:HEAD_DIM] / a[HEAD_DIM:HEAD_DIM + 1])
    o_ref[...] = jnp.concatenate(outs, axis=0).T.astype(BF16)


def _moba_call(cfar, qa, ka, vat, kmean, bias_t):
    s = qa.shape[0]
    blk = MOBA_BLOCK
    nblk = s // blk
    grid_spec = pltpu.PrefetchScalarGridSpec(
        num_scalar_prefetch=1,
        grid=(HEAD_PAIRS, nblk),
        in_specs=[
            pl.BlockSpec((blk, LANES), lambda p, i, c: (i, p)),
            pl.BlockSpec((s, LANES), lambda p, i, c: (0, p)),
            pl.BlockSpec((2 * VT_ROWS, s), lambda p, i, c: (p, 0)),
            pl.BlockSpec((nblk, LANES), lambda p, i, c: (0, p)),
            pl.BlockSpec((MOBA_NEAR, None, blk, 2 * blk), lambda p, i, c: (0, p, 0, 0)),
        ],
        out_specs=pl.BlockSpec((blk, LANES), lambda p, i, c: (i, p)),
        scratch_shapes=[
            pltpu.VMEM((2 * blk, LANES), BF16),
            pltpu.VMEM((nblk, 2 * blk), F32),
            pltpu.VMEM((1, 2 * blk), F32),
            pltpu.VMEM((2, VT_ROWS, blk), F32),
            pltpu.VMEM((MOBA_FAR_UNROLL * blk, 2 * blk), F32),
            pltpu.VMEM((MOBA_FAR_UNROLL * blk, 2 * blk), F32),
        ],
    )
    return pl.pallas_call(
        functools.partial(_moba_kernel, nblk=nblk),
        grid_spec=grid_spec,
        out_shape=jax.ShapeDtypeStruct((s, MOBA_HEADS * HEAD_DIM), BF16),
        compiler_params=pltpu.CompilerParams(dimension_semantics=("arbitrary", "arbitrary"),
                                             vmem_limit_bytes=VMEM_LIMIT),
        name="moba",
    )(cfar, qa, ka, vat, kmean, bias_t)


def _swa_kernel(sink_ref, q_ref, kprev_ref, kcur_ref, vprev_ref, vcur_ref, bias_ref, o_ref):
    b = pl.program_id(0)
    w = SWA_WINDOW
    kall = jnp.concatenate([kprev_ref[...], kcur_ref[...]], axis=0)
    vall = jnp.concatenate([vprev_ref[...], vcur_ref[...]], axis=1)
    row = lax.broadcasted_iota(jnp.int32, (2 * w, 1), 0)
    keep = row >= jnp.where(b > 0, 0, w)
    lane = lax.broadcasted_iota(jnp.int32, (1, LANES), 1)
    col = lax.broadcasted_iota(jnp.int32, (1, 2 * w), 1)
    group = SWA_Q_HEADS // SWA_KV_HEADS
    chains = [(sb, pr) for sb in range(SWA_STEP_BLOCKS) for pr in range(SWA_Q_HEADS // 2)]
    sts = []
    for sb, pr in chains:
        g = (2 * pr) // group
        qp = q_ref[sb * w:(sb + 1) * w, pr * LANES:(pr + 1) * LANES]
        zero = jnp.zeros_like(qp)
        qq = jnp.concatenate([jnp.where(lane < HEAD_DIM, qp, zero),
                              jnp.where(lane >= HEAD_DIM, qp, zero)], axis=0)
        kd = kall[sb * w:(sb + 2) * w, g * LANES:(g + 1) * LANES]
        sts.append(_nt_dot(kd, qq))
    for (sb, pr), st in zip(chains, sts):
        g = (2 * pr) // group
        st = st + bias_ref[pr]
        if sb == 0:
            st = jnp.where(keep, st, NEG_INF)
        sink = jnp.where(col < w, sink_ref[2 * pr], sink_ref[2 * pr + 1])
        m = jnp.maximum(jnp.max(st, axis=0, keepdims=True), sink)
        pt = jnp.exp2(st - m).astype(BF16)
        acc = _dot(vall[g * VT_ROWS:(g + 1) * VT_ROWS, sb * w:(sb + 2) * w], pt)
        denom = acc[HEAD_DIM:HEAD_DIM + 1] + jnp.exp2(sink - m)
        o = acc[:HEAD_DIM] / denom
        oo = jnp.concatenate([o[:, :w], o[:, w:]], axis=0)
        o_ref[sb * w:(sb + 1) * w, pr * LANES:(pr + 1) * LANES] = oo.T.astype(BF16)


def _swa_call(sinks, qb, kbd, vbt, bias_t):
    s = qb.shape[0]
    w = SWA_WINDOW
    n = SWA_STEP_BLOCKS
    assert s % (n * w) == 0
    prev_blk = lambda i: jnp.maximum(n * i - 1, 0)
    grid_spec = pltpu.PrefetchScalarGridSpec(
        num_scalar_prefetch=1,
        grid=(s // (n * w),),
        in_specs=[
            pl.BlockSpec((n * w, _W_QB), lambda i, c: (i, 0)),
            pl.BlockSpec((w, _W_KB), lambda i, c: (prev_blk(i), 0)),
            pl.BlockSpec((n * w, _W_KB), lambda i, c: (i, 0)),
            pl.BlockSpec((SWA_KV_HEADS * VT_ROWS, w), lambda i, c: (0, prev_blk(i))),
            pl.BlockSpec((SWA_KV_HEADS * VT_ROWS, n * w), lambda i, c: (0, i)),
            pl.BlockSpec(bias_t.shape, lambda i, c: (0, 0, 0)),
        ],
        out_specs=pl.BlockSpec((n * w, _W_QB), lambda i, c: (i, 0)),
    )
    return pl.pallas_call(
        _swa_kernel,
        grid_spec=grid_spec,
        out_shape=jax.ShapeDtypeStruct((s, _W_QB), BF16),
        compiler_params=pltpu.CompilerParams(dimension_semantics=("arbitrary",), vmem_limit_bytes=VMEM_LIMIT),
        name="swa",
    )(sinks, qb, kbd, kbd, vbt, vbt, bias_t)


def _merge_kernel(x_ref, ya_ref, yb_ref, ga_ref, gb_ref, wa_ref, wb_ref, wo_ref, gffn_ref, xo_ref, h2_ref):
    merged = ga_ref[...] * _dot(ya_ref[...], wa_ref[...]) + gb_ref[...] * _dot(yb_ref[...], wb_ref[...])
    xn = x_ref[...] + _dot(merged.astype(BF16), wo_ref[...])
    xo_ref[...] = xn
    h2_ref[...] = _rms_norm_rows(xn, gffn_ref[...]).astype(BF16)


def _merge_call(x, ya, yb, ga, gb, wa, wb, wo, gffn, tm=512):
    s, d = x.shape
    full = lambda a: pl.BlockSpec(a.shape, lambda i: (0,) * a.ndim)
    rows = lambda a: pl.BlockSpec((tm, a.shape[1]), lambda i: (i, 0))
    return pl.pallas_call(
        _merge_kernel,
        grid=(s // tm,),
        in_specs=[rows(x), rows(ya), rows(yb), rows(ga), rows(gb), full(wa), full(wb), full(wo), full(gffn)],
        out_specs=(pl.BlockSpec((tm, d), lambda i: (i, 0)), pl.BlockSpec((tm, d), lambda i: (i, 0))),
        out_shape=(jax.ShapeDtypeStruct((s, d), F32), jax.ShapeDtypeStruct((s, d), BF16)),
        compiler_params=pltpu.CompilerParams(dimension_semantics=("arbitrary",), vmem_limit_bytes=VMEM_LIMIT),
        name="merge",
    )(x, ya, yb, ga, gb, wa, wb, wo, gffn)


def _ffn_kernel(x_ref, h_ref, w1_ref, w2_ref, o_ref, *, tf):
    h = h_ref[...]
    acc = x_ref[...]
    for c in range(w1_ref.shape[1] // tf):
        u = jnp.maximum(_dot(h, w1_ref[:, c * tf:(c + 1) * tf]), 0.0)
        acc = acc + _dot((u * u).astype(BF16), w2_ref[c * tf:(c + 1) * tf, :])
    o_ref[...] = acc


def _ffn_call(x, h2, w1, w2, tm=512, tf=1024):
    s, d = x.shape
    full = lambda a: pl.BlockSpec(a.shape, lambda i: (0,) * a.ndim, pipeline_mode=pl.Buffered(1))
    rows = pl.BlockSpec((tm, d), lambda i: (i, 0))
    return pl.pallas_call(
        functools.partial(_ffn_kernel, tf=tf),
        grid=(s // tm,),
        in_specs=[rows, rows, full(w1), full(w2)],
        out_specs=rows,
        out_shape=jax.ShapeDtypeStruct((s, d), F32),
        compiler_params=pltpu.CompilerParams(dimension_semantics=("arbitrary",), vmem_limit_bytes=VMEM_LIMIT),
        name="ffn",
    )(x, h2, w1, w2)


def _bucket_lookup(table, dist, keep):
    idx = jnp.asarray(np.where(keep, _rel_bucket_np(dist), -1).astype(np.int32))[None]
    out = jnp.full((table.shape[1],) + dist.shape, NEG_INF, F32)
    for b in range(N_REL_BUCKETS):
        out = jnp.where(idx == b, table[b].reshape((-1,) + (1,) * dist.ndim), out)
    return out


def _bias_tables(rel_bias):
    blk, w = MOBA_BLOCK, SWA_WINDOW
    ck = np.arange(blk)[:, None]
    rq = np.arange(blk)[None, :]
    dist = np.stack([d * blk + rq - ck for d in range(MOBA_NEAR)])
    moba = _bucket_lookup(rel_bias[:, :MOBA_HEADS], dist, dist >= 0)
    moba = (moba.reshape(HEAD_PAIRS, 2, MOBA_NEAR, blk, blk).transpose(2, 0, 3, 1, 4)
            .reshape(MOBA_NEAR, HEAD_PAIRS, blk, 2 * blk))
    cfar = rel_bias[N_REL_BUCKETS - 1, :MOBA_HEADS]

    ck = np.arange(2 * w)[:, None]
    rq = np.arange(w)[None, :]
    dist = rq - (ck - w)
    swa = _bucket_lookup(rel_bias[:, MOBA_HEADS:], dist, (dist >= 0) & (dist < w))
    swa = swa.reshape(SWA_Q_HEADS // 2, 2, 2 * w, w).transpose(0, 2, 1, 3).reshape(SWA_Q_HEADS // 2, 2 * w, 2 * w)
    return (moba * LOG2E).astype(F32), (cfar * LOG2E).astype(F32), (swa * LOG2E).astype(F32)


def _layer_weights(w_in_l):
    c = np.cumsum([0, 512, 512, 512, 512, 128, 128, 1024, 1024]).tolist()
    kb = w_in_l[:, c[4]:c[5]]
    kb_dup = jnp.concatenate([kb[:, :HEAD_DIM], kb[:, :HEAD_DIM], kb[:, HEAD_DIM:], kb[:, HEAD_DIM:]], axis=1)
    return jnp.concatenate([w_in_l[:, :c[4]], kb_dup, w_in_l[:, c[5]:]], axis=1).astype(BF16)


def kernel(x, rel_bias, g_mix, w_in, b_gate, q_norm_a, k_norm_a, q_norm_b, k_norm_b, sinks,
           w_branch_a, w_branch_b, w_out, g_ffn, w_ff1, w_ff2):
    b, s, d = x.shape
    assert b == 1 and s % MOBA_BLOCK == 0
    depth = w_in.shape[0]
    bias_moba, cfar, bias_swa = _bias_tables(rel_bias)
    tile = lambda g, n: jnp.tile(g, n)[None, :]
    xs = x[0]
    for l in range(depth):
        qa, ka, kmean, vat, qb, kbd, vbt, ga, gb = _proj_call(
            xs, g_mix[l][None, :], _layer_weights(w_in[l]), b_gate[l][None, :],
            tile(q_norm_a[l], _W_QA // HEAD_DIM), tile(k_norm_a[l], _W_KA // HEAD_DIM),
            tile(q_norm_b[l], _W_QB // HEAD_DIM), tile(k_norm_b[l], _W_KB // HEAD_DIM))
        ya = _moba_call(cfar, qa, ka, vat, kmean[:, 0, :], bias_moba)
        yb = _swa_call(sinks[l] * LOG2E, qb, kbd, vbt, bias_swa)
        xs, h2 = _merge_call(xs, ya, yb, ga, gb, w_branch_a[l].astype(BF16), w_branch_b[l].astype(BF16),
                             w_out[l].astype(BF16), g_ffn[l][None, :])
        xs = _ffn_call(xs, h2, w_ff1[l].astype(BF16), w_ff2[l].astype(BF16))
    return xs[None]
```

```python
import functools

import jax
import jax.numpy as jnp
import numpy as np
from jax import lax
from jax.experimental import pallas as pl
from jax.experimental.pallas import tpu as pltpu

HEAD_DIM = 64
MOBA_HEADS = 8
MOBA_BLOCK = 256
MOBA_TOPK = 3
SWA_Q_HEADS = 8
SWA_KV_HEADS = 2
SWA_WINDOW = 128
SWA_STEP_BLOCKS = 2
N_REL_BUCKETS = 32
REL_MAX_DISTANCE = 2048
NORM_EPS = 1e-6
LOG2E = 1.4426950408889634
ATTN_SCALE = HEAD_DIM ** -0.5 * LOG2E

LANES = 128
HEAD_PAIRS = MOBA_HEADS // 2
VT_ROWS = 80
MOBA_NEAR = 7
MOBA_FAR_UNROLL = 4
assert MOBA_FAR_UNROLL == MOBA_NEAR // 2 + 1
VMEM_LIMIT = 56 * 1024 * 1024

BF16 = jnp.bfloat16
F32 = jnp.float32
NEG_INF = float("-inf")
M_INIT = -1e30


def _rel_bucket_np(dist):
    n = np.maximum(dist, 0)
    exact = N_REL_BUCKETS // 2
    nf = np.maximum(n, 1).astype(np.float32)
    large = exact + (np.log(nf / np.float32(exact)) / np.float32(np.log(REL_MAX_DISTANCE / exact))
                     * np.float32(N_REL_BUCKETS - exact)).astype(np.int32)
    large = np.minimum(large, N_REL_BUCKETS - 1)
    return np.where(n < exact, n, large)


def _nt_dot(a, b):
    return lax.dot_general(a, b, (((1,), (1,)), ((), ())), preferred_element_type=F32)


def _dot(a, b):
    return jnp.dot(a, b, preferred_element_type=F32)


def _rms_norm_rows(xf, g):
    ms = jnp.mean(xf * xf, axis=-1, keepdims=True)
    return xf * lax.rsqrt(ms + NORM_EPS) * g


def _head_norm(acc, g2):
    w = acc.shape[1]
    lane = lax.broadcasted_iota(jnp.int32, (1, LANES), 1)
    lo = lane < HEAD_DIM
    outs = []
    for b in range(w // LANES):
        xb = acc[:, b * LANES:(b + 1) * LANES]
        x2 = xb * xb
        s_lo = jnp.sum(jnp.where(lo, x2, 0.0), axis=-1, keepdims=True)
        s_hi = jnp.sum(jnp.where(lo, 0.0, x2), axis=-1, keepdims=True)
        r = jnp.where(lo, lax.rsqrt(s_lo / HEAD_DIM + NORM_EPS), lax.rsqrt(s_hi / HEAD_DIM + NORM_EPS))
        outs.append(xb * r * g2[:, b * LANES:(b + 1) * LANES])
    return outs[0] if len(outs) == 1 else jnp.concatenate(outs, axis=1)


def _v_transposed(v, n_heads):
    rows = v.shape[0]
    vt = v.T
    r = lax.broadcasted_iota(jnp.int32, (VT_ROWS - HEAD_DIM, rows), 0)
    aug = jnp.where(r == 0, 1.0, 0.0).astype(F32)
    parts = []
    for h in range(n_heads):
        parts.append(vt[h * HEAD_DIM:(h + 1) * HEAD_DIM])
        parts.append(aug)
    return jnp.concatenate(parts, axis=0).astype(BF16)


_W_QA, _W_KA, _W_VA, _W_QB, _W_KB, _W_VB, _W_GA, _W_GB = 512, 512, 512, 512, 256, 128, 1024, 1024
_PROJ_COLS = np.cumsum([0, _W_QA, _W_KA, _W_VA, _W_QB, _W_KB, _W_VB, _W_GA, _W_GB]).tolist()


def _proj_kernel(x_ref, gmix_ref, w_ref, bgate_ref, gqa_ref, gka_ref, gqb_ref, gkb_ref,
                 qa_ref, ka_ref, kmean_ref, vat_ref, qb_ref, kb_ref, vbt_ref, ga_ref, gb_ref):
    h = _rms_norm_rows(x_ref[...], gmix_ref[...]).astype(BF16)
    c = _PROJ_COLS

    def seg(k):
        return _dot(h, w_ref[:, c[k]:c[k + 1]])

    qa_ref[...] = (_head_norm(seg(0), gqa_ref[...]) * ATTN_SCALE).astype(BF16)
    ka = _head_norm(seg(1), gka_ref[...])
    ka_ref[...] = ka.astype(BF16)
    kmean_ref[0] = jnp.mean(ka, axis=0, keepdims=True)
    vat_ref[...] = _v_transposed(seg(2), MOBA_HEADS)
    qb_ref[...] = (_head_norm(seg(3), gqb_ref[...]) * ATTN_SCALE).astype(BF16)
    kb_ref[...] = _head_norm(seg(4), gkb_ref[...]).astype(BF16)
    vbt_ref[...] = _v_transposed(seg(5), SWA_KV_HEADS)
    bg = bgate_ref[...]
    d = _W_GA
    ga_ref[...] = 1.0 / (1.0 + jnp.exp(-(seg(6) + bg[:, :d])))
    gb_ref[...] = 1.0 / (1.0 + jnp.exp(-(seg(7) + bg[:, d:])))


def _proj_call(x, gmix, w, bgate, gqa, gka, gqb, gkb):
    s, d = x.shape
    tm = MOBA_BLOCK
    nblk = s // tm
    full = lambda shape: pl.BlockSpec(shape, lambda i: (0,) * len(shape))
    rows = lambda width: pl.BlockSpec((tm, width), lambda i: (i, 0))
    cols = lambda height: pl.BlockSpec((height, tm), lambda i: (0, i))
    out_shape = (
        jax.ShapeDtypeStruct((s, _W_QA), BF16),
        jax.ShapeDtypeStruct((s, _W_KA), BF16),
        jax.ShapeDtypeStruct((nblk, 1, _W_KA), F32),
        jax.ShapeDtypeStruct((MOBA_HEADS * VT_ROWS, s), BF16),
        jax.ShapeDtypeStruct((s, _W_QB), BF16),
        jax.ShapeDtypeStruct((s, _W_KB), BF16),
        jax.ShapeDtypeStruct((SWA_KV_HEADS * VT_ROWS, s), BF16),
        jax.ShapeDtypeStruct((s, _W_GA), F32),
        jax.ShapeDtypeStruct((s, _W_GB), F32),
    )
    out_specs = (
        rows(_W_QA), rows(_W_KA), pl.BlockSpec((1, 1, _W_KA), lambda i: (i, 0, 0)),
        cols(MOBA_HEADS * VT_ROWS), rows(_W_QB), rows(_W_KB), cols(SWA_KV_HEADS * VT_ROWS),
        rows(_W_GA), rows(_W_GB),
    )
    return pl.pallas_call(
        _proj_kernel,
        grid=(nblk,),
        in_specs=[rows(d), full(gmix.shape), full(w.shape), full(bgate.shape),
                  full(gqa.shape), full(gka.shape), full(gqb.shape), full(gkb.shape)],
        out_specs=out_specs,
        out_shape=out_shape,
        compiler_params=pltpu.CompilerParams(dimension_semantics=("arbitrary",), vmem_limit_bytes=VMEM_LIMIT),
        name="proj",
    )(x, gmix, w, bgate, gqa, gka, gqb, gkb)


def _moba_kernel(cfar_ref, q_ref, k_ref, vt_ref, kmean_ref, bias_ref, o_ref,
                 qq_scr, sel_scr, m_scr, acc_scr, sta_scr, stb_scr, *, nblk):
    p = pl.program_id(0)
    i = pl.program_id(1)
    blk = MOBA_BLOCK
    q = q_ref[...]
    lane = lax.broadcasted_iota(jnp.int32, (1, LANES), 1)
    km = kmean_ref[...]
    km_hi = km.astype(BF16)
    km_lo = (km - km_hi.astype(F32)).astype(BF16)
    n_iota = lax.broadcasted_iota(jnp.int32, (nblk, blk), 0)

    qms = []
    for hh in range(2):
        hmask = (lane < HEAD_DIM) if hh == 0 else (lane >= HEAD_DIM)
        qms.append(jnp.where(hmask, q, jnp.zeros_like(q)))
        qq_scr[hh * blk:(hh + 1) * blk, :] = qms[hh]

    gates = [_nt_dot(km_hi, qms[hh]) + _nt_dot(km_lo, qms[hh]) for hh in range(2)]

    def select_blocks():
        for hh in range(2):
            cols = slice(hh * blk, (hh + 1) * blk)
            g = jnp.where(n_iota < i, gates[hh], NEG_INF)
            sel = jnp.zeros((nblk, blk), F32)
            for _ in range(MOBA_TOPK):
                mx = jnp.max(g, axis=0, keepdims=True)
                idx = jnp.min(jnp.where(g == mx, n_iota, nblk), axis=0, keepdims=True)
                hit = n_iota == idx
                valid = jnp.where(mx > NEG_INF, 1.0, 0.0)
                sel = jnp.maximum(sel, jnp.where(hit, valid, 0.0))
                g = jnp.where(hit, NEG_INF, g)
            sel_scr[:, cols] = jnp.where(n_iota == i, 1.0, sel)
            acc_scr[hh] = jnp.zeros((VT_ROWS, blk), F32)
        m_scr[...] = jnp.full((1, 2 * blk), M_INIT, F32)

    col = lax.broadcasted_iota(jnp.int32, (1, 2 * blk), 1)
    cfar = jnp.where(col < blk, cfar_ref[2 * p], cfar_ref[2 * p + 1])
    n_far = jnp.maximum(i - (MOBA_NEAR - 1), 0)

    def scores(first, nb, st_ref):
        off = pl.multiple_of(first * blk, blk)
        st_ref[:nb * blk, :] = _nt_dot(k_ref[pl.ds(off, nb * blk), :], qq_scr[...])

    def attend(first, nb, st_ref, near):
        ms, pvs = [], []
        for u in range(nb):
            j = first + u
            s_u = st_ref[u * blk:(u + 1) * blk, :]
            if near:
                s_u = s_u + bias_ref[jnp.clip(i - j, 0, MOBA_NEAR - 1)]
                on = sel_scr[pl.ds(j, 1), :] > 0.5
                m_u = jnp.max(s_u, axis=0, keepdims=True)
                ms.append(jnp.where(on, m_u, NEG_INF))
            else:
                on = sel_scr[pl.ds(j, 1), :] > jnp.where(j < n_far, 0.5, 2.0)
                m_u = jnp.max(s_u, axis=0, keepdims=True)
                ms.append(jnp.where(on, m_u + cfar, NEG_INF))
            pt = jnp.exp2(s_u - m_u).astype(BF16)
            ko = pl.multiple_of(j * blk, blk)
            pvs.append([_dot(vt_ref[hh * VT_ROWS:(hh + 1) * VT_ROWS, pl.ds(ko, blk)], pt[:, hh * blk:(hh + 1) * blk])
                        for hh in range(2)])
        m_old = m_scr[...]
        m_new = m_old
        for m_u in ms:
            m_new = jnp.maximum(m_new, m_u)
        alpha = jnp.exp2(m_old - m_new)
        ws = [jnp.exp2(m_u - m_new) for m_u in ms]
        for hh in range(2):
            cols = slice(hh * blk, (hh + 1) * blk)
            acc = acc_scr[hh] * alpha[:, cols]
            for w_u, pv in zip(ws, pvs):
                acc = acc + pv[hh] * w_u[:, cols]
            acc_scr[hh] = acc
        m_scr[...] = m_new

    fu = MOBA_FAR_UNROLL
    n_groups = (n_far + fu - 1) // fu
    near0 = jnp.maximum(i - (MOBA_NEAR - 1), 0)
    nb1 = MOBA_NEAR - fu

    def item_start(g):
        return jnp.where(g < n_groups, g * fu, near0)

    def near_tail(st_x, st_y):
        scores(near0 + fu, nb1, st_y)
        attend(near0, fu, st_x, True)
        attend(near0 + fu, nb1, st_y, True)

    scores(item_start(0), fu, sta_scr)
    select_blocks()

    def pair_body(t, carry):
        scores((2 * t + 1) * fu, fu, stb_scr)
        attend(2 * t * fu, fu, sta_scr, False)
        scores(item_start(2 * t + 2), fu, sta_scr)
        attend((2 * t + 1) * fu, fu, stb_scr, False)
        return carry

    lax.fori_loop(0, n_groups // 2, pair_body, 0)

    @pl.when(n_groups % 2 == 0)
    def _():
        near_tail(sta_scr, stb_scr)

    @pl.when(n_groups % 2 == 1)
    def _():
        scores(near0, fu, stb_scr)
        attend((n_groups - 1) * fu, fu, sta_scr, False)
        near_tail(stb_scr, sta_scr)

    outs = []
    for hh in range(2):
        a = acc_scr[hh]
        outs.append(a[:HEAD_DIM] / a[HEAD_DIM:HEAD_DIM + 1])
    o_ref[...] = jnp.concatenate(outs, axis=0).T.astype(BF16)


def _moba_call(cfar, qa, ka, vat, kmean, bias_t):
    s = qa.shape[0]
    blk = MOBA_BLOCK
    nblk = s // blk
    grid_spec = pltpu.PrefetchScalarGridSpec(
        num_scalar_prefetch=1,
        grid=(HEAD_PAIRS, nblk),
        in_specs=[
            pl.BlockSpec((blk, LANES), lambda p, i, c: (i, p)),
            pl.BlockSpec((s, LANES), lambda p, i, c: (0, p)),
            pl.BlockSpec((2 * VT_ROWS, s), lambda p, i, c: (p, 0)),
            pl.BlockSpec((nblk, LANES), lambda p, i, c: (0, p)),
            pl.BlockSpec((MOBA_NEAR, None, blk, 2 * blk), lambda p, i, c: (0, p, 0, 0)),
        ],
        out_specs=pl.BlockSpec((blk, LANES), lambda p, i, c: (i, p)),
        scratch_shapes=[
            pltpu.VMEM((2 * blk, LANES), BF16),
            pltpu.VMEM((nblk, 2 * blk), F32),
            pltpu.VMEM((1, 2 * blk), F32),
            pltpu.VMEM((2, VT_ROWS, blk), F32),
            pltpu.VMEM((MOBA_FAR_UNROLL * blk, 2 * blk), F32),
            pltpu.VMEM((MOBA_FAR_UNROLL * blk, 2 * blk), F32),
        ],
    )
    return pl.pallas_call(
        functools.partial(_moba_kernel, nblk=nblk),
        grid_spec=grid_spec,
        out_shape=jax.ShapeDtypeStruct((s, MOBA_HEADS * HEAD_DIM), BF16),
        compiler_params=pltpu.CompilerParams(dimension_semantics=("arbitrary", "arbitrary"),
                                             vmem_limit_bytes=VMEM_LIMIT),
        name="moba",
    )(cfar, qa, ka, vat, kmean, bias_t)


def _swa_kernel(sink_ref, q_ref, kprev_ref, kcur_ref, vprev_ref, vcur_ref, bias_ref, o_ref, st_scr):
    b = pl.program_id(0)
    w = SWA_WINDOW
    kall = jnp.concatenate([kprev_ref[...], kcur_ref[...]], axis=0)
    vall = jnp.concatenate([vprev_ref[...], vcur_ref[...]], axis=1)
    row = lax.broadcasted_iota(jnp.int32, (2 * w, 1), 0)
    keep = row >= jnp.where(b > 0, 0, w)
    lane = lax.broadcasted_iota(jnp.int32, (1, LANES), 1)
    col = lax.broadcasted_iota(jnp.int32, (1, 2 * w), 1)
    group = SWA_Q_HEADS // SWA_KV_HEADS
    chains = [(sb, pr) for sb in range(SWA_STEP_BLOCKS) for pr in range(SWA_Q_HEADS // 2)]
    for c, (sb, pr) in enumerate(chains):
        g = (2 * pr) // group
        qp = q_ref[sb * w:(sb + 1) * w, pr * LANES:(pr + 1) * LANES]
        zero = jnp.zeros_like(qp)
        qq = jnp.concatenate([jnp.where(lane < HEAD_DIM, qp, zero),
                              jnp.where(lane >= HEAD_DIM, qp, zero)], axis=0)
        kd = kall[sb * w:(sb + 2) * w, g * LANES:(g + 1) * LANES]
        st_scr[c] = _nt_dot(kd, qq)
    for c, (sb, pr) in enumerate(chains):
        g = (2 * pr) // group
        st = st_scr[c] + bias_ref[pr]
        if sb == 0:
            st = jnp.where(keep, st, NEG_INF)
        sink = jnp.where(col < w, sink_ref[2 * pr], sink_ref[2 * pr + 1])
        m = jnp.maximum(jnp.max(st, axis=0, keepdims=True), sink)
        pt = jnp.exp2(st - m).astype(BF16)
        acc = _dot(vall[g * VT_ROWS:(g + 1) * VT_ROWS, sb * w:(sb + 2) * w], pt)
        denom = acc[HEAD_DIM:HEAD_DIM + 1] + jnp.exp2(sink - m)
        o = acc[:HEAD_DIM] / denom
        oo = jnp.concatenate([o[:, :w], o[:, w:]], axis=0)
        o_ref[sb * w:(sb + 1) * w, pr * LANES:(pr + 1) * LANES] = oo.T.astype(BF16)


def _swa_call(sinks, qb, kbd, vbt, bias_t):
    s = qb.shape[0]
    w = SWA_WINDOW
    n = SWA_STEP_BLOCKS
    assert s % (n * w) == 0
    prev_blk = lambda i: jnp.maximum(n * i - 1, 0)
    grid_spec = pltpu.PrefetchScalarGridSpec(
        num_scalar_prefetch=1,
        grid=(s // (n * w),),
        in_specs=[
            pl.BlockSpec((n * w, _W_QB), lambda i, c: (i, 0)),
            pl.BlockSpec((w, _W_KB), lambda i, c: (prev_blk(i), 0)),
            pl.BlockSpec((n * w, _W_KB), lambda i, c: (i, 0)),
            pl.BlockSpec((SWA_KV_HEADS * VT_ROWS, w), lambda i, c: (0, prev_blk(i))),
            pl.BlockSpec((SWA_KV_HEADS * VT_ROWS, n * w), lambda i, c: (0, i)),
            pl.BlockSpec(bias_t.shape, lambda i, c: (0, 0, 0)),
        ],
        out_specs=pl.BlockSpec((n * w, _W_QB), lambda i, c: (i, 0)),
        scratch_shapes=[pltpu.VMEM((n * SWA_Q_HEADS // 2, 2 * w, 2 * w), F32)],
    )
    return pl.pallas_call(
        _swa_kernel,
        grid_spec=grid_spec,
        out_shape=jax.ShapeDtypeStruct((s, _W_QB), BF16),
        compiler_params=pltpu.CompilerParams(dimension_semantics=("arbitrary",), vmem_limit_bytes=VMEM_LIMIT),
        name="swa",
    )(sinks, qb, kbd, kbd, vbt, vbt, bias_t)


def _merge_kernel(x_ref, ya_ref, yb_ref, ga_ref, gb_ref, wa_ref, wb_ref, wo_ref, gffn_ref, xo_ref, h2_ref):
    merged = ga_ref[...] * _dot(ya_ref[...], wa_ref[...]) + gb_ref[...] * _dot(yb_ref[...], wb_ref[...])
    xn = x_ref[...] + _dot(merged.astype(BF16), wo_ref[...])
    xo_ref[...] = xn
    h2_ref[...] = _rms_norm_rows(xn, gffn_ref[...]).astype(BF16)


def _merge_call(x, ya, yb, ga, gb, wa, wb, wo, gffn, tm=512):
    s, d = x.shape
    full = lambda a: pl.BlockSpec(a.shape, lambda i: (0,) * a.ndim)
    rows = lambda a: pl.BlockSpec((tm, a.shape[1]), lambda i: (i, 0))
    return pl.pallas_call(
        _merge_kernel,
        grid=(s // tm,),
        in_specs=[rows(x), rows(ya), rows(yb), rows(ga), rows(gb), full(wa), full(wb), full(wo), full(gffn)],
        out_specs=(pl.BlockSpec((tm, d), lambda i: (i, 0)), pl.BlockSpec((tm, d), lambda i: (i, 0))),
        out_shape=(jax.ShapeDtypeStruct((s, d), F32), jax.ShapeDtypeStruct((s, d), BF16)),
        compiler_params=pltpu.CompilerParams(dimension_semantics=("arbitrary",), vmem_limit_bytes=VMEM_LIMIT),
        name="merge",
    )(x, ya, yb, ga, gb, wa, wb, wo, gffn)


def _ffn_kernel(x_ref, h_ref, w1_ref, w2_ref, o_ref, *, tf):
    h = h_ref[...]
    acc = x_ref[...]
    for c in range(w1_ref.shape[1] // tf):
        u = jnp.maximum(_dot(h, w1_ref[:, c * tf:(c + 1) * tf]), 0.0)
        acc = acc + _dot((u * u).astype(BF16), w2_ref[c * tf:(c + 1) * tf, :])
    o_ref[...] = acc


def _ffn_call(x, h2, w1, w2, tm=512, tf=1024):
    s, d = x.shape
    full = lambda a: pl.BlockSpec(a.shape, lambda i: (0,) * a.ndim, pipeline_mode=pl.Buffered(1))
    rows = pl.BlockSpec((tm, d), lambda i: (i, 0))
    return pl.pallas_call(
        functools.partial(_ffn_kernel, tf=tf),
        grid=(s // tm,),
        in_specs=[rows, rows, full(w1), full(w2)],
        out_specs=rows,
        out_shape=jax.ShapeDtypeStruct((s, d), F32),
        compiler_params=pltpu.CompilerParams(dimension_semantics=("arbitrary",), vmem_limit_bytes=VMEM_LIMIT),
        name="ffn",
    )(x, h2, w1, w2)


def _bucket_lookup(table, dist, keep):
    idx = jnp.asarray(np.where(keep, _rel_bucket_np(dist), -1).astype(np.int32))[None]
    out = jnp.full((table.shape[1],) + dist.shape, NEG_INF, F32)
    for b in range(N_REL_BUCKETS):
        out = jnp.where(idx == b, table[b].reshape((-1,) + (1,) * dist.ndim), out)
    return out


def _bias_tables(rel_bias):
    blk, w = MOBA_BLOCK, SWA_WINDOW
    ck = np.arange(blk)[:, None]
    rq = np.arange(blk)[None, :]
    dist = np.stack([d * blk + rq - ck for d in range(MOBA_NEAR)])
    moba = _bucket_lookup(rel_bias[:, :MOBA_HEADS], dist, dist >= 0)
    moba = (moba.reshape(HEAD_PAIRS, 2, MOBA_NEAR, blk, blk).transpose(2, 0, 3, 1, 4)
            .reshape(MOBA_NEAR, HEAD_PAIRS, blk, 2 * blk))
    cfar = rel_bias[N_REL_BUCKETS - 1, :MOBA_HEADS]

    ck = np.arange(2 * w)[:, None]
    rq = np.arange(w)[None, :]
    dist = rq - (ck - w)
    swa = _bucket_lookup(rel_bias[:, MOBA_HEADS:], dist, (dist >= 0) & (dist < w))
    swa = swa.reshape(SWA_Q_HEADS // 2, 2, 2 * w, w).transpose(0, 2, 1, 3).reshape(SWA_Q_HEADS // 2, 2 * w, 2 * w)
    return (moba * LOG2E).astype(F32), (cfar * LOG2E).astype(F32), (swa * LOG2E).astype(F32)


def _layer_weights(w_in_l):
    c = np.cumsum([0, 512, 512, 512, 512, 128, 128, 1024, 1024]).tolist()
    kb = w_in_l[:, c[4]:c[5]]
    kb_dup = jnp.concatenate([kb[:, :HEAD_DIM], kb[:, :HEAD_DIM], kb[:, HEAD_DIM:], kb[:, HEAD_DIM:]], axis=1)
    return jnp.concatenate([w_in_l[:, :c[4]], kb_dup, w_in_l[:, c[5]:]], axis=1).astype(BF16)


def kernel(x, rel_bias, g_mix, w_in, b_gate, q_norm_a, k_norm_a, q_norm_b, k_norm_b, sinks,
           w_branch_a, w_branch_b, w_out, g_ffn, w_ff1, w_ff2):
    b, s, d = x.shape
    assert b == 1 and s % MOBA_BLOCK == 0
    depth = w_in.shape[0]
    bias_moba, cfar, bias_swa = _bias_tables(rel_bias)
    tile = lambda g, n: jnp.tile(g, n)[None, :]
    xs = x[0]
    for l in range(depth):
        qa, ka, kmean, vat, qb, kbd, vbt, ga, gb = _proj_call(
            xs, g_mix[l][None, :], _layer_weights(w_in[l]), b_gate[l][None, :],
            tile(q_norm_a[l], _W_QA // HEAD_DIM), tile(k_norm_a[l], _W_KA // HEAD_DIM),
            tile(q_norm_b[l], _W_QB // HEAD_DIM), tile(k_norm_b[l], _W_KB // HEAD_DIM))
        ya = _moba_call(cfar, qa, ka, vat, kmean[:, 0, :], bias_moba)
        yb = _swa_call(sinks[l] * LOG2E, qb, kbd, vbt, bias_swa)
        xs, h2 = _merge_call(xs, ya, yb, ga, gb, w_branch_a[l].astype(BF16), w_branch_b[l].astype(BF16),
                             w_out[l].astype(BF16), g_ffn[l][None, :])
        xs = _ffn_call(xs, h2, w_ff1[l].astype(BF16), w_ff2[l].astype(BF16))
    return xs[None]
```

```python
import functools

import jax
import jax.numpy as jnp
import numpy as np
from jax import lax
from jax.experimental import pallas as pl
from jax.experimental.pallas import tpu as pltpu

HEAD_DIM = 64
MOBA_HEADS = 8
MOBA_BLOCK = 256
MOBA_TOPK = 3
SWA_Q_HEADS = 8
SWA_KV_HEADS = 2
SWA_WINDOW = 128
SWA_STEP_BLOCKS = 2
N_REL_BUCKETS = 32
REL_MAX_DISTANCE = 2048
NORM_EPS = 1e-6
LOG2E = 1.4426950408889634
ATTN_SCALE = HEAD_DIM ** -0.5 * LOG2E

LANES = 128
HEAD_PAIRS = MOBA_HEADS // 2
VT_ROWS = 80
MOBA_NEAR = 7
MOBA_FAR_UNROLL = 4
assert MOBA_FAR_UNROLL == MOBA_NEAR // 2 + 1
VMEM_LIMIT = 56 * 1024 * 1024

BF16 = jnp.bfloat16
F32 = jnp.float32
NEG_INF = float("-inf")
M_INIT = -1e30


def _rel_bucket_np(dist):
    n = np.maximum(dist, 0)
    exact = N_REL_BUCKETS // 2
    nf = np.maximum(n, 1).astype(np.float32)
    large = exact + (np.log(nf / np.float32(exact)) / np.float32(np.log(REL_MAX_DISTANCE / exact))
                     * np.float32(N_REL_BUCKETS - exact)).astype(np.int32)
    large = np.minimum(large, N_REL_BUCKETS - 1)
    return np.where(n < exact, n, large)


def _nt_dot(a, b):
    return lax.dot_general(a, b, (((1,), (1,)), ((), ())), preferred_element_type=F32)


def _dot(a, b):
    return jnp.dot(a, b, preferred_element_type=F32)


def _rms_norm_rows(xf, g):
    ms = jnp.mean(xf * xf, axis=-1, keepdims=True)
    return xf * lax.rsqrt(ms + NORM_EPS) * g


def _head_norm(acc, g2):
    w = acc.shape[1]
    lane = lax.broadcasted_iota(jnp.int32, (1, LANES), 1)
    lo = lane < HEAD_DIM
    outs = []
    for b in range(w // LANES):
        xb = acc[:, b * LANES:(b + 1) * LANES]
        x2 = xb * xb
        s_lo = jnp.sum(jnp.where(lo, x2, 0.0), axis=-1, keepdims=True)
        s_hi = jnp.sum(jnp.where(lo, 0.0, x2), axis=-1, keepdims=True)
        r = jnp.where(lo, lax.rsqrt(s_lo / HEAD_DIM + NORM_EPS), lax.rsqrt(s_hi / HEAD_DIM + NORM_EPS))
        outs.append(xb * r * g2[:, b * LANES:(b + 1) * LANES])
    return outs[0] if len(outs) == 1 else jnp.concatenate(outs, axis=1)


def _v_transposed(v, n_heads):
    rows = v.shape[0]
    vt = v.T
    r = lax.broadcasted_iota(jnp.int32, (VT_ROWS - HEAD_DIM, rows), 0)
    aug = jnp.where(r == 0, 1.0, 0.0).astype(F32)
    parts = []
    for h in range(n_heads):
        parts.append(vt[h * HEAD_DIM:(h + 1) * HEAD_DIM])
        parts.append(aug)
    return jnp.concatenate(parts, axis=0).astype(BF16)


_W_QA, _W_KA, _W_VA, _W_QB, _W_KB, _W_VB, _W_GA, _W_GB = 512, 512, 512, 512, 256, 128, 1024, 1024
_PROJ_COLS = np.cumsum([0, _W_QA, _W_KA, _W_VA, _W_QB, _W_KB, _W_VB, _W_GA, _W_GB]).tolist()


def _proj_kernel(x_ref, gmix_ref, w_ref, bgate_ref, gqa_ref, gka_ref, gqb_ref, gkb_ref,
                 qa_ref, ka_ref, kmean_ref, vat_ref, qb_ref, kb_ref, vbt_ref, ga_ref, gb_ref):
    h = _rms_norm_rows(x_ref[...], gmix_ref[...]).astype(BF16)
    c = _PROJ_COLS

    def seg(k):
        return _dot(h, w_ref[:, c[k]:c[k + 1]])

    qa_ref[...] = (_head_norm(seg(0), gqa_ref[...]) * ATTN_SCALE).astype(BF16)
    ka = _head_norm(seg(1), gka_ref[...])
    ka_ref[...] = ka.astype(BF16)
    kmean_ref[0] = jnp.mean(ka, axis=0, keepdims=True)
    vat_ref[...] = _v_transposed(seg(2), MOBA_HEADS)
    qb_ref[...] = (_head_norm(seg(3), gqb_ref[...]) * ATTN_SCALE).astype(BF16)
    kb_ref[...] = _head_norm(seg(4), gkb_ref[...]).astype(BF16)
    vbt_ref[...] = _v_transposed(seg(5), SWA_KV_HEADS)
    bg = bgate_ref[...]
    d = _W_GA
    ga_ref[...] = 1.0 / (1.0 + jnp.exp(-(seg(6) + bg[:, :d])))
    gb_ref[...] = 1.0 / (1.0 + jnp.exp(-(seg(7) + bg[:, d:])))


def _proj_call(x, gmix, w, bgate, gqa, gka, gqb, gkb):
    s, d = x.shape
    tm = MOBA_BLOCK
    nblk = s // tm
    full = lambda shape: pl.BlockSpec(shape, lambda i: (0,) * len(shape))
    rows = lambda width: pl.BlockSpec((tm, width), lambda i: (i, 0))
    cols = lambda height: pl.BlockSpec((height, tm), lambda i: (0, i))
    out_shape = (
        jax.ShapeDtypeStruct((s, _W_QA), BF16),
        jax.ShapeDtypeStruct((s, _W_KA), BF16),
        jax.ShapeDtypeStruct((nblk, 1, _W_KA), F32),
        jax.ShapeDtypeStruct((MOBA_HEADS * VT_ROWS, s), BF16),
        jax.ShapeDtypeStruct((s, _W_QB), BF16),
        jax.ShapeDtypeStruct((s, _W_KB), BF16),
        jax.ShapeDtypeStruct((SWA_KV_HEADS * VT_ROWS, s), BF16),
        jax.ShapeDtypeStruct((s, _W_GA), F32),
        jax.ShapeDtypeStruct((s, _W_GB), F32),
    )
    out_specs = (
        rows(_W_QA), rows(_W_KA), pl.BlockSpec((1, 1, _W_KA), lambda i: (i, 0, 0)),
        cols(MOBA_HEADS * VT_ROWS), rows(_W_QB), rows(_W_KB), cols(SWA_KV_HEADS * VT_ROWS),
        rows(_W_GA), rows(_W_GB),
    )
    return pl.pallas_call(
        _proj_kernel,
        grid=(nblk,),
        in_specs=[rows(d), full(gmix.shape), full(w.shape), full(bgate.shape),
                  full(gqa.shape), full(gka.shape), full(gqb.shape), full(gkb.shape)],
        out_specs=out_specs,
        out_shape=out_shape,
        compiler_params=pltpu.CompilerParams(dimension_semantics=("arbitrary",), vmem_limit_bytes=VMEM_LIMIT),
        name="proj",
    )(x, gmix, w, bgate, gqa, gka, gqb, gkb)


def _moba_kernel(cfar_ref, q_ref, k_ref, vt_ref, kmean_ref, bias_ref, o_ref,
                 qq_scr, sel_scr, m_scr, acc_scr, sta_ref, stb_ref, mxa_ref, mxb_ref, *, nblk):
    sta_scr = (sta_ref, mxa_ref)
    stb_scr = (stb_ref, mxb_ref)
    p = pl.program_id(0)
    i = pl.program_id(1)
    blk = MOBA_BLOCK
    q = q_ref[...]
    lane = lax.broadcasted_iota(jnp.int32, (1, LANES), 1)
    km = kmean_ref[...]
    km_hi = km.astype(BF16)
    km_lo = (km - km_hi.astype(F32)).astype(BF16)
    n_iota = lax.broadcasted_iota(jnp.int32, (nblk, blk), 0)

    qms = []
    for hh in range(2):
        hmask = (lane < HEAD_DIM) if hh == 0 else (lane >= HEAD_DIM)
        qms.append(jnp.where(hmask, q, jnp.zeros_like(q)))
        qq_scr[hh * blk:(hh + 1) * blk, :] = qms[hh]

    gates = [_nt_dot(km_hi, qms[hh]) + _nt_dot(km_lo, qms[hh]) for hh in range(2)]

    def select_blocks():
        for hh in range(2):
            cols = slice(hh * blk, (hh + 1) * blk)
            g = jnp.where(n_iota < i, gates[hh], NEG_INF)
            sel = jnp.zeros((nblk, blk), F32)
            for _ in range(MOBA_TOPK):
                mx = jnp.max(g, axis=0, keepdims=True)
                idx = jnp.min(jnp.where(g == mx, n_iota, nblk), axis=0, keepdims=True)
                hit = n_iota == idx
                valid = jnp.where(mx > NEG_INF, 1.0, 0.0)
                sel = jnp.maximum(sel, jnp.where(hit, valid, 0.0))
                g = jnp.where(hit, NEG_INF, g)
            sel_scr[:, cols] = jnp.where(n_iota == i, 1.0, sel)
            acc_scr[hh] = jnp.zeros((VT_ROWS, blk), F32)
        m_scr[...] = jnp.full((1, 2 * blk), M_INIT, F32)

    col = lax.broadcasted_iota(jnp.int32, (1, 2 * blk), 1)
    cfar = jnp.where(col < blk, cfar_ref[2 * p], cfar_ref[2 * p + 1])
    n_far = jnp.maximum(i - (MOBA_NEAR - 1), 0)

    def scores(first, nb, buf):
        st_ref, mx_ref = buf
        off = pl.multiple_of(first * blk, blk)
        st = _nt_dot(k_ref[pl.ds(off, nb * blk), :], qq_scr[...])
        st_ref[:nb * blk, :] = st
        for u in range(nb):
            mx_ref[u:u + 1, :] = jnp.max(st[u * blk:(u + 1) * blk], axis=0, keepdims=True)

    def attend(first, nb, buf, near):
        st_ref, mx_ref = buf
        ms, pvs = [], []
        for u in range(nb):
            j = first + u
            s_u = st_ref[u * blk:(u + 1) * blk, :]
            if near:
                s_u = s_u + bias_ref[jnp.clip(i - j, 0, MOBA_NEAR - 1)]
                on = sel_scr[pl.ds(j, 1), :] > 0.5
                m_u = jnp.max(s_u, axis=0, keepdims=True)
                ms.append(jnp.where(on, m_u, NEG_INF))
            else:
                on = sel_scr[pl.ds(j, 1), :] > jnp.where(j < n_far, 0.5, 2.0)
                m_u = mx_ref[u:u + 1, :]
                ms.append(jnp.where(on, m_u + cfar, NEG_INF))
            pt = jnp.exp2(s_u - m_u).astype(BF16)
            ko = pl.multiple_of(j * blk, blk)
            pvs.append([_dot(vt_ref[hh * VT_ROWS:(hh + 1) * VT_ROWS, pl.ds(ko, blk)], pt[:, hh * blk:(hh + 1) * blk])
                        for hh in range(2)])
        m_old = m_scr[...]
        m_new = m_old
        for m_u in ms:
            m_new = jnp.maximum(m_new, m_u)
        alpha = jnp.exp2(m_old - m_new)
        ws = [jnp.exp2(m_u - m_new) for m_u in ms]
        for hh in range(2):
            cols = slice(hh * blk, (hh + 1) * blk)
            acc = acc_scr[hh] * alpha[:, cols]
            for w_u, pv in zip(ws, pvs):
                acc = acc + pv[hh] * w_u[:, cols]
            acc_scr[hh] = acc
        m_scr[...] = m_new

    fu = MOBA_FAR_UNROLL
    n_groups = (n_far + fu - 1) // fu
    near0 = jnp.maximum(i - (MOBA_NEAR - 1), 0)
    nb1 = MOBA_NEAR - fu

    def item_start(g):
        return jnp.where(g < n_groups, g * fu, near0)

    def near_tail(st_x, st_y):
        scores(near0 + fu, nb1, st_y)
        attend(near0, fu, st_x, True)
        attend(near0 + fu, nb1, st_y, True)

    scores(item_start(0), fu, sta_scr)
    select_blocks()

    def pair_body(t, carry):
        scores((2 * t + 1) * fu, fu, stb_scr)
        attend(2 * t * fu, fu, sta_scr, False)
        scores(item_start(2 * t + 2), fu, sta_scr)
        attend((2 * t + 1) * fu, fu, stb_scr, False)
        return carry

    lax.fori_loop(0, n_groups // 2, pair_body, 0)

    @pl.when(n_groups % 2 == 0)
    def _():
        near_tail(sta_scr, stb_scr)

    @pl.when(n_groups % 2 == 1)
    def _():
        scores(near0, fu, stb_scr)
        attend((n_groups - 1) * fu, fu, sta_scr, False)
        near_tail(stb_scr, sta_scr)

    outs = []
    for hh in range(2):
        a = acc_scr[hh]
        outs.append(a[:HEAD_DIM] / a[HEAD_DIM:HEAD_DIM + 1])
    o_ref[...] = jnp.concatenate(outs, axis=0).T.astype(BF16)


def _moba_call(cfar, qa, ka, vat, kmean, bias_t):
    s = qa.shape[0]
    blk = MOBA_BLOCK
    nblk = s // blk
    grid_spec = pltpu.PrefetchScalarGridSpec(
        num_scalar_prefetch=1,
        grid=(HEAD_PAIRS, nblk),
        in_specs=[
            pl.BlockSpec((blk, LANES), lambda p, i, c: (i, p)),
            pl.BlockSpec((s, LANES), lambda p, i, c: (0, p)),
            pl.BlockSpec((2 * VT_ROWS, s), lambda p, i, c: (p, 0)),
            pl.BlockSpec((nblk, LANES), lambda p, i, c: (0, p)),
            pl.BlockSpec((MOBA_NEAR, None, blk, 2 * blk), lambda p, i, c: (0, p, 0, 0)),
        ],
        out_specs=pl.BlockSpec((blk, LANES), lambda p, i, c: (i, p)),
        scratch_shapes=[
            pltpu.VMEM((2 * blk, LANES), BF16),
            pltpu.VMEM((nblk, 2 * blk), F32),
            pltpu.VMEM((1, 2 * blk), F32),
            pltpu.VMEM((2, VT_ROWS, blk), F32),
            pltpu.VMEM((MOBA_FAR_UNROLL * blk, 2 * blk), F32),
            pltpu.VMEM((MOBA_FAR_UNROLL * blk, 2 * blk), F32),
            pltpu.VMEM((8, 2 * blk), F32),
            pltpu.VMEM((8, 2 * blk), F32),
        ],
    )
    return pl.pallas_call(
        functools.partial(_moba_kernel, nblk=nblk),
        grid_spec=grid_spec,
        out_shape=jax.ShapeDtypeStruct((s, MOBA_HEADS * HEAD_DIM), BF16),
        compiler_params=pltpu.CompilerParams(dimension_semantics=("arbitrary", "arbitrary"),
                                             vmem_limit_bytes=VMEM_LIMIT),
        name="moba",
    )(cfar, qa, ka, vat, kmean, bias_t)


def _swa_kernel(sink_ref, q_ref, kprev_ref, kcur_ref, vprev_ref, vcur_ref, bias_ref, o_ref, st_scr):
    b = pl.program_id(0)
    w = SWA_WINDOW
    kall = jnp.concatenate([kprev_ref[...], kcur_ref[...]], axis=0)
    vall = jnp.concatenate([vprev_ref[...], vcur_ref[...]], axis=1)
    row = lax.broadcasted_iota(jnp.int32, (2 * w, 1), 0)
    keep = row >= jnp.where(b > 0, 0, w)
    lane = lax.broadcasted_iota(jnp.int32, (1, LANES), 1)
    col = lax.broadcasted_iota(jnp.int32, (1, 2 * w), 1)
    group = SWA_Q_HEADS // SWA_KV_HEADS
    chains = [(sb, pr) for sb in range(SWA_STEP_BLOCKS) for pr in range(SWA_Q_HEADS // 2)]
    for c, (sb, pr) in enumerate(chains):
        g = (2 * pr) // group
        qp = q_ref[sb * w:(sb + 1) * w, pr * LANES:(pr + 1) * LANES]
        zero = jnp.zeros_like(qp)
        qq = jnp.concatenate([jnp.where(lane < HEAD_DIM, qp, zero),
                              jnp.where(lane >= HEAD_DIM, qp, zero)], axis=0)
        kd = kall[sb * w:(sb + 2) * w, g * LANES:(g + 1) * LANES]
        st_scr[c] = _nt_dot(kd, qq)
    for c, (sb, pr) in enumerate(chains):
        g = (2 * pr) // group
        st = st_scr[c] + bias_ref[pr]
        if sb == 0:
            st = jnp.where(keep, st, NEG_INF)
        sink = jnp.where(col < w, sink_ref[2 * pr], sink_ref[2 * pr + 1])
        m = jnp.maximum(jnp.max(st, axis=0, keepdims=True), sink)
        pt = jnp.exp2(st - m).astype(BF16)
        acc = _dot(vall[g * VT_ROWS:(g + 1) * VT_ROWS, sb * w:(sb + 2) * w], pt)
        denom = acc[HEAD_DIM:HEAD_DIM + 1] + jnp.exp2(sink - m)
        o = acc[:HEAD_DIM] / denom
        oo = jnp.concatenate([o[:, :w], o[:, w:]], axis=0)
        o_ref[sb * w:(sb + 1) * w, pr * LANES:(pr + 1) * LANES] = oo.T.astype(BF16)


def _swa_call(sinks, qb, kbd, vbt, bias_t):
    s = qb.shape[0]
    w = SWA_WINDOW
    n = SWA_STEP_BLOCKS
    assert s % (n * w) == 0
    prev_blk = lambda i: jnp.maximum(n * i - 1, 0)
    grid_spec = pltpu.PrefetchScalarGridSpec(
        num_scalar_prefetch=1,
        grid=(s // (n * w),),
        in_specs=[
            pl.BlockSpec((n * w, _W_QB), lambda i, c: (i, 0)),
            pl.BlockSpec((w, _W_KB), lambda i, c: (prev_blk(i), 0)),
            pl.BlockSpec((n * w, _W_KB), lambda i, c: (i, 0)),
            pl.BlockSpec((SWA_KV_HEADS * VT_ROWS, w), lambda i, c: (0, prev_blk(i))),
            pl.BlockSpec((SWA_KV_HEADS * VT_ROWS, n * w), lambda i, c: (0, i)),
            pl.BlockSpec(bias_t.shape, lambda i, c: (0, 0, 0)),
        ],
        out_specs=pl.BlockSpec((n * w, _W_QB), lambda i, c: (i, 0)),
        scratch_shapes=[pltpu.VMEM((n * SWA_Q_HEADS // 2, 2 * w, 2 * w), F32)],
    )
    return pl.pallas_call(
        _swa_kernel,
        grid_spec=grid_spec,
        out_shape=jax.ShapeDtypeStruct((s, _W_QB), BF16),
        compiler_params=pltpu.CompilerParams(dimension_semantics=("arbitrary",), vmem_limit_bytes=VMEM_LIMIT),
        name="swa",
    )(sinks, qb, kbd, kbd, vbt, vbt, bias_t)


def _merge_kernel(x_ref, ya_ref, yb_ref, ga_ref, gb_ref, wa_ref, wb_ref, wo_ref, gffn_ref, xo_ref, h2_ref):
    merged = ga_ref[...] * _dot(ya_ref[...], wa_ref[...]) + gb_ref[...] * _dot(yb_ref[...], wb_ref[...])
    xn = x_ref[...] + _dot(merged.astype(BF16), wo_ref[...])
    xo_ref[...] = xn
    h2_ref[...] = _rms_norm_rows(xn, gffn_ref[...]).astype(BF16)


def _merge_call(x, ya, yb, ga, gb, wa, wb, wo, gffn, tm=512):
    s, d = x.shape
    full = lambda a: pl.BlockSpec(a.shape, lambda i: (0,) * a.ndim)
    rows = lambda a: pl.BlockSpec((tm, a.shape[1]), lambda i: (i, 0))
    return pl.pallas_call(
        _merge_kernel,
        grid=(s // tm,),
        in_specs=[rows(x), rows(ya), rows(yb), rows(ga), rows(gb), full(wa), full(wb), full(wo), full(gffn)],
        out_specs=(pl.BlockSpec((tm, d), lambda i: (i, 0)), pl.BlockSpec((tm, d), lambda i: (i, 0))),
        out_shape=(jax.ShapeDtypeStruct((s, d), F32), jax.ShapeDtypeStruct((s, d), BF16)),
        compiler_params=pltpu.CompilerParams(dimension_semantics=("arbitrary",), vmem_limit_bytes=VMEM_LIMIT),
        name="merge",
    )(x, ya, yb, ga, gb, wa, wb, wo, gffn)


def _ffn_kernel(x_ref, h_ref, w1_ref, w2_ref, o_ref, *, tf):
    h = h_ref[...]
    acc = x_ref[...]
    for c in range(w1_ref.shape[1] // tf):
        u = jnp.maximum(_dot(h, w1_ref[:, c * tf:(c + 1) * tf]), 0.0)
        acc = acc + _dot((u * u).astype(BF16), w2_ref[c * tf:(c + 1) * tf, :])
    o_ref[...] = acc


def _ffn_call(x, h2, w1, w2, tm=512, tf=1024):
    s, d = x.shape
    full = lambda a: pl.BlockSpec(a.shape, lambda i: (0,) * a.ndim, pipeline_mode=pl.Buffered(1))
    rows = pl.BlockSpec((tm, d), lambda i: (i, 0))
    return pl.pallas_call(
        functools.partial(_ffn_kernel, tf=tf),
        grid=(s // tm,),
        in_specs=[rows, rows, full(w1), full(w2)],
        out_specs=rows,
        out_shape=jax.ShapeDtypeStruct((s, d), F32),
        compiler_params=pltpu.CompilerParams(dimension_semantics=("arbitrary",), vmem_limit_bytes=VMEM_LIMIT),
        name="ffn",
    )(x, h2, w1, w2)


def _bucket_lookup(table, dist, keep):
    idx = jnp.asarray(np.where(keep, _rel_bucket_np(dist), -1).astype(np.int32))[None]
    out = jnp.full((table.shape[1],) + dist.shape, NEG_INF, F32)
    for b in range(N_REL_BUCKETS):
        out = jnp.where(idx == b, table[b].reshape((-1,) + (1,) * dist.ndim), out)
    return out


def _bias_tables(rel_bias):
    blk, w = MOBA_BLOCK, SWA_WINDOW
    ck = np.arange(blk)[:, None]
    rq = np.arange(blk)[None, :]
    dist = np.stack([d * blk + rq - ck for d in range(MOBA_NEAR)])
    moba = _bucket_lookup(rel_bias[:, :MOBA_HEADS], dist, dist >= 0)
    moba = (moba.reshape(HEAD_PAIRS, 2, MOBA_NEAR, blk, blk).transpose(2, 0, 3, 1, 4)
            .reshape(MOBA_NEAR, HEAD_PAIRS, blk, 2 * blk))
    cfar = rel_bias[N_REL_BUCKETS - 1, :MOBA_HEADS]

    ck = np.arange(2 * w)[:, None]
    rq = np.arange(w)[None, :]
    dist = rq - (ck - w)
    swa = _bucket_lookup(rel_bias[:, MOBA_HEADS:], dist, (dist >= 0) & (dist < w))
    swa = swa.reshape(SWA_Q_HEADS // 2, 2, 2 * w, w).transpose(0, 2, 1, 3).reshape(SWA_Q_HEADS // 2, 2 * w, 2 * w)
    return (moba * LOG2E).astype(F32), (cfar * LOG2E).astype(F32), (swa * LOG2E).astype(F32)


def _layer_weights(w_in_l):
    c = np.cumsum([0, 512, 512, 512, 512, 128, 128, 1024, 1024]).tolist()
    kb = w_in_l[:, c[4]:c[5]]
    kb_dup = jnp.concatenate([kb[:, :HEAD_DIM], kb[:, :HEAD_DIM], kb[:, HEAD_DIM:], kb[:, HEAD_DIM:]], axis=1)
    return jnp.concatenate([w_in_l[:, :c[4]], kb_dup, w_in_l[:, c[5]:]], axis=1).astype(BF16)


def kernel(x, rel_bias, g_mix, w_in, b_gate, q_norm_a, k_norm_a, q_norm_b, k_norm_b, sinks,
           w_branch_a, w_branch_b, w_out, g_ffn, w_ff1, w_ff2):
    b, s, d = x.shape
    assert b == 1 and s % MOBA_BLOCK == 0
    depth = w_in.shape[0]
    bias_moba, cfar, bias_swa = _bias_tables(rel_bias)
    tile = lambda g, n: jnp.tile(g, n)[None, :]
    xs = x[0]
    for l in range(depth):
        qa, ka, kmean, vat, qb, kbd, vbt, ga, gb = _proj_call(
            xs, g_mix[l][None, :], _layer_weights(w_in[l]), b_gate[l][None, :],
            tile(q_norm_a[l], _W_QA // HEAD_DIM), tile(k_norm_a[l], _W_KA // HEAD_DIM),
            tile(q_norm_b[l], _W_QB // HEAD_DIM), tile(k_norm_b[l], _W_KB // HEAD_DIM))
        ya = _moba_call(cfar, qa, ka, vat, kmean[:, 0, :], bias_moba)
        yb = _swa_call(sinks[l] * LOG2E, qb, kbd, vbt, bias_swa)
        xs, h2 = _merge_call(xs, ya, yb, ga, gb, w_branch_a[l].astype(BF16), w_branch_b[l].astype(BF16),
                             w_out[l].astype(BF16), g_ffn[l][None, :])
        xs = _ffn_call(xs, h2, w_ff1[l].astype(BF16), w_ff2[l].astype(BF16))
    return xs[None]
```

```python
import functools

import jax
import jax.numpy as jnp
import numpy as np
from jax import lax
from jax.experimental import pallas as pl
from jax.experimental.pallas import tpu as pltpu

HEAD_DIM = 64
MOBA_HEADS = 8
MOBA_BLOCK = 256
MOBA_TOPK = 3
SWA_Q_HEADS = 8
SWA_KV_HEADS = 2
SWA_WINDOW = 128
SWA_STEP_BLOCKS = 2
N_REL_BUCKETS = 32
REL_MAX_DISTANCE = 2048
NORM_EPS = 1e-6
LOG2E = 1.4426950408889634
ATTN_SCALE = HEAD_DIM ** -0.5 * LOG2E

LANES = 128
HEAD_PAIRS = MOBA_HEADS // 2
VT_ROWS = 80
MOBA_NEAR = 7
MOBA_FAR_UNROLL = 4
assert MOBA_FAR_UNROLL == MOBA_NEAR // 2 + 1
VMEM_LIMIT = 56 * 1024 * 1024

BF16 = jnp.bfloat16
F32 = jnp.float32
NEG_INF = float("-inf")
M_INIT = -1e30


def _rel_bucket_np(dist):
    n = np.maximum(dist, 0)
    exact = N_REL_BUCKETS // 2
    nf = np.maximum(n, 1).astype(np.float32)
    large = exact + (np.log(nf / np.float32(exact)) / np.float32(np.log(REL_MAX_DISTANCE / exact))
                     * np.float32(N_REL_BUCKETS - exact)).astype(np.int32)
    large = np.minimum(large, N_REL_BUCKETS - 1)
    return np.where(n < exact, n, large)


def _nt_dot(a, b):
    return lax.dot_general(a, b, (((1,), (1,)), ((), ())), preferred_element_type=F32)


def _dot(a, b):
    return jnp.dot(a, b, preferred_element_type=F32)


def _rms_norm_rows(xf, g):
    ms = jnp.mean(xf * xf, axis=-1, keepdims=True)
    return xf * lax.rsqrt(ms + NORM_EPS) * g


def _head_norm(acc, g2):
    w = acc.shape[1]
    lane = lax.broadcasted_iota(jnp.int32, (1, LANES), 1)
    lo = lane < HEAD_DIM
    outs = []
    for b in range(w // LANES):
        xb = acc[:, b * LANES:(b + 1) * LANES]
        x2 = xb * xb
        s_lo = jnp.sum(jnp.where(lo, x2, 0.0), axis=-1, keepdims=True)
        s_hi = jnp.sum(jnp.where(lo, 0.0, x2), axis=-1, keepdims=True)
        r = jnp.where(lo, lax.rsqrt(s_lo / HEAD_DIM + NORM_EPS), lax.rsqrt(s_hi / HEAD_DIM + NORM_EPS))
        outs.append(xb * r * g2[:, b * LANES:(b + 1) * LANES])
    return outs[0] if len(outs) == 1 else jnp.concatenate(outs, axis=1)


def _v_transposed(v, n_heads):
    rows = v.shape[0]
    vt = v.T
    r = lax.broadcasted_iota(jnp.int32, (VT_ROWS - HEAD_DIM, rows), 0)
    aug = jnp.where(r == 0, 1.0, 0.0).astype(F32)
    parts = []
    for h in range(n_heads):
        parts.append(vt[h * HEAD_DIM:(h + 1) * HEAD_DIM])
        parts.append(aug)
    return jnp.concatenate(parts, axis=0).astype(BF16)


_W_QA, _W_KA, _W_VA, _W_QB, _W_KB, _W_VB, _W_GA, _W_GB = 512, 512, 512, 512, 256, 128, 1024, 1024
_PROJ_COLS = np.cumsum([0, _W_QA, _W_KA, _W_VA, _W_QB, SWA_KV_HEADS * HEAD_DIM, _W_VB, _W_GA, _W_GB]).tolist()


def _proj_kernel(x_ref, gmix_ref, w_ref, bgate_ref, gqa_ref, gka_ref, gqb_ref, gkb_ref,
                 qa_ref, ka_ref, kmean_ref, vat_ref, qb_ref, kb_ref, vbt_ref, ga_ref, gb_ref):
    h = _rms_norm_rows(x_ref[...], gmix_ref[...]).astype(BF16)
    c = _PROJ_COLS

    def seg(k):
        return _dot(h, w_ref[:, c[k]:c[k + 1]])

    qa_ref[...] = (_head_norm(seg(0), gqa_ref[...]) * ATTN_SCALE).astype(BF16)
    ka = _head_norm(seg(1), gka_ref[...])
    ka_ref[...] = ka.astype(BF16)
    kmean_ref[0] = jnp.mean(ka, axis=0, keepdims=True)
    vat_ref[...] = _v_transposed(seg(2), MOBA_HEADS)
    qb_ref[...] = (_head_norm(seg(3), gqb_ref[...]) * ATTN_SCALE).astype(BF16)
    kb = _head_norm(seg(4), gkb_ref[...])
    kb_swapped = pltpu.roll(kb, HEAD_DIM, axis=1)
    lo = lax.broadcasted_iota(jnp.int32, (1, LANES), 1) < HEAD_DIM
    kb_ref[...] = jnp.concatenate([jnp.where(lo, kb, kb_swapped), jnp.where(lo, kb_swapped, kb)], axis=1).astype(BF16)
    vbt_ref[...] = _v_transposed(seg(5), SWA_KV_HEADS)
    bg = bgate_ref[...]
    d = _W_GA
    ga_ref[...] = 1.0 / (1.0 + jnp.exp(-(seg(6) + bg[:, :d])))
    gb_ref[...] = 1.0 / (1.0 + jnp.exp(-(seg(7) + bg[:, d:])))


def _layer_spec(a, l, **kw):
    return pl.BlockSpec((None,) + a.shape[1:], lambda i: (l,) + (0,) * (a.ndim - 1), **kw)


def _proj_call(l, x, gmix, w, bgate, gqa, gka, gqb, gkb):
    s, d = x.shape
    tm = MOBA_BLOCK
    nblk = s // tm
    full = lambda a: _layer_spec(a, l)
    rows = lambda width: pl.BlockSpec((tm, width), lambda i: (i, 0))
    cols = lambda height: pl.BlockSpec((height, tm), lambda i: (0, i))
    out_shape = (
        jax.ShapeDtypeStruct((s, _W_QA), BF16),
        jax.ShapeDtypeStruct((s, _W_KA), BF16),
        jax.ShapeDtypeStruct((nblk, 1, _W_KA), F32),
        jax.ShapeDtypeStruct((MOBA_HEADS * VT_ROWS, s), BF16),
        jax.ShapeDtypeStruct((s, _W_QB), BF16),
        jax.ShapeDtypeStruct((s, _W_KB), BF16),
        jax.ShapeDtypeStruct((SWA_KV_HEADS * VT_ROWS, s), BF16),
        jax.ShapeDtypeStruct((s, _W_GA), F32),
        jax.ShapeDtypeStruct((s, _W_GB), F32),
    )
    out_specs = (
        rows(_W_QA), rows(_W_KA), pl.BlockSpec((1, 1, _W_KA), lambda i: (i, 0, 0)),
        cols(MOBA_HEADS * VT_ROWS), rows(_W_QB), rows(_W_KB), cols(SWA_KV_HEADS * VT_ROWS),
        rows(_W_GA), rows(_W_GB),
    )
    return pl.pallas_call(
        _proj_kernel,
        grid=(nblk,),
        in_specs=[rows(d), full(gmix), full(w), full(bgate), full(gqa), full(gka), full(gqb), full(gkb)],
        out_specs=out_specs,
        out_shape=out_shape,
        compiler_params=pltpu.CompilerParams(dimension_semantics=("arbitrary",), vmem_limit_bytes=VMEM_LIMIT),
        name="proj",
    )(x, gmix, w, bgate, gqa, gka, gqb, gkb)


def _moba_kernel(cfar_ref, q_ref, k_ref, vt_ref, kmean_ref, bias_ref, o_ref,
                 qq_scr, sel_scr, m_scr, acc_scr, sta_scr, stb_scr, *, nblk):
    p = pl.program_id(0)
    i = pl.program_id(1)
    blk = MOBA_BLOCK
    q = q_ref[...]
    lane = lax.broadcasted_iota(jnp.int32, (1, LANES), 1)
    km = kmean_ref[...]
    km_hi = km.astype(BF16)
    km_lo = (km - km_hi.astype(F32)).astype(BF16)
    n_iota = lax.broadcasted_iota(jnp.int32, (nblk, blk), 0)

    qms = []
    for hh in range(2):
        hmask = (lane < HEAD_DIM) if hh == 0 else (lane >= HEAD_DIM)
        qms.append(jnp.where(hmask, q, jnp.zeros_like(q)))
        qq_scr[hh * blk:(hh + 1) * blk, :] = qms[hh]

    gates = [_nt_dot(km_hi, qms[hh]) + _nt_dot(km_lo, qms[hh]) for hh in range(2)]

    def select_blocks():
        for hh in range(2):
            cols = slice(hh * blk, (hh + 1) * blk)
            g = jnp.where(n_iota < i, gates[hh], NEG_INF)
            sel = jnp.zeros((nblk, blk), F32)
            for _ in range(MOBA_TOPK):
                mx = jnp.max(g, axis=0, keepdims=True)
                idx = jnp.min(jnp.where(g == mx, n_iota, nblk), axis=0, keepdims=True)
                hit = n_iota == idx
                valid = jnp.where(mx > NEG_INF, 1.0, 0.0)
                sel = jnp.maximum(sel, jnp.where(hit, valid, 0.0))
                g = jnp.where(hit, NEG_INF, g)
            sel_scr[:, cols] = jnp.where(n_iota == i, 1.0, sel)
            acc_scr[hh] = jnp.zeros((VT_ROWS, blk), F32)
        m_scr[...] = jnp.full((1, 2 * blk), M_INIT, F32)

    col = lax.broadcasted_iota(jnp.int32, (1, 2 * blk), 1)
    cfar = jnp.where(col < blk, cfar_ref[2 * p], cfar_ref[2 * p + 1])
    n_far = jnp.maximum(i - (MOBA_NEAR - 1), 0)

    def scores(first, nb, st_ref):
        off = pl.multiple_of(first * blk, blk)
        st_ref[:nb * blk, :] = _nt_dot(k_ref[pl.ds(off, nb * blk), :], qq_scr[...])

    def attend(first, nb, st_ref, near):
        ms, pvs = [], []
        for u in range(nb):
            j = first + u
            s_u = st_ref[u * blk:(u + 1) * blk, :]
            if near:
                s_u = s_u + bias_ref[jnp.clip(i - j, 0, MOBA_NEAR - 1)]
                on = sel_scr[pl.ds(j, 1), :] > 0.5
                m_u = jnp.max(s_u, axis=0, keepdims=True)
                ms.append(jnp.where(on, m_u, NEG_INF))
            else:
                on = sel_scr[pl.ds(j, 1), :] > jnp.where(j < n_far, 0.5, 2.0)
                m_u = jnp.max(s_u, axis=0, keepdims=True)
                ms.append(jnp.where(on, m_u + cfar, NEG_INF))
            pt = jnp.exp2(s_u - m_u).astype(BF16)
            ko = pl.multiple_of(j * blk, blk)
            pvs.append([_dot(vt_ref[hh * VT_ROWS:(hh + 1) * VT_ROWS, pl.ds(ko, blk)], pt[:, hh * blk:(hh + 1) * blk])
                        for hh in range(2)])
        m_old = m_scr[...]
        m_new = m_old
        for m_u in ms:
            m_new = jnp.maximum(m_new, m_u)
        alpha = jnp.exp2(m_old - m_new)
        ws = [jnp.exp2(m_u - m_new) for m_u in ms]
        for hh in range(2):
            cols = slice(hh * blk, (hh + 1) * blk)
            acc = acc_scr[hh] * alpha[:, cols]
            for w_u, pv in zip(ws, pvs):
                acc = acc + pv[hh] * w_u[:, cols]
            acc_scr[hh] = acc
        m_scr[...] = m_new

    fu = MOBA_FAR_UNROLL
    n_groups = (n_far + fu - 1) // fu
    near0 = jnp.maximum(i - (MOBA_NEAR - 1), 0)
    nb1 = MOBA_NEAR - fu

    def item_start(g):
        return jnp.where(g < n_groups, g * fu, near0)

    def near_tail(st_x, st_y):
        scores(near0 + fu, nb1, st_y)
        attend(near0, fu, st_x, True)
        attend(near0 + fu, nb1, st_y, True)

    scores(item_start(0), fu, sta_scr)
    select_blocks()

    def pair_body(t, carry):
        scores((2 * t + 1) * fu, fu, stb_scr)
        attend(2 * t * fu, fu, sta_scr, False)
        scores(item_start(2 * t + 2), fu, sta_scr)
        attend((2 * t + 1) * fu, fu, stb_scr, False)
        return carry

    lax.fori_loop(0, n_groups // 2, pair_body, 0)

    @pl.when(n_groups % 2 == 0)
    def _():
        near_tail(sta_scr, stb_scr)

    @pl.when(n_groups % 2 == 1)
    def _():
        scores(near0, fu, stb_scr)
        attend((n_groups - 1) * fu, fu, sta_scr, False)
        near_tail(stb_scr, sta_scr)

    outs = []
    for hh in range(2):
        a = acc_scr[hh]
        outs.append(a[:HEAD_DIM] / a[HEAD_DIM:HEAD_DIM + 1])
    o_ref[...] = jnp.concatenate(outs, axis=0).T.astype(BF16)


def _moba_call(cfar, qa, ka, vat, kmean, bias_t):
    s = qa.shape[0]
    blk = MOBA_BLOCK
    nblk = s // blk
    grid_spec = pltpu.PrefetchScalarGridSpec(
        num_scalar_prefetch=1,
        grid=(HEAD_PAIRS, nblk),
        in_specs=[
            pl.BlockSpec((blk, LANES), lambda p, i, c: (i, p)),
            pl.BlockSpec((s, LANES), lambda p, i, c: (0, p)),
            pl.BlockSpec((2 * VT_ROWS, s), lambda p, i, c: (p, 0)),
            pl.BlockSpec((nblk, LANES), lambda p, i, c: (0, p)),
            pl.BlockSpec((MOBA_NEAR, None, blk, 2 * blk), lambda p, i, c: (0, p, 0, 0)),
        ],
        out_specs=pl.BlockSpec((blk, LANES), lambda p, i, c: (i, p)),
        scratch_shapes=[
            pltpu.VMEM((2 * blk, LANES), BF16),
            pltpu.VMEM((nblk, 2 * blk), F32),
            pltpu.VMEM((1, 2 * blk), F32),
            pltpu.VMEM((2, VT_ROWS, blk), F32),
            pltpu.VMEM((MOBA_FAR_UNROLL * blk, 2 * blk), F32),
            pltpu.VMEM((MOBA_FAR_UNROLL * blk, 2 * blk), F32),
        ],
    )
    return pl.pallas_call(
        functools.partial(_moba_kernel, nblk=nblk),
        grid_spec=grid_spec,
        out_shape=jax.ShapeDtypeStruct((s, MOBA_HEADS * HEAD_DIM), BF16),
        compiler_params=pltpu.CompilerParams(dimension_semantics=("arbitrary", "arbitrary"),
                                             vmem_limit_bytes=VMEM_LIMIT),
        name="moba",
    )(cfar, qa, ka, vat, kmean, bias_t)


def _swa_kernel(sink_ref, q_ref, kprev_ref, kcur_ref, vprev_ref, vcur_ref, bias_ref, o_ref, st_scr):
    b = pl.program_id(0)
    w = SWA_WINDOW
    kall = jnp.concatenate([kprev_ref[...], kcur_ref[...]], axis=0)
    vall = jnp.concatenate([vprev_ref[...], vcur_ref[...]], axis=1)
    row = lax.broadcasted_iota(jnp.int32, (2 * w, 1), 0)
    keep = row >= jnp.where(b > 0, 0, w)
    lane = lax.broadcasted_iota(jnp.int32, (1, LANES), 1)
    col = lax.broadcasted_iota(jnp.int32, (1, 2 * w), 1)
    group = SWA_Q_HEADS // SWA_KV_HEADS
    chains = [(sb, pr) for sb in range(SWA_STEP_BLOCKS) for pr in range(SWA_Q_HEADS // 2)]
    for c, (sb, pr) in enumerate(chains):
        g = (2 * pr) // group
        qp = q_ref[sb * w:(sb + 1) * w, pr * LANES:(pr + 1) * LANES]
        zero = jnp.zeros_like(qp)
        qq = jnp.concatenate([jnp.where(lane < HEAD_DIM, qp, zero),
                              jnp.where(lane >= HEAD_DIM, qp, zero)], axis=0)
        kd = kall[sb * w:(sb + 2) * w, g * LANES:(g + 1) * LANES]
        st_scr[c] = _nt_dot(kd, qq)
    for c, (sb, pr) in enumerate(chains):
        g = (2 * pr) // group
        st = st_scr[c] + bias_ref[pr]
        if sb == 0:
            st = jnp.where(keep, st, NEG_INF)
        sink = jnp.where(col < w, sink_ref[2 * pr], sink_ref[2 * pr + 1])
        m = jnp.maximum(jnp.max(st, axis=0, keepdims=True), sink)
        pt = jnp.exp2(st - m).astype(BF16)
        acc = _dot(vall[g * VT_ROWS:(g + 1) * VT_ROWS, sb * w:(sb + 2) * w], pt)
        denom = acc[HEAD_DIM:HEAD_DIM + 1] + jnp.exp2(sink - m)
        o = acc[:HEAD_DIM] / denom
        oo = jnp.concatenate([o[:, :w], o[:, w:]], axis=0)
        o_ref[sb * w:(sb + 1) * w, pr * LANES:(pr + 1) * LANES] = oo.T.astype(BF16)


def _swa_call(sinks, qb, kbd, vbt, bias_t):
    s = qb.shape[0]
    w = SWA_WINDOW
    n = SWA_STEP_BLOCKS
    assert s % (n * w) == 0
    prev_blk = lambda i: jnp.maximum(n * i - 1, 0)
    grid_spec = pltpu.PrefetchScalarGridSpec(
        num_scalar_prefetch=1,
        grid=(s // (n * w),),
        in_specs=[
            pl.BlockSpec((n * w, _W_QB), lambda i, c: (i, 0)),
            pl.BlockSpec((w, _W_KB), lambda i, c: (prev_blk(i), 0)),
            pl.BlockSpec((n * w, _W_KB), lambda i, c: (i, 0)),
            pl.BlockSpec((SWA_KV_HEADS * VT_ROWS, w), lambda i, c: (0, prev_blk(i))),
            pl.BlockSpec((SWA_KV_HEADS * VT_ROWS, n * w), lambda i, c: (0, i)),
            pl.BlockSpec(bias_t.shape, lambda i, c: (0, 0, 0)),
        ],
        out_specs=pl.BlockSpec((n * w, _W_QB), lambda i, c: (i, 0)),
        scratch_shapes=[pltpu.VMEM((n * SWA_Q_HEADS // 2, 2 * w, 2 * w), F32)],
    )
    return pl.pallas_call(
        _swa_kernel,
        grid_spec=grid_spec,
        out_shape=jax.ShapeDtypeStruct((s, _W_QB), BF16),
        compiler_params=pltpu.CompilerParams(dimension_semantics=("arbitrary",), vmem_limit_bytes=VMEM_LIMIT),
        name="swa",
    )(sinks, qb, kbd, kbd, vbt, vbt, bias_t)


def _post_kernel(x_ref, ya_ref, yb_ref, ga_ref, gb_ref, wa_ref, wb_ref, wo_ref, gffn_ref, w1_ref, w2_ref, o_ref,
                 *, tf):
    merged = ga_ref[...] * _dot(ya_ref[...], wa_ref[...]) + gb_ref[...] * _dot(yb_ref[...], wb_ref[...])
    xn = x_ref[...] + _dot(merged.astype(BF16), wo_ref[...])
    h = _rms_norm_rows(xn, gffn_ref[...]).astype(BF16)
    acc = xn
    for c in range(w1_ref.shape[1] // tf):
        u = jnp.maximum(_dot(h, w1_ref[:, c * tf:(c + 1) * tf]), 0.0)
        acc = acc + _dot((u * u).astype(BF16), w2_ref[c * tf:(c + 1) * tf, :])
    o_ref[...] = acc


def _post_call(l, x, ya, yb, ga, gb, wa, wb, wo, gffn, w1, w2, tm=512, tf=1024):
    s, d = x.shape
    full = lambda a: _layer_spec(a, l, pipeline_mode=pl.Buffered(1))
    rows = lambda a: pl.BlockSpec((tm, a.shape[1]), lambda i: (i, 0))
    return pl.pallas_call(
        functools.partial(_post_kernel, tf=tf),
        grid=(s // tm,),
        in_specs=[rows(x), rows(ya), rows(yb), rows(ga), rows(gb),
                  full(wa), full(wb), full(wo), full(gffn), full(w1), full(w2)],
        out_specs=pl.BlockSpec((tm, d), lambda i: (i, 0)),
        out_shape=jax.ShapeDtypeStruct((s, d), F32),
        compiler_params=pltpu.CompilerParams(dimension_semantics=("arbitrary",), vmem_limit_bytes=VMEM_LIMIT),
        name="post",
    )(x, ya, yb, ga, gb, wa, wb, wo, gffn, w1, w2)


def _bias_by_distance(table, lo, hi, keep):
    dist = np.arange(lo, hi)
    vals = jnp.take(table, jnp.asarray(_rel_bucket_np(dist).astype(np.int32)), axis=0).T
    return jnp.where(jnp.asarray(keep(dist))[None, :], vals, NEG_INF)


def _toeplitz(w, n):
    lead = w.shape[:-1]
    w_pad = jnp.concatenate([w, jnp.zeros(lead + (1,), w.dtype)], axis=-1)
    skew = jnp.tile(w_pad, (1,) * len(lead) + (n,))[..., :n * (2 * n - 1)].reshape(lead + (n, 2 * n - 1))
    return skew[..., n - 1:]


def _bias_tables(rel_bias):
    blk, w = MOBA_BLOCK, SWA_WINDOW
    vec = _bias_by_distance(rel_bias[:, :MOBA_HEADS], 1 - blk, MOBA_NEAR * blk, lambda dist: dist >= 0)
    diag = jnp.stack([vec[:, d * blk:d * blk + 2 * blk - 1] for d in range(MOBA_NEAR)], axis=1)
    moba = _toeplitz(diag, blk)
    moba = (moba.reshape(HEAD_PAIRS, 2, MOBA_NEAR, blk, blk).transpose(2, 0, 3, 1, 4)
            .reshape(MOBA_NEAR, HEAD_PAIRS, blk, 2 * blk))
    cfar = rel_bias[N_REL_BUCKETS - 1, :MOBA_HEADS]

    vec = _bias_by_distance(rel_bias[:, MOBA_HEADS:], 1 - w, 3 * w, lambda dist: (dist >= 0) & (dist < w))
    swa = _toeplitz(vec, 2 * w)[:, :, :w]
    swa = swa.reshape(SWA_Q_HEADS // 2, 2, 2 * w, w).transpose(0, 2, 1, 3).reshape(SWA_Q_HEADS // 2, 2 * w, 2 * w)
    return (moba * LOG2E).astype(F32), (cfar * LOG2E).astype(F32), (swa * LOG2E).astype(F32)


def kernel(x, rel_bias, g_mix, w_in, b_gate, q_norm_a, k_norm_a, q_norm_b, k_norm_b, sinks,
           w_branch_a, w_branch_b, w_out, g_ffn, w_ff1, w_ff2):
    b, s, d = x.shape
    assert b == 1 and s % MOBA_BLOCK == 0 and s // MOBA_BLOCK >= MOBA_NEAR
    depth = w_in.shape[0]
    bias_moba, cfar, bias_swa = _bias_tables(rel_bias)
    row = lambda a: a[:, None, :]
    gains = lambda g, width: row(jnp.tile(g, (1, width // HEAD_DIM)))
    w_in_b, wa_b, wb_b, wo_b, w1_b, w2_b = (a.astype(BF16) for a in (w_in, w_branch_a, w_branch_b, w_out, w_ff1, w_ff2))
    gqa, gka = gains(q_norm_a, _W_QA), gains(k_norm_a, _W_KA)
    gqb, gkb = gains(q_norm_b, _W_QB), gains(k_norm_b, SWA_KV_HEADS * HEAD_DIM)
    sinks2 = sinks * LOG2E
    xs = x[0]
    for l in range(depth):
        qa, ka, kmean, vat, qb, kbd, vbt, ga, gb = _proj_call(l, xs, row(g_mix), w_in_b, row(b_gate), gqa, gka, gqb, gkb)
        ya = _moba_call(cfar, qa, ka, vat, kmean.reshape(kmean.shape[0], kmean.shape[2]), bias_moba)
        yb = _swa_call(sinks2[l], qb, kbd, vbt, bias_swa)
        xs = _post_call(l, xs, ya, yb, ga, gb, wa_b, wb_b, wo_b, row(g_ffn), w1_b, w2_b)
    return xs[None]
```

```python
import functools

import jax
import jax.numpy as jnp
import numpy as np
from jax import lax
from jax.experimental import pallas as pl
from jax.experimental.pallas import tpu as pltpu

HEAD_DIM = 64
MOBA_HEADS = 8
MOBA_BLOCK = 256
MOBA_TOPK = 3
SWA_Q_HEADS = 8
SWA_KV_HEADS = 2
SWA_WINDOW = 128
SWA_STEP_BLOCKS = 2
N_REL_BUCKETS = 32
REL_MAX_DISTANCE = 2048
NORM_EPS = 1e-6
LOG2E = 1.4426950408889634
ATTN_SCALE = HEAD_DIM ** -0.5 * LOG2E

LANES = 128
HEAD_PAIRS = MOBA_HEADS // 2
VT_ROWS = 80
MOBA_NEAR = 7
MOBA_FAR_UNROLL = 4
assert MOBA_FAR_UNROLL == MOBA_NEAR // 2 + 1
VMEM_LIMIT = 56 * 1024 * 1024

BF16 = jnp.bfloat16
F32 = jnp.float32
NEG_INF = float("-inf")
M_INIT = -1e30


def _rel_bucket_np(dist):
    n = np.maximum(dist, 0)
    exact = N_REL_BUCKETS // 2
    nf = np.maximum(n, 1).astype(np.float32)
    large = exact + (np.log(nf / np.float32(exact)) / np.float32(np.log(REL_MAX_DISTANCE / exact))
                     * np.float32(N_REL_BUCKETS - exact)).astype(np.int32)
    large = np.minimum(large, N_REL_BUCKETS - 1)
    return np.where(n < exact, n, large)


def _nt_dot(a, b):
    return lax.dot_general(a, b, (((1,), (1,)), ((), ())), preferred_element_type=F32)


def _dot(a, b):
    return jnp.dot(a, b, preferred_element_type=F32)


def _rms_norm_rows(xf, g):
    ms = jnp.mean(xf * xf, axis=-1, keepdims=True)
    return xf * lax.rsqrt(ms + NORM_EPS) * g


def _head_norm(acc, g2):
    w = acc.shape[1]
    lane = lax.broadcasted_iota(jnp.int32, (1, LANES), 1)
    lo = lane < HEAD_DIM
    outs = []
    for b in range(w // LANES):
        xb = acc[:, b * LANES:(b + 1) * LANES]
        x2 = xb * xb
        s_lo = jnp.sum(jnp.where(lo, x2, 0.0), axis=-1, keepdims=True)
        s_hi = jnp.sum(jnp.where(lo, 0.0, x2), axis=-1, keepdims=True)
        r = jnp.where(lo, lax.rsqrt(s_lo / HEAD_DIM + NORM_EPS), lax.rsqrt(s_hi / HEAD_DIM + NORM_EPS))
        outs.append(xb * r * g2[:, b * LANES:(b + 1) * LANES])
    return outs[0] if len(outs) == 1 else jnp.concatenate(outs, axis=1)


def _v_transposed(v, n_heads):
    rows = v.shape[0]
    vt = v.T
    r = lax.broadcasted_iota(jnp.int32, (VT_ROWS - HEAD_DIM, rows), 0)
    aug = jnp.where(r == 0, 1.0, 0.0).astype(F32)
    parts = []
    for h in range(n_heads):
        parts.append(vt[h * HEAD_DIM:(h + 1) * HEAD_DIM])
        parts.append(aug)
    return jnp.concatenate(parts, axis=0).astype(BF16)


_W_QA, _W_KA, _W_VA, _W_QB, _W_KB, _W_VB, _W_GA, _W_GB = 512, 512, 512, 512, 256, 128, 1024, 1024
_PROJ_COLS = np.cumsum([0, _W_QA, _W_KA, _W_VA, _W_QB, SWA_KV_HEADS * HEAD_DIM, _W_VB, _W_GA, _W_GB]).tolist()


def _proj_kernel(x_ref, gmix_ref, w_ref, bgate_ref, gqa_ref, gka_ref, gqb_ref, gkb_ref,
                 qa_ref, ka_ref, kmean_ref, vat_ref, qb_ref, kb_ref, vbt_ref, ga_ref, gb_ref):
    h = _rms_norm_rows(x_ref[...], gmix_ref[...]).astype(BF16)
    c = _PROJ_COLS

    def seg(k):
        return _dot(h, w_ref[:, c[k]:c[k + 1]])

    qa_ref[...] = (_head_norm(seg(0), gqa_ref[...]) * ATTN_SCALE).astype(BF16)
    ka = _head_norm(seg(1), gka_ref[...])
    ka_ref[...] = ka.astype(BF16)
    kmean_ref[0] = jnp.mean(ka, axis=0, keepdims=True)
    vat_ref[...] = _v_transposed(seg(2), MOBA_HEADS)
    qb_ref[...] = (_head_norm(seg(3), gqb_ref[...]) * ATTN_SCALE).astype(BF16)
    kb = _head_norm(seg(4), gkb_ref[...])
    kb_swapped = pltpu.roll(kb, HEAD_DIM, axis=1)
    lo = lax.broadcasted_iota(jnp.int32, (1, LANES), 1) < HEAD_DIM
    kb_ref[...] = jnp.concatenate([jnp.where(lo, kb, kb_swapped), jnp.where(lo, kb_swapped, kb)], axis=1).astype(BF16)
    vbt_ref[...] = _v_transposed(seg(5), SWA_KV_HEADS)
    bg = bgate_ref[...]
    d = _W_GA
    ga_ref[...] = 1.0 / (1.0 + jnp.exp(-(seg(6) + bg[:, :d])))
    gb_ref[...] = 1.0 / (1.0 + jnp.exp(-(seg(7) + bg[:, d:])))


def _layer_spec(a, l, **kw):
    return pl.BlockSpec((None,) + a.shape[1:], lambda i: (l,) + (0,) * (a.ndim - 1), **kw)


def _proj_call(l, x, gmix, w, bgate, gqa, gka, gqb, gkb):
    s, d = x.shape
    tm = MOBA_BLOCK
    nblk = s // tm
    full = lambda a: _layer_spec(a, l)
    rows = lambda width: pl.BlockSpec((tm, width), lambda i: (i, 0))
    cols = lambda height: pl.BlockSpec((height, tm), lambda i: (0, i))
    out_shape = (
        jax.ShapeDtypeStruct((s, _W_QA), BF16),
        jax.ShapeDtypeStruct((s, _W_KA), BF16),
        jax.ShapeDtypeStruct((nblk, 1, _W_KA), F32),
        jax.ShapeDtypeStruct((MOBA_HEADS * VT_ROWS, s), BF16),
        jax.ShapeDtypeStruct((s, _W_QB), BF16),
        jax.ShapeDtypeStruct((s, _W_KB), BF16),
        jax.ShapeDtypeStruct((SWA_KV_HEADS * VT_ROWS, s), BF16),
        jax.ShapeDtypeStruct((s, _W_GA), F32),
        jax.ShapeDtypeStruct((s, _W_GB), F32),
    )
    out_specs = (
        rows(_W_QA), rows(_W_KA), pl.BlockSpec((1, 1, _W_KA), lambda i: (i, 0, 0)),
        cols(MOBA_HEADS * VT_ROWS), rows(_W_QB), rows(_W_KB), cols(SWA_KV_HEADS * VT_ROWS),
        rows(_W_GA), rows(_W_GB),
    )
    return pl.pallas_call(
        _proj_kernel,
        grid=(nblk,),
        in_specs=[rows(d), full(gmix), full(w), full(bgate), full(gqa), full(gka), full(gqb), full(gkb)],
        out_specs=out_specs,
        out_shape=out_shape,
        compiler_params=pltpu.CompilerParams(dimension_semantics=("arbitrary",), vmem_limit_bytes=VMEM_LIMIT),
        name="proj",
    )(x, gmix, w, bgate, gqa, gka, gqb, gkb)


def _moba_kernel(cfar_ref, q_ref, k_ref, vt_ref, kmean_ref, diag_ref, o_ref,
                 qq_scr, sel_scr, m_scr, acc_scr, sta_scr, stb_scr, bias_ref, *, nblk):
    p = pl.program_id(0)
    i = pl.program_id(1)
    blk = MOBA_BLOCK

    @pl.when(i == 0)
    def _():
        for d in range(MOBA_NEAR):
            for hh in range(2):
                skew = pltpu.roll(jnp.broadcast_to(diag_ref[d, hh:hh + 1, :], (blk, 2 * blk)), 0, 1,
                                  stride=1, stride_axis=0)
                bias_ref[d, :, hh * blk:(hh + 1) * blk] = skew[:, :blk]

    q = q_ref[...]
    lane = lax.broadcasted_iota(jnp.int32, (1, LANES), 1)
    km = kmean_ref[...]
    km_hi = km.astype(BF16)
    km_lo = (km - km_hi.astype(F32)).astype(BF16)
    n_iota = lax.broadcasted_iota(jnp.int32, (nblk, blk), 0)

    qms = []
    for hh in range(2):
        hmask = (lane < HEAD_DIM) if hh == 0 else (lane >= HEAD_DIM)
        qms.append(jnp.where(hmask, q, jnp.zeros_like(q)))
        qq_scr[hh * blk:(hh + 1) * blk, :] = qms[hh]

    gates = [_nt_dot(km_hi, qms[hh]) + _nt_dot(km_lo, qms[hh]) for hh in range(2)]

    def select_blocks():
        for hh in range(2):
            cols = slice(hh * blk, (hh + 1) * blk)
            g = jnp.where(n_iota < i, gates[hh], NEG_INF)
            sel = jnp.zeros((nblk, blk), F32)
            for _ in range(MOBA_TOPK):
                mx = jnp.max(g, axis=0, keepdims=True)
                idx = jnp.min(jnp.where(g == mx, n_iota, nblk), axis=0, keepdims=True)
                hit = n_iota == idx
                valid = jnp.where(mx > NEG_INF, 1.0, 0.0)
                sel = jnp.maximum(sel, jnp.where(hit, valid, 0.0))
                g = jnp.where(hit, NEG_INF, g)
            sel_scr[:, cols] = jnp.where(n_iota == i, 1.0, sel)
            acc_scr[hh] = jnp.zeros((VT_ROWS, blk), F32)
        m_scr[...] = jnp.full((1, 2 * blk), M_INIT, F32)

    col = lax.broadcasted_iota(jnp.int32, (1, 2 * blk), 1)
    cfar = jnp.where(col < blk, cfar_ref[2 * p], cfar_ref[2 * p + 1])
    n_far = jnp.maximum(i - (MOBA_NEAR - 1), 0)

    def scores(first, nb, st_ref):
        off = pl.multiple_of(first * blk, blk)
        st_ref[:nb * blk, :] = _nt_dot(k_ref[pl.ds(off, nb * blk), :], qq_scr[...])

    def attend(first, nb, st_ref, near):
        ms, pvs = [], []
        for u in range(nb):
            j = first + u
            s_u = st_ref[u * blk:(u + 1) * blk, :]
            if near:
                s_u = s_u + bias_ref[jnp.clip(i - j, 0, MOBA_NEAR - 1)]
                on = sel_scr[pl.ds(j, 1), :] > 0.5
                m_u = jnp.max(s_u, axis=0, keepdims=True)
                ms.append(jnp.where(on, m_u, NEG_INF))
            else:
                on = sel_scr[pl.ds(j, 1), :] > jnp.where(j < n_far, 0.5, 2.0)
                m_u = jnp.max(s_u, axis=0, keepdims=True)
                ms.append(jnp.where(on, m_u + cfar, NEG_INF))
            pt = jnp.exp2(s_u - m_u).astype(BF16)
            ko = pl.multiple_of(j * blk, blk)
            pvs.append([_dot(vt_ref[hh * VT_ROWS:(hh + 1) * VT_ROWS, pl.ds(ko, blk)], pt[:, hh * blk:(hh + 1) * blk])
                        for hh in range(2)])
        m_old = m_scr[...]
        m_new = m_old
        for m_u in ms:
            m_new = jnp.maximum(m_new, m_u)
        alpha = jnp.exp2(m_old - m_new)
        ws = [jnp.exp2(m_u - m_new) for m_u in ms]
        for hh in range(2):
            cols = slice(hh * blk, (hh + 1) * blk)
            acc = acc_scr[hh] * alpha[:, cols]
            for w_u, pv in zip(ws, pvs):
                acc = acc + pv[hh] * w_u[:, cols]
            acc_scr[hh] = acc
        m_scr[...] = m_new

    fu = MOBA_FAR_UNROLL
    n_groups = (n_far + fu - 1) // fu
    near0 = jnp.maximum(i - (MOBA_NEAR - 1), 0)
    nb1 = MOBA_NEAR - fu

    def item_start(g):
        return jnp.where(g < n_groups, g * fu, near0)

    def near_tail(st_x, st_y):
        scores(near0 + fu, nb1, st_y)
        attend(near0, fu, st_x, True)
        attend(near0 + fu, nb1, st_y, True)

    scores(item_start(0), fu, sta_scr)
    select_blocks()

    def pair_body(t, carry):
        scores((2 * t + 1) * fu, fu, stb_scr)
        attend(2 * t * fu, fu, sta_scr, False)
        scores(item_start(2 * t + 2), fu, sta_scr)
        attend((2 * t + 1) * fu, fu, stb_scr, False)
        return carry

    lax.fori_loop(0, n_groups // 2, pair_body, 0)

    @pl.when(n_groups % 2 == 0)
    def _():
        near_tail(sta_scr, stb_scr)

    @pl.when(n_groups % 2 == 1)
    def _():
        scores(near0, fu, stb_scr)
        attend((n_groups - 1) * fu, fu, sta_scr, False)
        near_tail(stb_scr, sta_scr)

    outs = []
    for hh in range(2):
        a = acc_scr[hh]
        outs.append(a[:HEAD_DIM] / a[HEAD_DIM:HEAD_DIM + 1])
    o_ref[...] = jnp.concatenate(outs, axis=0).T.astype(BF16)


def _moba_call(cfar, qa, ka, vat, kmean, bias_t):
    s = qa.shape[0]
    blk = MOBA_BLOCK
    nblk = s // blk
    grid_spec = pltpu.PrefetchScalarGridSpec(
        num_scalar_prefetch=1,
        grid=(HEAD_PAIRS, nblk),
        in_specs=[
            pl.BlockSpec((blk, LANES), lambda p, i, c: (i, p)),
            pl.BlockSpec((s, LANES), lambda p, i, c: (0, p)),
            pl.BlockSpec((2 * VT_ROWS, s), lambda p, i, c: (p, 0)),
            pl.BlockSpec((nblk, LANES), lambda p, i, c: (0, p)),
            pl.BlockSpec((MOBA_NEAR, None, 2, 2 * blk), lambda p, i, c: (0, p, 0, 0)),
        ],
        out_specs=pl.BlockSpec((blk, LANES), lambda p, i, c: (i, p)),
        scratch_shapes=[
            pltpu.VMEM((2 * blk, LANES), BF16),
            pltpu.VMEM((nblk, 2 * blk), F32),
            pltpu.VMEM((1, 2 * blk), F32),
            pltpu.VMEM((2, VT_ROWS, blk), F32),
            pltpu.VMEM((MOBA_FAR_UNROLL * blk, 2 * blk), F32),
            pltpu.VMEM((MOBA_FAR_UNROLL * blk, 2 * blk), F32),
            pltpu.VMEM((MOBA_NEAR, blk, 2 * blk), F32),
        ],
    )
    return pl.pallas_call(
        functools.partial(_moba_kernel, nblk=nblk),
        grid_spec=grid_spec,
        out_shape=jax.ShapeDtypeStruct((s, MOBA_HEADS * HEAD_DIM), BF16),
        compiler_params=pltpu.CompilerParams(dimension_semantics=("arbitrary", "arbitrary"),
                                             vmem_limit_bytes=VMEM_LIMIT),
        name="moba",
    )(cfar, qa, ka, vat, kmean, bias_t)


def _swa_kernel(sink_ref, q_ref, kprev_ref, kcur_ref, vprev_ref, vcur_ref, bias_ref, o_ref, st_scr):
    b = pl.program_id(0)
    w = SWA_WINDOW
    kall = jnp.concatenate([kprev_ref[...], kcur_ref[...]], axis=0)
    vall = jnp.concatenate([vprev_ref[...], vcur_ref[...]], axis=1)
    row = lax.broadcasted_iota(jnp.int32, (2 * w, 1), 0)
    keep = row >= jnp.where(b > 0, 0, w)
    lane = lax.broadcasted_iota(jnp.int32, (1, LANES), 1)
    col = lax.broadcasted_iota(jnp.int32, (1, 2 * w), 1)
    group = SWA_Q_HEADS // SWA_KV_HEADS
    chains = [(sb, pr) for sb in range(SWA_STEP_BLOCKS) for pr in range(SWA_Q_HEADS // 2)]
    for c, (sb, pr) in enumerate(chains):
        g = (2 * pr) // group
        qp = q_ref[sb * w:(sb + 1) * w, pr * LANES:(pr + 1) * LANES]
        zero = jnp.zeros_like(qp)
        qq = jnp.concatenate([jnp.where(lane < HEAD_DIM, qp, zero),
                              jnp.where(lane >= HEAD_DIM, qp, zero)], axis=0)
        kd = kall[sb * w:(sb + 2) * w, g * LANES:(g + 1) * LANES]
        st_scr[c] = _nt_dot(kd, qq)
    for c, (sb, pr) in enumerate(chains):
        g = (2 * pr) // group
        st = st_scr[c] + bias_ref[pr]
        if sb == 0:
            st = jnp.where(keep, st, NEG_INF)
        sink = jnp.where(col < w, sink_ref[2 * pr], sink_ref[2 * pr + 1])
        m = jnp.maximum(jnp.max(st, axis=0, keepdims=True), sink)
        pt = jnp.exp2(st - m).astype(BF16)
        acc = _dot(vall[g * VT_ROWS:(g + 1) * VT_ROWS, sb * w:(sb + 2) * w], pt)
        denom = acc[HEAD_DIM:HEAD_DIM + 1] + jnp.exp2(sink - m)
        o = acc[:HEAD_DIM] / denom
        oo = jnp.concatenate([o[:, :w], o[:, w:]], axis=0)
        o_ref[sb * w:(sb + 1) * w, pr * LANES:(pr + 1) * LANES] = oo.T.astype(BF16)


def _swa_call(sinks, qb, kbd, vbt, bias_t):
    s = qb.shape[0]
    w = SWA_WINDOW
    n = SWA_STEP_BLOCKS
    assert s % (n * w) == 0
    prev_blk = lambda i: jnp.maximum(n * i - 1, 0)
    grid_spec = pltpu.PrefetchScalarGridSpec(
        num_scalar_prefetch=1,
        grid=(s // (n * w),),
        in_specs=[
            pl.BlockSpec((n * w, _W_QB), lambda i, c: (i, 0)),
            pl.BlockSpec((w, _W_KB), lambda i, c: (prev_blk(i), 0)),
            pl.BlockSpec((n * w, _W_KB), lambda i, c: (i, 0)),
            pl.BlockSpec((SWA_KV_HEADS * VT_ROWS, w), lambda i, c: (0, prev_blk(i))),
            pl.BlockSpec((SWA_KV_HEADS * VT_ROWS, n * w), lambda i, c: (0, i)),
            pl.BlockSpec(bias_t.shape, lambda i, c: (0, 0, 0)),
        ],
        out_specs=pl.BlockSpec((n * w, _W_QB), lambda i, c: (i, 0)),
        scratch_shapes=[pltpu.VMEM((n * SWA_Q_HEADS // 2, 2 * w, 2 * w), F32)],
    )
    return pl.pallas_call(
        _swa_kernel,
        grid_spec=grid_spec,
        out_shape=jax.ShapeDtypeStruct((s, _W_QB), BF16),
        compiler_params=pltpu.CompilerParams(dimension_semantics=("arbitrary",), vmem_limit_bytes=VMEM_LIMIT),
        name="swa",
    )(sinks, qb, kbd, kbd, vbt, vbt, bias_t)


def _post_kernel(x_ref, ya_ref, yb_ref, ga_ref, gb_ref, wa_ref, wb_ref, wo_ref, gffn_ref, w1_ref, w2_ref, o_ref,
                 *, tf):
    merged = ga_ref[...] * _dot(ya_ref[...], wa_ref[...]) + gb_ref[...] * _dot(yb_ref[...], wb_ref[...])
    xn = x_ref[...] + _dot(merged.astype(BF16), wo_ref[...])
    h = _rms_norm_rows(xn, gffn_ref[...]).astype(BF16)
    acc = xn
    for c in range(w1_ref.shape[1] // tf):
        u = jnp.maximum(_dot(h, w1_ref[:, c * tf:(c + 1) * tf]), 0.0)
        acc = acc + _dot((u * u).astype(BF16), w2_ref[c * tf:(c + 1) * tf, :])
    o_ref[...] = acc


def _post_call(l, x, ya, yb, ga, gb, wa, wb, wo, gffn, w1, w2, tm=512, tf=1024):
    s, d = x.shape
    full = lambda a: _layer_spec(a, l, pipeline_mode=pl.Buffered(1))
    rows = lambda a: pl.BlockSpec((tm, a.shape[1]), lambda i: (i, 0))
    return pl.pallas_call(
        functools.partial(_post_kernel, tf=tf),
        grid=(s // tm,),
        in_specs=[rows(x), rows(ya), rows(yb), rows(ga), rows(gb),
                  full(wa), full(wb), full(wo), full(gffn), full(w1), full(w2)],
        out_specs=pl.BlockSpec((tm, d), lambda i: (i, 0)),
        out_shape=jax.ShapeDtypeStruct((s, d), F32),
        compiler_params=pltpu.CompilerParams(dimension_semantics=("arbitrary",), vmem_limit_bytes=VMEM_LIMIT),
        name="post",
    )(x, ya, yb, ga, gb, wa, wb, wo, gffn, w1, w2)


def _bias_by_distance(table, lo, hi, keep):
    dist = np.arange(lo, hi)
    vals = jnp.take(table, jnp.asarray(_rel_bucket_np(dist).astype(np.int32)), axis=0).T
    return jnp.where(jnp.asarray(keep(dist))[None, :], vals, NEG_INF)


def _toeplitz(w, n):
    lead = w.shape[:-1]
    w_pad = jnp.concatenate([w, jnp.zeros(lead + (1,), w.dtype)], axis=-1)
    skew = jnp.tile(w_pad, (1,) * len(lead) + (n,))[..., :n * (2 * n - 1)].reshape(lead + (n, 2 * n - 1))
    return skew[..., n - 1:]


def _bias_tables(rel_bias):
    blk, w = MOBA_BLOCK, SWA_WINDOW
    vec = _bias_by_distance(rel_bias[:, :MOBA_HEADS], 1 - blk, MOBA_NEAR * blk, lambda dist: dist >= 0)
    vec = jnp.concatenate([vec, jnp.zeros((MOBA_HEADS, 1), F32)], axis=1)
    moba = jnp.stack([jnp.roll(vec[:, d * blk:(d + 2) * blk], 1 - blk, axis=1) for d in range(MOBA_NEAR)])
    moba = moba.reshape(MOBA_NEAR, HEAD_PAIRS, 2, 2 * blk)
    cfar = rel_bias[N_REL_BUCKETS - 1, :MOBA_HEADS]

    vec = _bias_by_distance(rel_bias[:, MOBA_HEADS:], 1 - w, 3 * w, lambda dist: (dist >= 0) & (dist < w))
    swa = _toeplitz(vec, 2 * w)[:, :, :w]
    swa = swa.reshape(SWA_Q_HEADS // 2, 2, 2 * w, w).transpose(0, 2, 1, 3).reshape(SWA_Q_HEADS // 2, 2 * w, 2 * w)
    return (moba * LOG2E).astype(F32), (cfar * LOG2E).astype(F32), (swa * LOG2E).astype(F32)


def kernel(x, rel_bias, g_mix, w_in, b_gate, q_norm_a, k_norm_a, q_norm_b, k_norm_b, sinks,
           w_branch_a, w_branch_b, w_out, g_ffn, w_ff1, w_ff2):
    b, s, d = x.shape
    assert b == 1 and s % MOBA_BLOCK == 0 and s // MOBA_BLOCK >= MOBA_NEAR
    depth = w_in.shape[0]
    bias_moba, cfar, bias_swa = _bias_tables(rel_bias)
    row = lambda a: a[:, None, :]
    gains = lambda g, width: row(jnp.tile(g, (1, width // HEAD_DIM)))
    w_in_b, wa_b, wb_b, wo_b, w1_b, w2_b = (a.astype(BF16) for a in (w_in, w_branch_a, w_branch_b, w_out, w_ff1, w_ff2))
    gqa, gka = gains(q_norm_a, _W_QA), gains(k_norm_a, _W_KA)
    gqb, gkb = gains(q_norm_b, _W_QB), gains(k_norm_b, SWA_KV_HEADS * HEAD_DIM)
    sinks2 = sinks * LOG2E
    xs = x[0]
    for l in range(depth):
        qa, ka, kmean, vat, qb, kbd, vbt, ga, gb = _proj_call(l, xs, row(g_mix), w_in_b, row(b_gate), gqa, gka, gqb, gkb)
        ya = _moba_call(cfar, qa, ka, vat, kmean.reshape(kmean.shape[0], kmean.shape[2]), bias_moba)
        yb = _swa_call(sinks2[l], qb, kbd, vbt, bias_swa)
        xs = _post_call(l, xs, ya, yb, ga, gb, wa_b, wb_b, wo_b, row(g_ffn), w1_b, w2_b)
    return xs[None]
```

```python
import functools

import jax
import jax.numpy as jnp
import numpy as np
from jax import lax
from jax.experimental import pallas as pl
from jax.experimental.pallas import tpu as pltpu

HEAD_DIM = 64
MOBA_HEADS = 8
MOBA_BLOCK = 256
MOBA_TOPK = 3
SWA_Q_HEADS = 8
SWA_KV_HEADS = 2
SWA_WINDOW = 128
SWA_STEP_BLOCKS = 2
N_REL_BUCKETS = 32
REL_MAX_DISTANCE = 2048
NORM_EPS = 1e-6
LOG2E = 1.4426950408889634
ATTN_SCALE = HEAD_DIM ** -0.5 * LOG2E

LANES = 128
HEAD_PAIRS = MOBA_HEADS // 2
VT_ROWS = 80
MOBA_NEAR = 7
MOBA_FAR_UNROLL = 4
assert MOBA_FAR_UNROLL == MOBA_NEAR // 2 + 1
VMEM_LIMIT = 56 * 1024 * 1024

BF16 = jnp.bfloat16
F32 = jnp.float32
NEG_INF = float("-inf")
M_INIT = -1e30


def _rel_bucket_np(dist):
    n = np.maximum(dist, 0)
    exact = N_REL_BUCKETS // 2
    nf = np.maximum(n, 1).astype(np.float32)
    large = exact + (np.log(nf / np.float32(exact)) / np.float32(np.log(REL_MAX_DISTANCE / exact))
                     * np.float32(N_REL_BUCKETS - exact)).astype(np.int32)
    large = np.minimum(large, N_REL_BUCKETS - 1)
    return np.where(n < exact, n, large)


def _nt_dot(a, b):
    return lax.dot_general(a, b, (((1,), (1,)), ((), ())), preferred_element_type=F32)


def _dot(a, b):
    return jnp.dot(a, b, preferred_element_type=F32)


def _rms_norm_rows(xf, g):
    ms = jnp.mean(xf * xf, axis=-1, keepdims=True)
    return xf * lax.rsqrt(ms + NORM_EPS) * g


def _head_norm(acc, g2):
    w = acc.shape[1]
    lane = lax.broadcasted_iota(jnp.int32, (1, LANES), 1)
    lo = lane < HEAD_DIM
    outs = []
    for b in range(w // LANES):
        xb = acc[:, b * LANES:(b + 1) * LANES]
        x2 = xb * xb
        s_lo = jnp.sum(jnp.where(lo, x2, 0.0), axis=-1, keepdims=True)
        s_hi = jnp.sum(jnp.where(lo, 0.0, x2), axis=-1, keepdims=True)
        r = jnp.where(lo, lax.rsqrt(s_lo / HEAD_DIM + NORM_EPS), lax.rsqrt(s_hi / HEAD_DIM + NORM_EPS))
        outs.append(xb * r * g2[:, b * LANES:(b + 1) * LANES])
    return outs[0] if len(outs) == 1 else jnp.concatenate(outs, axis=1)


def _v_transposed(v, n_heads):
    rows = v.shape[0]
    vt = v.T
    r = lax.broadcasted_iota(jnp.int32, (VT_ROWS - HEAD_DIM, rows), 0)
    aug = jnp.where(r == 0, 1.0, 0.0).astype(F32)
    parts = []
    for h in range(n_heads):
        parts.append(vt[h * HEAD_DIM:(h + 1) * HEAD_DIM])
        parts.append(aug)
    return jnp.concatenate(parts, axis=0).astype(BF16)


_W_QA, _W_KA, _W_VA, _W_QB, _W_KB, _W_VB, _W_GA, _W_GB = 512, 512, 512, 512, 256, 128, 1024, 1024
_PROJ_COLS = np.cumsum([0, _W_QA, _W_KA, _W_VA, _W_QB, SWA_KV_HEADS * HEAD_DIM, _W_VB, _W_GA, _W_GB]).tolist()


def _proj_kernel(x_ref, gmix_ref, w_ref, bgate_ref, gqa_ref, gka_ref, gqb_ref, gkb_ref,
                 qa_ref, ka_ref, kmean_ref, vat_ref, qb_ref, kb_ref, vbt_ref, ga_ref, gb_ref):
    h = _rms_norm_rows(x_ref[...], gmix_ref[...]).astype(BF16)
    c = _PROJ_COLS

    def seg(k):
        return _dot(h, w_ref[:, c[k]:c[k + 1]])

    qa_ref[...] = (_head_norm(seg(0), gqa_ref[...]) * ATTN_SCALE).astype(BF16)
    ka = _head_norm(seg(1), gka_ref[...])
    ka_ref[...] = ka.astype(BF16)
    kmean_ref[0] = jnp.mean(ka, axis=0, keepdims=True)
    vat_ref[...] = _v_transposed(seg(2), MOBA_HEADS)
    qb_ref[...] = (_head_norm(seg(3), gqb_ref[...]) * ATTN_SCALE).astype(BF16)
    kb = _head_norm(seg(4), gkb_ref[...])
    kb_swapped = pltpu.roll(kb, HEAD_DIM, axis=1)
    lo = lax.broadcasted_iota(jnp.int32, (1, LANES), 1) < HEAD_DIM
    kb_ref[...] = jnp.concatenate([jnp.where(lo, kb, kb_swapped), jnp.where(lo, kb_swapped, kb)], axis=1).astype(BF16)
    vbt_ref[...] = _v_transposed(seg(5), SWA_KV_HEADS)
    bg = bgate_ref[...]
    d = _W_GA
    ga_ref[...] = 1.0 / (1.0 + jnp.exp(-(seg(6) + bg[:, :d])))
    gb_ref[...] = 1.0 / (1.0 + jnp.exp(-(seg(7) + bg[:, d:])))


def _layer_spec(a, l, **kw):
    return pl.BlockSpec((None,) + a.shape[1:], lambda i: (l,) + (0,) * (a.ndim - 1), **kw)


def _proj_call(l, x, gmix, w, bgate, gqa, gka, gqb, gkb):
    s, d = x.shape
    tm = MOBA_BLOCK
    nblk = s // tm
    full = lambda a: _layer_spec(a, l)
    rows = lambda width: pl.BlockSpec((tm, width), lambda i: (i, 0))
    cols = lambda height: pl.BlockSpec((height, tm), lambda i: (0, i))
    out_shape = (
        jax.ShapeDtypeStruct((s, _W_QA), BF16),
        jax.ShapeDtypeStruct((s, _W_KA), BF16),
        jax.ShapeDtypeStruct((nblk, 1, _W_KA), F32),
        jax.ShapeDtypeStruct((MOBA_HEADS * VT_ROWS, s), BF16),
        jax.ShapeDtypeStruct((s, _W_QB), BF16),
        jax.ShapeDtypeStruct((s, _W_KB), BF16),
        jax.ShapeDtypeStruct((SWA_KV_HEADS * VT_ROWS, s), BF16),
        jax.ShapeDtypeStruct((s, _W_GA), F32),
        jax.ShapeDtypeStruct((s, _W_GB), F32),
    )
    out_specs = (
        rows(_W_QA), rows(_W_KA), pl.BlockSpec((1, 1, _W_KA), lambda i: (i, 0, 0)),
        cols(MOBA_HEADS * VT_ROWS), rows(_W_QB), rows(_W_KB), cols(SWA_KV_HEADS * VT_ROWS),
        rows(_W_GA), rows(_W_GB),
    )
    return pl.pallas_call(
        _proj_kernel,
        grid=(nblk,),
        in_specs=[rows(d), full(gmix), full(w), full(bgate), full(gqa), full(gka), full(gqb), full(gkb)],
        out_specs=out_specs,
        out_shape=out_shape,
        compiler_params=pltpu.CompilerParams(dimension_semantics=("arbitrary",), vmem_limit_bytes=VMEM_LIMIT),
        name="proj",
    )(x, gmix, w, bgate, gqa, gka, gqb, gkb)


def _moba_kernel(cfar_ref, q_ref, k_ref, vt_ref, kmean_ref, diag_ref, o_ref,
                 qq_scr, sel_scr, m_scr, acc_scr, sta_scr, stb_scr, bias_ref, *, nblk):
    p = pl.program_id(0)
    i = pl.program_id(1)
    blk = MOBA_BLOCK

    @pl.when(i == 0)
    def _():
        for d in range(MOBA_NEAR):
            for hh in range(2):
                skew = pltpu.roll(jnp.broadcast_to(diag_ref[d, hh:hh + 1, :], (blk, 2 * blk)), 0, 1,
                                  stride=1, stride_axis=0)
                bias_ref[d, :, hh * blk:(hh + 1) * blk] = skew[:, :blk]

    q = q_ref[...]
    lane = lax.broadcasted_iota(jnp.int32, (1, LANES), 1)
    km = kmean_ref[...]
    km_hi = km.astype(BF16)
    km_lo = (km - km_hi.astype(F32)).astype(BF16)
    n_iota = lax.broadcasted_iota(jnp.int32, (nblk, blk), 0)

    qms = []
    for hh in range(2):
        hmask = (lane < HEAD_DIM) if hh == 0 else (lane >= HEAD_DIM)
        qms.append(jnp.where(hmask, q, jnp.zeros_like(q)))
        qq_scr[hh * blk:(hh + 1) * blk, :] = qms[hh]

    gates = [_nt_dot(km_hi, qms[hh]) + _nt_dot(km_lo, qms[hh]) for hh in range(2)]

    def select_blocks():
        for hh in range(2):
            cols = slice(hh * blk, (hh + 1) * blk)
            g = jnp.where(n_iota < i, gates[hh], NEG_INF)
            sel = jnp.zeros((nblk, blk), F32)
            for _ in range(MOBA_TOPK):
                mx = jnp.max(g, axis=0, keepdims=True)
                idx = jnp.min(jnp.where(g == mx, n_iota, nblk), axis=0, keepdims=True)
                hit = n_iota == idx
                valid = jnp.where(mx > NEG_INF, 1.0, 0.0)
                sel = jnp.maximum(sel, jnp.where(hit, valid, 0.0))
                g = jnp.where(hit, NEG_INF, g)
            sel_scr[:, cols] = jnp.where(n_iota == i, 1.0, sel)
            acc_scr[hh] = jnp.zeros((VT_ROWS, blk), F32)
        m_scr[...] = jnp.full((1, 2 * blk), M_INIT, F32)

    col = lax.broadcasted_iota(jnp.int32, (1, 2 * blk), 1)
    cfar = jnp.where(col < blk, cfar_ref[2 * p], cfar_ref[2 * p + 1])
    n_far = jnp.maximum(i - (MOBA_NEAR - 1), 0)

    def scores(first, nb, st_ref):
        off = pl.multiple_of(first * blk, blk)
        st_ref[:nb * blk, :2 * blk] = _nt_dot(k_ref[pl.ds(off, nb * blk), :], qq_scr[...])

    def attend(first, nb, st_ref, near):
        ms, pvs = [], []
        for u in range(nb):
            j = first + u
            s_u = st_ref[u * blk:(u + 1) * blk, :2 * blk]
            if near:
                s_u = s_u + bias_ref[jnp.clip(i - j, 0, MOBA_NEAR - 1)]
                on = sel_scr[pl.ds(j, 1), :] > 0.5
                m_u = jnp.max(s_u, axis=0, keepdims=True)
                ms.append(jnp.where(on, m_u, NEG_INF))
            else:
                on = sel_scr[pl.ds(j, 1), :] > jnp.where(j < n_far, 0.5, 2.0)
                m_u = jnp.max(s_u, axis=0, keepdims=True)
                ms.append(jnp.where(on, m_u + cfar, NEG_INF))
            pt = jnp.exp2(s_u - m_u).astype(BF16)
            ko = pl.multiple_of(j * blk, blk)
            pvs.append([_dot(vt_ref[hh * VT_ROWS:(hh + 1) * VT_ROWS, pl.ds(ko, blk)], pt[:, hh * blk:(hh + 1) * blk])
                        for hh in range(2)])
        m_old = m_scr[...]
        m_new = m_old
        for m_u in ms:
            m_new = jnp.maximum(m_new, m_u)
        alpha = jnp.exp2(m_old - m_new)
        ws = [jnp.exp2(m_u - m_new) for m_u in ms]
        for hh in range(2):
            cols = slice(hh * blk, (hh + 1) * blk)
            acc = acc_scr[hh] * alpha[:, cols]
            for w_u, pv in zip(ws, pvs):
                acc = acc + pv[hh] * w_u[:, cols]
            acc_scr[hh] = acc
        m_scr[...] = m_new

    fu = MOBA_FAR_UNROLL
    n_groups = (n_far + fu - 1) // fu
    near0 = jnp.maximum(i - (MOBA_NEAR - 1), 0)
    nb1 = MOBA_NEAR - fu

    def item_start(g):
        return jnp.where(g < n_groups, g * fu, near0)

    def near_tail(st_x, st_y):
        scores(near0 + fu, nb1, st_y)
        attend(near0, fu, st_x, True)
        attend(near0 + fu, nb1, st_y, True)

    scores(item_start(0), fu, sta_scr)
    select_blocks()

    def pair_body(t, carry):
        scores((2 * t + 1) * fu, fu, stb_scr)
        attend(2 * t * fu, fu, sta_scr, False)
        scores(item_start(2 * t + 2), fu, sta_scr)
        attend((2 * t + 1) * fu, fu, stb_scr, False)
        return carry

    lax.fori_loop(0, n_groups // 2, pair_body, 0)

    @pl.when(n_groups % 2 == 0)
    def _():
        near_tail(sta_scr, stb_scr)

    @pl.when(n_groups % 2 == 1)
    def _():
        scores(near0, fu, stb_scr)
        attend((n_groups - 1) * fu, fu, sta_scr, False)
        near_tail(stb_scr, sta_scr)

    outs = []
    for hh in range(2):
        a = acc_scr[hh]
        outs.append(a[:HEAD_DIM] / a[HEAD_DIM:HEAD_DIM + 1])
    o_ref[...] = jnp.concatenate(outs, axis=0).T.astype(BF16)


def _moba_call(cfar, qa, ka, vat, kmean, bias_t):
    s = qa.shape[0]
    blk = MOBA_BLOCK
    nblk = s // blk
    grid_spec = pltpu.PrefetchScalarGridSpec(
        num_scalar_prefetch=1,
        grid=(HEAD_PAIRS, nblk),
        in_specs=[
            pl.BlockSpec((blk, LANES), lambda p, i, c: (i, p)),
            pl.BlockSpec((s, LANES), lambda p, i, c: (0, p)),
            pl.BlockSpec((2 * VT_ROWS, s), lambda p, i, c: (p, 0)),
            pl.BlockSpec((nblk, LANES), lambda p, i, c: (0, p)),
            pl.BlockSpec((MOBA_NEAR, None, 2, 2 * blk), lambda p, i, c: (0, p, 0, 0)),
        ],
        out_specs=pl.BlockSpec((blk, LANES), lambda p, i, c: (i, p)),
        scratch_shapes=[
            pltpu.VMEM((2 * blk, LANES), BF16),
            pltpu.VMEM((nblk, 2 * blk), F32),
            pltpu.VMEM((1, 2 * blk), F32),
            pltpu.VMEM((2, VT_ROWS, blk), F32),
            pltpu.VMEM((MOBA_FAR_UNROLL * blk, 2 * blk + LANES), F32),
            pltpu.VMEM((MOBA_FAR_UNROLL * blk, 2 * blk + LANES), F32),
            pltpu.VMEM((MOBA_NEAR, blk, 2 * blk), F32),
        ],
    )
    return pl.pallas_call(
        functools.partial(_moba_kernel, nblk=nblk),
        grid_spec=grid_spec,
        out_shape=jax.ShapeDtypeStruct((s, MOBA_HEADS * HEAD_DIM), BF16),
        compiler_params=pltpu.CompilerParams(dimension_semantics=("arbitrary", "arbitrary"),
                                             vmem_limit_bytes=VMEM_LIMIT),
        name="moba",
    )(cfar, qa, ka, vat, kmean, bias_t)


def _swa_kernel(sink_ref, q_ref, kprev_ref, kcur_ref, vprev_ref, vcur_ref, bias_ref, o_ref, st_scr):
    b = pl.program_id(0)
    w = SWA_WINDOW
    kall = jnp.concatenate([kprev_ref[...], kcur_ref[...]], axis=0)
    vall = jnp.concatenate([vprev_ref[...], vcur_ref[...]], axis=1)
    row = lax.broadcasted_iota(jnp.int32, (2 * w, 1), 0)
    keep = row >= jnp.where(b > 0, 0, w)
    lane = lax.broadcasted_iota(jnp.int32, (1, LANES), 1)
    col = lax.broadcasted_iota(jnp.int32, (1, 2 * w), 1)
    group = SWA_Q_HEADS // SWA_KV_HEADS
    chains = [(sb, pr) for sb in range(SWA_STEP_BLOCKS) for pr in range(SWA_Q_HEADS // 2)]
    for c, (sb, pr) in enumerate(chains):
        g = (2 * pr) // group
        qp = q_ref[sb * w:(sb + 1) * w, pr * LANES:(pr + 1) * LANES]
        zero = jnp.zeros_like(qp)
        qq = jnp.concatenate([jnp.where(lane < HEAD_DIM, qp, zero),
                              jnp.where(lane >= HEAD_DIM, qp, zero)], axis=0)
        kd = kall[sb * w:(sb + 2) * w, g * LANES:(g + 1) * LANES]
        st_scr[c, :, :2 * w] = _nt_dot(kd, qq)
    for c, (sb, pr) in enumerate(chains):
        g = (2 * pr) // group
        st = st_scr[c, :, :2 * w] + bias_ref[pr]
        if sb == 0:
            st = jnp.where(keep, st, NEG_INF)
        sink = jnp.where(col < w, sink_ref[2 * pr], sink_ref[2 * pr + 1])
        m = jnp.maximum(jnp.max(st, axis=0, keepdims=True), sink)
        pt = jnp.exp2(st - m).astype(BF16)
        acc = _dot(vall[g * VT_ROWS:(g + 1) * VT_ROWS, sb * w:(sb + 2) * w], pt)
        denom = acc[HEAD_DIM:HEAD_DIM + 1] + jnp.exp2(sink - m)
        o = acc[:HEAD_DIM] / denom
        oo = jnp.concatenate([o[:, :w], o[:, w:]], axis=0)
        o_ref[sb * w:(sb + 1) * w, pr * LANES:(pr + 1) * LANES] = oo.T.astype(BF16)


def _swa_call(sinks, qb, kbd, vbt, bias_t):
    s = qb.shape[0]
    w = SWA_WINDOW
    n = SWA_STEP_BLOCKS
    assert s % (n * w) == 0
    prev_blk = lambda i: jnp.maximum(n * i - 1, 0)
    grid_spec = pltpu.PrefetchScalarGridSpec(
        num_scalar_prefetch=1,
        grid=(s // (n * w),),
        in_specs=[
            pl.BlockSpec((n * w, _W_QB), lambda i, c: (i, 0)),
            pl.BlockSpec((w, _W_KB), lambda i, c: (prev_blk(i), 0)),
            pl.BlockSpec((n * w, _W_KB), lambda i, c: (i, 0)),
            pl.BlockSpec((SWA_KV_HEADS * VT_ROWS, w), lambda i, c: (0, prev_blk(i))),
            pl.BlockSpec((SWA_KV_HEADS * VT_ROWS, n * w), lambda i, c: (0, i)),
            pl.BlockSpec(bias_t.shape, lambda i, c: (0, 0, 0)),
        ],
        out_specs=pl.BlockSpec((n * w, _W_QB), lambda i, c: (i, 0)),
        scratch_shapes=[pltpu.VMEM((n * SWA_Q_HEADS // 2, 2 * w, 2 * w + LANES), F32)],
    )
    return pl.pallas_call(
        _swa_kernel,
        grid_spec=grid_spec,
        out_shape=jax.ShapeDtypeStruct((s, _W_QB), BF16),
        compiler_params=pltpu.CompilerParams(dimension_semantics=("arbitrary",), vmem_limit_bytes=VMEM_LIMIT),
        name="swa",
    )(sinks, qb, kbd, kbd, vbt, vbt, bias_t)


def _post_kernel(x_ref, ya_ref, yb_ref, ga_ref, gb_ref, wa_ref, wb_ref, wo_ref, gffn_ref, w1_ref, w2_ref, o_ref,
                 *, tf):
    merged = ga_ref[...] * _dot(ya_ref[...], wa_ref[...]) + gb_ref[...] * _dot(yb_ref[...], wb_ref[...])
    xn = x_ref[...] + _dot(merged.astype(BF16), wo_ref[...])
    h = _rms_norm_rows(xn, gffn_ref[...]).astype(BF16)
    acc = xn
    for c in range(w1_ref.shape[1] // tf):
        u = jnp.maximum(_dot(h, w1_ref[:, c * tf:(c + 1) * tf]), 0.0)
        acc = acc + _dot((u * u).astype(BF16), w2_ref[c * tf:(c + 1) * tf, :])
    o_ref[...] = acc


def _post_call(l, x, ya, yb, ga, gb, wa, wb, wo, gffn, w1, w2, tm=512, tf=1024):
    s, d = x.shape
    full = lambda a: _layer_spec(a, l, pipeline_mode=pl.Buffered(1))
    rows = lambda a: pl.BlockSpec((tm, a.shape[1]), lambda i: (i, 0))
    return pl.pallas_call(
        functools.partial(_post_kernel, tf=tf),
        grid=(s // tm,),
        in_specs=[rows(x), rows(ya), rows(yb), rows(ga), rows(gb),
                  full(wa), full(wb), full(wo), full(gffn), full(w1), full(w2)],
        out_specs=pl.BlockSpec((tm, d), lambda i: (i, 0)),
        out_shape=jax.ShapeDtypeStruct((s, d), F32),
        compiler_params=pltpu.CompilerParams(dimension_semantics=("arbitrary",), vmem_limit_bytes=VMEM_LIMIT),
        name="post",
    )(x, ya, yb, ga, gb, wa, wb, wo, gffn, w1, w2)


def _bias_by_distance(table, lo, hi, keep):
    dist = np.arange(lo, hi)
    vals = jnp.take(table, jnp.asarray(_rel_bucket_np(dist).astype(np.int32)), axis=0).T
    return jnp.where(jnp.asarray(keep(dist))[None, :], vals, NEG_INF)


def _toeplitz(w, n):
    lead = w.shape[:-1]
    w_pad = jnp.concatenate([w, jnp.zeros(lead + (1,), w.dtype)], axis=-1)
    skew = jnp.tile(w_pad, (1,) * len(lead) + (n,))[..., :n * (2 * n - 1)].reshape(lead + (n, 2 * n - 1))
    return skew[..., n - 1:]


def _bias_tables(rel_bias):
    blk, w = MOBA_BLOCK, SWA_WINDOW
    vec = _bias_by_distance(rel_bias[:, :MOBA_HEADS], 1 - blk, MOBA_NEAR * blk, lambda dist: dist >= 0)
    vec = jnp.concatenate([vec, jnp.zeros((MOBA_HEADS, 1), F32)], axis=1)
    moba = jnp.stack([jnp.roll(vec[:, d * blk:(d + 2) * blk], 1 - blk, axis=1) for d in range(MOBA_NEAR)])
    moba = moba.reshape(MOBA_NEAR, HEAD_PAIRS, 2, 2 * blk)
    cfar = rel_bias[N_REL_BUCKETS - 1, :MOBA_HEADS]

    vec = _bias_by_distance(rel_bias[:, MOBA_HEADS:], 1 - w, 3 * w, lambda dist: (dist >= 0) & (dist < w))
    swa = _toeplitz(vec, 2 * w)[:, :, :w]
    swa = swa.reshape(SWA_Q_HEADS // 2, 2, 2 * w, w).transpose(0, 2, 1, 3).reshape(SWA_Q_HEADS // 2, 2 * w, 2 * w)
    return (moba * LOG2E).astype(F32), (cfar * LOG2E).astype(F32), (swa * LOG2E).astype(F32)


def kernel(x, rel_bias, g_mix, w_in, b_gate, q_norm_a, k_norm_a, q_norm_b, k_norm_b, sinks,
           w_branch_a, w_branch_b, w_out, g_ffn, w_ff1, w_ff2):
    b, s, d = x.shape
    assert b == 1 and s % MOBA_BLOCK == 0 and s // MOBA_BLOCK >= MOBA_NEAR
    depth = w_in.shape[0]
    bias_moba, cfar, bias_swa = _bias_tables(rel_bias)
    row = lambda a: a[:, None, :]
    gains = lambda g, width: row(jnp.tile(g, (1, width // HEAD_DIM)))
    w_in_b, wa_b, wb_b, wo_b, w1_b, w2_b = (a.astype(BF16) for a in (w_in, w_branch_a, w_branch_b, w_out, w_ff1, w_ff2))
    gqa, gka = gains(q_norm_a, _W_QA), gains(k_norm_a, _W_KA)
    gqb, gkb = gains(q_norm_b, _W_QB), gains(k_norm_b, SWA_KV_HEADS * HEAD_DIM)
    sinks2 = sinks * LOG2E
    xs = x[0]
    for l in range(depth):
        qa, ka, kmean, vat, qb, kbd, vbt, ga, gb = _proj_call(l, xs, row(g_mix), w_in_b, row(b_gate), gqa, gka, gqb, gkb)
        ya = _moba_call(cfar, qa, ka, vat, kmean.reshape(kmean.shape[0], kmean.shape[2]), bias_moba)
        yb = _swa_call(sinks2[l], qb, kbd, vbt, bias_swa)
        xs = _post_call(l, xs, ya, yb, ga, gb, wa_b, wb_b, wo_b, row(g_ffn), w1_b, w2_b)
    return xs[None]
```

```python
import functools

import jax
import jax.numpy as jnp
import numpy as np
from jax import lax
from jax.experimental import pallas as pl
from jax.experimental.pallas import tpu as pltpu

HEAD_DIM = 64
MOBA_HEADS = 8
MOBA_BLOCK = 256
MOBA_TOPK = 3
SWA_Q_HEADS = 8
SWA_KV_HEADS = 2
SWA_WINDOW = 128
SWA_STEP_BLOCKS = 8
PROJ_STEP_BLOCKS = 2
N_REL_BUCKETS = 32
REL_MAX_DISTANCE = 2048
NORM_EPS = 1e-6
LOG2E = 1.4426950408889634
ATTN_SCALE = HEAD_DIM ** -0.5 * LOG2E

LANES = 128
HEAD_PAIRS = MOBA_HEADS // 2
VT_ROWS = 80
MOBA_NEAR = 7
MOBA_FAR_UNROLL = 4
assert MOBA_FAR_UNROLL + 2 < MOBA_NEAR
MOBA_LOOP_GROUPS = 4
VMEM_LIMIT = 56 * 1024 * 1024

BF16 = jnp.bfloat16
F32 = jnp.float32
NEG_INF = float("-inf")
M_INIT = -1e30


def _rel_bucket_np(dist):
    n = np.maximum(dist, 0)
    exact = N_REL_BUCKETS // 2
    nf = np.maximum(n, 1).astype(np.float32)
    large = exact + (np.log(nf / np.float32(exact)) / np.float32(np.log(REL_MAX_DISTANCE / exact))
                     * np.float32(N_REL_BUCKETS - exact)).astype(np.int32)
    large = np.minimum(large, N_REL_BUCKETS - 1)
    return np.where(n < exact, n, large)


def _nt_dot(a, b):
    return lax.dot_general(a, b, (((1,), (1,)), ((), ())), preferred_element_type=F32)


def _dot(a, b):
    return jnp.dot(a, b, preferred_element_type=F32)


def _rms_norm_rows(xf, g):
    ms = jnp.mean(xf * xf, axis=-1, keepdims=True)
    return xf * lax.rsqrt(ms + NORM_EPS) * g


def _head_norm(acc, g2):
    w = acc.shape[1]
    lane = lax.broadcasted_iota(jnp.int32, (1, LANES), 1)
    lo = lane < HEAD_DIM
    outs = []
    for b in range(w // LANES):
        xb = acc[:, b * LANES:(b + 1) * LANES]
        x2 = xb * xb
        s_lo = jnp.sum(jnp.where(lo, x2, 0.0), axis=-1, keepdims=True)
        s_hi = jnp.sum(jnp.where(lo, 0.0, x2), axis=-1, keepdims=True)
        r = jnp.where(lo, lax.rsqrt(s_lo / HEAD_DIM + NORM_EPS), lax.rsqrt(s_hi / HEAD_DIM + NORM_EPS))
        outs.append(xb * r * g2[:, b * LANES:(b + 1) * LANES])
    return outs[0] if len(outs) == 1 else jnp.concatenate(outs, axis=1)


def _v_transposed(v, n_heads):
    rows = v.shape[0]
    vt = v.T
    r = lax.broadcasted_iota(jnp.int32, (VT_ROWS - HEAD_DIM, rows), 0)
    aug = jnp.where(r == 0, 1.0, 0.0).astype(F32)
    parts = []
    for h in range(n_heads):
        parts.append(vt[h * HEAD_DIM:(h + 1) * HEAD_DIM])
        parts.append(aug)
    return jnp.concatenate(parts, axis=0).astype(BF16)


_W_QA, _W_KA, _W_VA, _W_QB, _W_KB, _W_VB, _W_GA, _W_GB = 512, 512, 512, 512, 256, 128, 1024, 1024
_PROJ_COLS = np.cumsum([0, _W_QA, _W_KA, _W_VA, _W_QB, SWA_KV_HEADS * HEAD_DIM, _W_VB, _W_GA, _W_GB]).tolist()


def _proj_kernel(x_ref, gmix_ref, w_ref, bgate_ref, gqa_ref, gka_ref, gqb_ref, gkb_ref,
                 qa_ref, ka_ref, kmean_ref, vat_ref, qb_ref, kb_ref, vbt_ref, ga_ref, gb_ref):
    h = _rms_norm_rows(x_ref[...], gmix_ref[...]).astype(BF16)
    c = _PROJ_COLS

    def seg(k):
        return _dot(h, w_ref[:, c[k]:c[k + 1]])

    qa_ref[...] = (_head_norm(seg(0), gqa_ref[...]) * ATTN_SCALE).astype(BF16)
    ka = _head_norm(seg(1), gka_ref[...])
    ka_ref[...] = ka.astype(BF16)
    for b in range(PROJ_STEP_BLOCKS):
        kmean_ref[b] = jnp.mean(ka[b * MOBA_BLOCK:(b + 1) * MOBA_BLOCK], axis=0, keepdims=True)
    vat_ref[...] = _v_transposed(seg(2), MOBA_HEADS)
    qb_ref[...] = (_head_norm(seg(3), gqb_ref[...]) * ATTN_SCALE).astype(BF16)
    kb = _head_norm(seg(4), gkb_ref[...])
    kb_swapped = pltpu.roll(kb, HEAD_DIM, axis=1)
    lo = lax.broadcasted_iota(jnp.int32, (1, LANES), 1) < HEAD_DIM
    kb_ref[...] = jnp.concatenate([jnp.where(lo, kb, kb_swapped), jnp.where(lo, kb_swapped, kb)], axis=1).astype(BF16)
    vbt_ref[...] = _v_transposed(seg(5), SWA_KV_HEADS)
    bg = bgate_ref[...]
    d = _W_GA
    ga_ref[...] = 1.0 / (1.0 + jnp.exp(-(seg(6) + bg[:, :d])))
    gb_ref[...] = 1.0 / (1.0 + jnp.exp(-(seg(7) + bg[:, d:])))


def _layer_spec(a, l, **kw):
    return pl.BlockSpec((None,) + a.shape[1:], lambda i: (l,) + (0,) * (a.ndim - 1), **kw)


def _proj_call(l, x, gmix, w, bgate, gqa, gka, gqb, gkb):
    s, d = x.shape
    tm = PROJ_STEP_BLOCKS * MOBA_BLOCK
    assert s % tm == 0
    nblk = s // MOBA_BLOCK
    full = lambda a: _layer_spec(a, l)
    rows = lambda width: pl.BlockSpec((tm, width), lambda i: (i, 0))
    cols = lambda height: pl.BlockSpec((height, tm), lambda i: (0, i))
    out_shape = (
        jax.ShapeDtypeStruct((s, _W_QA), BF16),
        jax.ShapeDtypeStruct((s, _W_KA), BF16),
        jax.ShapeDtypeStruct((nblk, 1, _W_KA), F32),
        jax.ShapeDtypeStruct((MOBA_HEADS * VT_ROWS, s), BF16),
        jax.ShapeDtypeStruct((s, _W_QB), BF16),
        jax.ShapeDtypeStruct((s, _W_KB), BF16),
        jax.ShapeDtypeStruct((SWA_KV_HEADS * VT_ROWS, s), BF16),
        jax.ShapeDtypeStruct((s, _W_GA), F32),
        jax.ShapeDtypeStruct((s, _W_GB), F32),
    )
    out_specs = (
        rows(_W_QA), rows(_W_KA), pl.BlockSpec((PROJ_STEP_BLOCKS, 1, _W_KA), lambda i: (i, 0, 0)),
        cols(MOBA_HEADS * VT_ROWS), rows(_W_QB), rows(_W_KB), cols(SWA_KV_HEADS * VT_ROWS),
        rows(_W_GA), rows(_W_GB),
    )
    return pl.pallas_call(
        _proj_kernel,
        grid=(s // tm,),
        in_specs=[rows(d), full(gmix), full(w), full(bgate), full(gqa), full(gka), full(gqb), full(gkb)],
        out_specs=out_specs,
        out_shape=out_shape,
        compiler_params=pltpu.CompilerParams(dimension_semantics=("arbitrary",), vmem_limit_bytes=VMEM_LIMIT),
        name="proj",
    )(x, gmix, w, bgate, gqa, gka, gqb, gkb)


def _moba_kernel(cfar_ref, q_ref, k_ref, vt_ref, kmean_ref, diag_ref, o_ref,
                 qq_scr, sel_scr, m_scr, acc_scr, sta_scr, stb_scr, bias_ref, *, nblk):
    p = pl.program_id(0)
    i = pl.program_id(1)
    blk = MOBA_BLOCK

    @pl.when(i == 0)
    def _():
        for d in range(MOBA_NEAR):
            for hh in range(2):
                skew = pltpu.roll(jnp.broadcast_to(diag_ref[d, hh:hh + 1, :], (blk, 2 * blk)), 0, 1,
                                  stride=1, stride_axis=0)
                bias_ref[d, :, hh * blk:(hh + 1) * blk] = skew[:, :blk]

    q = q_ref[...]
    lane = lax.broadcasted_iota(jnp.int32, (1, LANES), 1)
    km = kmean_ref[...]
    km_hi = km.astype(BF16)
    km_lo = (km - km_hi.astype(F32)).astype(BF16)
    n_iota = lax.broadcasted_iota(jnp.int32, (nblk, blk), 0)

    qms = []
    for hh in range(2):
        hmask = (lane < HEAD_DIM) if hh == 0 else (lane >= HEAD_DIM)
        qms.append(jnp.where(hmask, q, jnp.zeros_like(q)))
        qq_scr[hh * blk:(hh + 1) * blk, :] = qms[hh]

    gates = [_nt_dot(km_hi, qms[hh]) + _nt_dot(km_lo, qms[hh]) for hh in range(2)]

    def select_blocks():
        for hh in range(2):
            cols = slice(hh * blk, (hh + 1) * blk)
            g = jnp.where(n_iota < i, gates[hh], NEG_INF)
            sel = jnp.zeros((nblk, blk), F32)
            for _ in range(MOBA_TOPK):
                mx = jnp.max(g, axis=0, keepdims=True)
                idx = jnp.min(jnp.where(g == mx, n_iota, nblk), axis=0, keepdims=True)
                hit = n_iota == idx
                valid = jnp.where(mx > NEG_INF, 1.0, 0.0)
                sel = jnp.maximum(sel, jnp.where(hit, valid, 0.0))
                g = jnp.where(hit, NEG_INF, g)
            sel_scr[:, cols] = jnp.where(n_iota == i, 1.0, sel)
            acc_scr[hh] = jnp.zeros((VT_ROWS, blk), F32)
        m_scr[...] = jnp.full((1, 2 * blk), M_INIT, F32)

    col = lax.broadcasted_iota(jnp.int32, (1, 2 * blk), 1)
    cfar = jnp.where(col < blk, cfar_ref[2 * p], cfar_ref[2 * p + 1])
    n_far = jnp.maximum(i - (MOBA_NEAR - 1), 0)

    def scores(first, nb, st_ref):
        off = pl.multiple_of(first * blk, blk)
        st_ref[:nb * blk, :2 * blk] = _nt_dot(k_ref[pl.ds(off, nb * blk), :], qq_scr[...])

    def attend(first, nb, st_ref, near):
        ms, pvs = [], []
        for u in range(nb):
            j = first + u
            s_u = st_ref[u * blk:(u + 1) * blk, :2 * blk]
            if near:
                s_u = s_u + bias_ref[jnp.clip(i - j, 0, MOBA_NEAR - 1)]
                on = sel_scr[pl.ds(j, 1), :] > 0.5
                m_u = jnp.max(s_u, axis=0, keepdims=True)
                ms.append(jnp.where(on, m_u, NEG_INF))
            else:
                on = sel_scr[pl.ds(j, 1), :] > jnp.where(j < n_far, 0.5, 2.0)
                m_u = jnp.max(s_u, axis=0, keepdims=True)
                ms.append(jnp.where(on, m_u + cfar, NEG_INF))
            pt = jnp.exp2(s_u - m_u).astype(BF16)
            ko = pl.multiple_of(j * blk, blk)
            pvs.append([_dot(vt_ref[hh * VT_ROWS:(hh + 1) * VT_ROWS, pl.ds(ko, blk)], pt[:, hh * blk:(hh + 1) * blk])
                        for hh in range(2)])
        m_old = m_scr[...]
        m_new = m_old
        for m_u in ms:
            m_new = jnp.maximum(m_new, m_u)
        alpha = jnp.exp2(m_old - m_new)
        ws = [jnp.exp2(m_u - m_new) for m_u in ms]
        for hh in range(2):
            cols = slice(hh * blk, (hh + 1) * blk)
            acc = acc_scr[hh] * alpha[:, cols]
            for w_u, pv in zip(ws, pvs):
                acc = acc + pv[hh] * w_u[:, cols]
            acc_scr[hh] = acc
        m_scr[...] = m_new

    fu = MOBA_FAR_UNROLL
    n_groups = (n_far + fu - 1) // fu
    near0 = jnp.maximum(i - (MOBA_NEAR - 1), 0)
    bufs = (sta_scr, stb_scr)

    def item_start(g):
        return jnp.where(g < n_groups, g * fu, near0)

    def run_items(items, then_first=None):
        for k, (first, nb, near) in enumerate(items):
            if k + 1 < len(items):
                scores(items[k + 1][0], items[k + 1][1], bufs[(k + 1) % 2])
            elif then_first is not None:
                scores(then_first, fu, bufs[(k + 1) % 2])
            attend(first, nb, bufs[k % 2], near)

    near_items = [(near0, fu, True), (near0 + fu, 2, True), (near0 + fu + 2, MOBA_NEAR - fu - 2, True)]

    scores(item_start(0), fu, sta_scr)
    select_blocks()

    unroll = MOBA_LOOP_GROUPS

    def loop_body(t, carry):
        g0 = t * unroll
        run_items([((g0 + k) * fu, fu, False) for k in range(unroll)], then_first=item_start(g0 + unroll))
        return carry

    lax.fori_loop(0, n_groups // unroll, loop_body, 0)

    g0 = n_groups // unroll * unroll
    for rem in range(unroll):

        @pl.when(n_groups - g0 == rem)
        def _():
            run_items([((g0 + k) * fu, fu, False) for k in range(rem)] + near_items)

    outs = []
    for hh in range(2):
        a = acc_scr[hh]
        outs.append(a[:HEAD_DIM] / a[HEAD_DIM:HEAD_DIM + 1])
    o_ref[...] = jnp.concatenate(outs, axis=0).T.astype(BF16)


def _moba_call(cfar, qa, ka, vat, kmean, bias_t):
    s = qa.shape[0]
    blk = MOBA_BLOCK
    nblk = s // blk
    grid_spec = pltpu.PrefetchScalarGridSpec(
        num_scalar_prefetch=1,
        grid=(HEAD_PAIRS, nblk),
        in_specs=[
            pl.BlockSpec((blk, LANES), lambda p, i, c: (i, p)),
            pl.BlockSpec((s, LANES), lambda p, i, c: (0, p)),
            pl.BlockSpec((2 * VT_ROWS, s), lambda p, i, c: (p, 0)),
            pl.BlockSpec((nblk, LANES), lambda p, i, c: (0, p)),
            pl.BlockSpec((MOBA_NEAR, None, 2, 2 * blk), lambda p, i, c: (0, p, 0, 0)),
        ],
        out_specs=pl.BlockSpec((blk, LANES), lambda p, i, c: (i, p)),
        scratch_shapes=[
            pltpu.VMEM((2 * blk, LANES), BF16),
            pltpu.VMEM((nblk, 2 * blk), F32),
            pltpu.VMEM((1, 2 * blk), F32),
            pltpu.VMEM((2, VT_ROWS, blk), F32),
            pltpu.VMEM((MOBA_FAR_UNROLL * blk, 2 * blk + LANES), F32),
            pltpu.VMEM((MOBA_FAR_UNROLL * blk, 2 * blk + LANES), F32),
            pltpu.VMEM((MOBA_NEAR, blk, 2 * blk), F32),
        ],
    )
    return pl.pallas_call(
        functools.partial(_moba_kernel, nblk=nblk),
        grid_spec=grid_spec,
        out_shape=jax.ShapeDtypeStruct((s, MOBA_HEADS * HEAD_DIM), BF16),
        compiler_params=pltpu.CompilerParams(dimension_semantics=("arbitrary", "arbitrary"),
                                             vmem_limit_bytes=VMEM_LIMIT),
        name="moba",
    )(cfar, qa, ka, vat, kmean, bias_t)


def _swa_kernel(sink_ref, q_ref, kprev_ref, kcur_ref, vprev_ref, vcur_ref, bias_ref, o_ref, st_scr):
    b = pl.program_id(0)
    w = SWA_WINDOW
    kall = jnp.concatenate([kprev_ref[...], kcur_ref[...]], axis=0)
    vall = jnp.concatenate([vprev_ref[...], vcur_ref[...]], axis=1)
    row = lax.broadcasted_iota(jnp.int32, (2 * w, 1), 0)
    keep = row >= jnp.where(b > 0, 0, w)
    lane = lax.broadcasted_iota(jnp.int32, (1, LANES), 1)
    col = lax.broadcasted_iota(jnp.int32, (1, 2 * w), 1)
    group = SWA_Q_HEADS // SWA_KV_HEADS
    chains = [(sb, pr) for sb in range(SWA_STEP_BLOCKS) for pr in range(SWA_Q_HEADS // 2)]
    for c, (sb, pr) in enumerate(chains):
        g = (2 * pr) // group
        qp = q_ref[sb * w:(sb + 1) * w, pr * LANES:(pr + 1) * LANES]
        zero = jnp.zeros_like(qp)
        qq = jnp.concatenate([jnp.where(lane < HEAD_DIM, qp, zero),
                              jnp.where(lane >= HEAD_DIM, qp, zero)], axis=0)
        kd = kall[sb * w:(sb + 2) * w, g * LANES:(g + 1) * LANES]
        st_scr[c, :, :2 * w] = _nt_dot(kd, qq)
    for c, (sb, pr) in enumerate(chains):
        g = (2 * pr) // group
        st = st_scr[c, :, :2 * w] + bias_ref[pr]
        if sb == 0:
            st = jnp.where(keep, st, NEG_INF)
        sink = jnp.where(col < w, sink_ref[2 * pr], sink_ref[2 * pr + 1])
        m = jnp.maximum(jnp.max(st, axis=0, keepdims=True), sink)
        pt = jnp.exp2(st - m).astype(BF16)
        acc = _dot(vall[g * VT_ROWS:(g + 1) * VT_ROWS, sb * w:(sb + 2) * w], pt)
        denom = acc[HEAD_DIM:HEAD_DIM + 1] + jnp.exp2(sink - m)
        o = acc[:HEAD_DIM] / denom
        oo = jnp.concatenate([o[:, :w], o[:, w:]], axis=0)
        o_ref[sb * w:(sb + 1) * w, pr * LANES:(pr + 1) * LANES] = oo.T.astype(BF16)


def _swa_call(sinks, qb, kbd, vbt, bias_t):
    s = qb.shape[0]
    w = SWA_WINDOW
    n = SWA_STEP_BLOCKS
    assert s % (n * w) == 0
    prev_blk = lambda i: jnp.maximum(n * i - 1, 0)
    grid_spec = pltpu.PrefetchScalarGridSpec(
        num_scalar_prefetch=1,
        grid=(s // (n * w),),
        in_specs=[
            pl.BlockSpec((n * w, _W_QB), lambda i, c: (i, 0)),
            pl.BlockSpec((w, _W_KB), lambda i, c: (prev_blk(i), 0)),
            pl.BlockSpec((n * w, _W_KB), lambda i, c: (i, 0)),
            pl.BlockSpec((SWA_KV_HEADS * VT_ROWS, w), lambda i, c: (0, prev_blk(i))),
            pl.BlockSpec((SWA_KV_HEADS * VT_ROWS, n * w), lambda i, c: (0, i)),
            pl.BlockSpec(bias_t.shape, lambda i, c: (0, 0, 0)),
        ],
        out_specs=pl.BlockSpec((n * w, _W_QB), lambda i, c: (i, 0)),
        scratch_shapes=[pltpu.VMEM((n * SWA_Q_HEADS // 2, 2 * w, 2 * w + LANES), F32)],
    )
    return pl.pallas_call(
        _swa_kernel,
        grid_spec=grid_spec,
        out_shape=jax.ShapeDtypeStruct((s, _W_QB), BF16),
        compiler_params=pltpu.CompilerParams(dimension_semantics=("arbitrary",), vmem_limit_bytes=VMEM_LIMIT),
        name="swa",
    )(sinks, qb, kbd, kbd, vbt, vbt, bias_t)


def _post_kernel(x_ref, ya_ref, yb_ref, ga_ref, gb_ref, wa_ref, wb_ref, wo_ref, gffn_ref, w1_ref, w2_ref, o_ref,
                 *, tf):
    merged = ga_ref[...] * _dot(ya_ref[...], wa_ref[...]) + gb_ref[...] * _dot(yb_ref[...], wb_ref[...])
    xn = x_ref[...] + _dot(merged.astype(BF16), wo_ref[...])
    h = _rms_norm_rows(xn, gffn_ref[...]).astype(BF16)
    acc = xn
    for c in range(w1_ref.shape[1] // tf):
        u = jnp.maximum(_dot(h, w1_ref[:, c * tf:(c + 1) * tf]), 0.0)
        acc = acc + _dot((u * u).astype(BF16), w2_ref[c * tf:(c + 1) * tf, :])
    o_ref[...] = acc


def _post_call(l, x, ya, yb, ga, gb, wa, wb, wo, gffn, w1, w2, tm=512, tf=1024):
    s, d = x.shape
    full = lambda a: _layer_spec(a, l, pipeline_mode=pl.Buffered(1))
    rows = lambda a: pl.BlockSpec((tm, a.shape[1]), lambda i: (i, 0))
    return pl.pallas_call(
        functools.partial(_post_kernel, tf=tf),
        grid=(s // tm,),
        in_specs=[rows(x), rows(ya), rows(yb), rows(ga), rows(gb),
                  full(wa), full(wb), full(wo), full(gffn), full(w1), full(w2)],
        out_specs=pl.BlockSpec((tm, d), lambda i: (i, 0)),
        out_shape=jax.ShapeDtypeStruct((s, d), F32),
        compiler_params=pltpu.CompilerParams(dimension_semantics=("arbitrary",), vmem_limit_bytes=VMEM_LIMIT),
        name="post",
    )(x, ya, yb, ga, gb, wa, wb, wo, gffn, w1, w2)


def _bias_by_distance(table, lo, hi, keep):
    dist = np.arange(lo, hi)
    vals = jnp.take(table, jnp.asarray(_rel_bucket_np(dist).astype(np.int32)), axis=0).T
    return jnp.where(jnp.asarray(keep(dist))[None, :], vals, NEG_INF)


def _toeplitz(w, n):
    lead = w.shape[:-1]
    w_pad = jnp.concatenate([w, jnp.zeros(lead + (1,), w.dtype)], axis=-1)
    skew = jnp.tile(w_pad, (1,) * len(lead) + (n,))[..., :n * (2 * n - 1)].reshape(lead + (n, 2 * n - 1))
    return skew[..., n - 1:]


def _bias_tables(rel_bias):
    blk, w = MOBA_BLOCK, SWA_WINDOW
    vec = _bias_by_distance(rel_bias[:, :MOBA_HEADS], 1 - blk, MOBA_NEAR * blk, lambda dist: dist >= 0)
    vec = jnp.concatenate([vec, jnp.zeros((MOBA_HEADS, 1), F32)], axis=1)
    moba = jnp.stack([jnp.roll(vec[:, d * blk:(d + 2) * blk], 1 - blk, axis=1) for d in range(MOBA_NEAR)])
    moba = moba.reshape(MOBA_NEAR, HEAD_PAIRS, 2, 2 * blk)
    cfar = rel_bias[N_REL_BUCKETS - 1, :MOBA_HEADS]

    vec = _bias_by_distance(rel_bias[:, MOBA_HEADS:], 1 - w, 3 * w, lambda dist: (dist >= 0) & (dist < w))
    swa = _toeplitz(vec, 2 * w)[:, :, :w]
    swa = swa.reshape(SWA_Q_HEADS // 2, 2, 2 * w, w).transpose(0, 2, 1, 3).reshape(SWA_Q_HEADS // 2, 2 * w, 2 * w)
    return (moba * LOG2E).astype(F32), (cfar * LOG2E).astype(F32), (swa * LOG2E).astype(F32)


def kernel(x, rel_bias, g_mix, w_in, b_gate, q_norm_a, k_norm_a, q_norm_b, k_norm_b, sinks,
           w_branch_a, w_branch_b, w_out, g_ffn, w_ff1, w_ff2):
    b, s, d = x.shape
    assert b == 1 and s % MOBA_BLOCK == 0 and s // MOBA_BLOCK >= MOBA_NEAR
    depth = w_in.shape[0]
    bias_moba, cfar, bias_swa = _bias_tables(rel_bias)
    row = lambda a: a[:, None, :]
    gains = lambda g, width: row(jnp.tile(g, (1, width // HEAD_DIM)))
    w_in_b, wa_b, wb_b, wo_b, w1_b, w2_b = (a.astype(BF16) for a in (w_in, w_branch_a, w_branch_b, w_out, w_ff1, w_ff2))
    gqa, gka = gains(q_norm_a, _W_QA), gains(k_norm_a, _W_KA)
    gqb, gkb = gains(q_norm_b, _W_QB), gains(k_norm_b, SWA_KV_HEADS * HEAD_DIM)
    sinks2 = sinks * LOG2E
    xs = x[0]
    for l in range(depth):
        qa, ka, kmean, vat, qb, kbd, vbt, ga, gb = _proj_call(l, xs, row(g_mix), w_in_b, row(b_gate), gqa, gka, gqb, gkb)
        ya = _moba_call(cfar, qa, ka, vat, kmean.reshape(kmean.shape[0], kmean.shape[2]), bias_moba)
        yb = _swa_call(sinks2[l], qb, kbd, vbt, bias_swa)
        xs = _post_call(l, xs, ya, yb, ga, gb, wa_b, wb_b, wo_b, row(g_ffn), w1_b, w2_b)
    return xs[None]
```

```python
import functools

import jax
import jax.numpy as jnp
import numpy as np
from jax import lax
from jax.experimental import pallas as pl
from jax.experimental.pallas import tpu as pltpu

HEAD_DIM = 64
MOBA_HEADS = 8
MOBA_BLOCK = 256
MOBA_TOPK = 3
SWA_Q_HEADS = 8
SWA_KV_HEADS = 2
SWA_WINDOW = 128
SWA_STEP_BLOCKS = 8
SWA_SCORES_AHEAD = 4
PROJ_STEP_BLOCKS = 2
N_REL_BUCKETS = 32
REL_MAX_DISTANCE = 2048
NORM_EPS = 1e-6
LOG2E = 1.4426950408889634
ATTN_SCALE = HEAD_DIM ** -0.5 * LOG2E

LANES = 128
HEAD_PAIRS = MOBA_HEADS // 2
VT_ROWS = 80
MOBA_NEAR = 7
MOBA_FAR_UNROLL = 4
assert MOBA_FAR_UNROLL + 2 < MOBA_NEAR
MOBA_LOOP_GROUPS = 4
VMEM_LIMIT = 56 * 1024 * 1024

BF16 = jnp.bfloat16
F32 = jnp.float32
NEG_INF = float("-inf")
M_INIT = -1e30


def _rel_bucket_np(dist):
    n = np.maximum(dist, 0)
    exact = N_REL_BUCKETS // 2
    nf = np.maximum(n, 1).astype(np.float32)
    large = exact + (np.log(nf / np.float32(exact)) / np.float32(np.log(REL_MAX_DISTANCE / exact))
                     * np.float32(N_REL_BUCKETS - exact)).astype(np.int32)
    large = np.minimum(large, N_REL_BUCKETS - 1)
    return np.where(n < exact, n, large)


def _nt_dot(a, b):
    return lax.dot_general(a, b, (((1,), (1,)), ((), ())), preferred_element_type=F32)


def _dot(a, b):
    return jnp.dot(a, b, preferred_element_type=F32)


def _rms_norm_rows(xf, g):
    ms = jnp.mean(xf * xf, axis=-1, keepdims=True)
    return xf * lax.rsqrt(ms + NORM_EPS) * g


def _head_norm(acc, g2):
    w = acc.shape[1]
    lane = lax.broadcasted_iota(jnp.int32, (1, LANES), 1)
    lo = lane < HEAD_DIM
    outs = []
    for b in range(w // LANES):
        xb = acc[:, b * LANES:(b + 1) * LANES]
        x2 = xb * xb
        s_lo = jnp.sum(jnp.where(lo, x2, 0.0), axis=-1, keepdims=True)
        s_hi = jnp.sum(jnp.where(lo, 0.0, x2), axis=-1, keepdims=True)
        r = jnp.where(lo, lax.rsqrt(s_lo / HEAD_DIM + NORM_EPS), lax.rsqrt(s_hi / HEAD_DIM + NORM_EPS))
        outs.append(xb * r * g2[:, b * LANES:(b + 1) * LANES])
    return outs[0] if len(outs) == 1 else jnp.concatenate(outs, axis=1)


def _v_transposed(v, n_heads):
    rows = v.shape[0]
    vt = v.T
    r = lax.broadcasted_iota(jnp.int32, (VT_ROWS - HEAD_DIM, rows), 0)
    aug = jnp.where(r == 0, 1.0, 0.0).astype(F32)
    parts = []
    for h in range(n_heads):
        parts.append(vt[h * HEAD_DIM:(h + 1) * HEAD_DIM])
        parts.append(aug)
    return jnp.concatenate(parts, axis=0).astype(BF16)


_W_QA, _W_KA, _W_VA, _W_QB, _W_KB, _W_VB, _W_GA, _W_GB = 512, 512, 512, 512, 256, 128, 1024, 1024
_PROJ_COLS = np.cumsum([0, _W_QA, _W_KA, _W_VA, _W_QB, SWA_KV_HEADS * HEAD_DIM, _W_VB, _W_GA, _W_GB]).tolist()


def _proj_kernel(x_ref, gmix_ref, w_ref, bgate_ref, gqa_ref, gka_ref, gqb_ref, gkb_ref,
                 qa_ref, ka_ref, kmean_ref, vat_ref, qb_ref, kb_ref, vbt_ref, ga_ref, gb_ref):
    h = _rms_norm_rows(x_ref[...], gmix_ref[...]).astype(BF16)
    c = _PROJ_COLS

    def seg(k):
        return _dot(h, w_ref[:, c[k]:c[k + 1]])

    qa_ref[...] = (_head_norm(seg(0), gqa_ref[...]) * ATTN_SCALE).astype(BF16)
    ka = _head_norm(seg(1), gka_ref[...])
    ka_ref[...] = ka.astype(BF16)
    for b in range(PROJ_STEP_BLOCKS):
        kmean_ref[b] = jnp.mean(ka[b * MOBA_BLOCK:(b + 1) * MOBA_BLOCK], axis=0, keepdims=True)
    vat_ref[...] = _v_transposed(seg(2), MOBA_HEADS)
    qb_ref[...] = (_head_norm(seg(3), gqb_ref[...]) * ATTN_SCALE).astype(BF16)
    kb = _head_norm(seg(4), gkb_ref[...])
    kb_swapped = pltpu.roll(kb, HEAD_DIM, axis=1)
    lo = lax.broadcasted_iota(jnp.int32, (1, LANES), 1) < HEAD_DIM
    kb_ref[...] = jnp.concatenate([jnp.where(lo, kb, kb_swapped), jnp.where(lo, kb_swapped, kb)], axis=1).astype(BF16)
    vbt_ref[...] = _v_transposed(seg(5), SWA_KV_HEADS)
    bg = bgate_ref[...]
    d = _W_GA
    ga_ref[...] = 1.0 / (1.0 + jnp.exp(-(seg(6) + bg[:, :d])))
    gb_ref[...] = 1.0 / (1.0 + jnp.exp(-(seg(7) + bg[:, d:])))


def _layer_spec(a, l, **kw):
    return pl.BlockSpec((None,) + a.shape[1:], lambda i: (l,) + (0,) * (a.ndim - 1), **kw)


def _proj_call(l, x, gmix, w, bgate, gqa, gka, gqb, gkb):
    s, d = x.shape
    tm = PROJ_STEP_BLOCKS * MOBA_BLOCK
    assert s % tm == 0
    nblk = s // MOBA_BLOCK
    full = lambda a: _layer_spec(a, l)
    rows = lambda width: pl.BlockSpec((tm, width), lambda i: (i, 0))
    cols = lambda height: pl.BlockSpec((height, tm), lambda i: (0, i))
    out_shape = (
        jax.ShapeDtypeStruct((s, _W_QA), BF16),
        jax.ShapeDtypeStruct((s, _W_KA), BF16),
        jax.ShapeDtypeStruct((nblk, 1, _W_KA), F32),
        jax.ShapeDtypeStruct((MOBA_HEADS * VT_ROWS, s), BF16),
        jax.ShapeDtypeStruct((s, _W_QB), BF16),
        jax.ShapeDtypeStruct((s, _W_KB), BF16),
        jax.ShapeDtypeStruct((SWA_KV_HEADS * VT_ROWS, s), BF16),
        jax.ShapeDtypeStruct((s, _W_GA), F32),
        jax.ShapeDtypeStruct((s, _W_GB), F32),
    )
    out_specs = (
        rows(_W_QA), rows(_W_KA), pl.BlockSpec((PROJ_STEP_BLOCKS, 1, _W_KA), lambda i: (i, 0, 0)),
        cols(MOBA_HEADS * VT_ROWS), rows(_W_QB), rows(_W_KB), cols(SWA_KV_HEADS * VT_ROWS),
        rows(_W_GA), rows(_W_GB),
    )
    return pl.pallas_call(
        _proj_kernel,
        grid=(s // tm,),
        in_specs=[rows(d), full(gmix), full(w), full(bgate), full(gqa), full(gka), full(gqb), full(gkb)],
        out_specs=out_specs,
        out_shape=out_shape,
        compiler_params=pltpu.CompilerParams(dimension_semantics=("arbitrary",), vmem_limit_bytes=VMEM_LIMIT),
        name="proj",
    )(x, gmix, w, bgate, gqa, gka, gqb, gkb)


def _moba_kernel(cfar_ref, q_ref, k_ref, vt_ref, kmean_ref, diag_ref, o_ref,
                 qq_scr, sel_scr, m_scr, acc_scr, sta_scr, stb_scr, bias_ref, *, nblk):
    p = pl.program_id(0)
    i = pl.program_id(1)
    blk = MOBA_BLOCK

    @pl.when(i == 0)
    def _():
        for d in range(MOBA_NEAR):
            for hh in range(2):
                skew = pltpu.roll(jnp.broadcast_to(diag_ref[d, hh:hh + 1, :], (blk, 2 * blk)), 0, 1,
                                  stride=1, stride_axis=0)
                bias_ref[d, :, hh * blk:(hh + 1) * blk] = skew[:, :blk]

    q = q_ref[...]
    lane = lax.broadcasted_iota(jnp.int32, (1, LANES), 1)
    km = kmean_ref[...]
    km_hi = km.astype(BF16)
    km_lo = (km - km_hi.astype(F32)).astype(BF16)
    n_iota = lax.broadcasted_iota(jnp.int32, (nblk, 2 * blk), 0)

    qq = jnp.concatenate([jnp.where(lane < HEAD_DIM, q, jnp.zeros_like(q)),
                          jnp.where(lane >= HEAD_DIM, q, jnp.zeros_like(q))], axis=0)
    qq_scr[...] = qq

    gate2 = _nt_dot(jnp.concatenate([km_hi, km_lo], axis=0), qq)
    gates = gate2[:nblk] + gate2[nblk:]

    def select_blocks():
        g = jnp.where(n_iota < i, gates, NEG_INF)
        sel = jnp.zeros((nblk, 2 * blk), F32)
        for _ in range(MOBA_TOPK):
            mx = jnp.max(g, axis=0, keepdims=True)
            idx = jnp.min(jnp.where(g == mx, n_iota, nblk), axis=0, keepdims=True)
            hit = n_iota == idx
            valid = jnp.where(mx > NEG_INF, 1.0, 0.0)
            sel = jnp.maximum(sel, jnp.where(hit, valid, 0.0))
            g = jnp.where(hit, NEG_INF, g)
        sel_scr[...] = jnp.where(n_iota == i, 1.0, sel)
        acc_scr[...] = jnp.zeros((2, VT_ROWS, blk), F32)
        m_scr[...] = jnp.full((1, 2 * blk), M_INIT, F32)

    col = lax.broadcasted_iota(jnp.int32, (1, 2 * blk), 1)
    cfar = jnp.where(col < blk, cfar_ref[2 * p], cfar_ref[2 * p + 1])
    n_far = jnp.maximum(i - (MOBA_NEAR - 1), 0)

    def scores(first, nb, st_ref):
        off = pl.multiple_of(first * blk, blk)
        st_ref[:nb * blk, :2 * blk] = _nt_dot(k_ref[pl.ds(off, nb * blk), :], qq_scr[...])

    def attend(first, nb, st_ref, near):
        ms, pvs = [], []
        for u in range(nb):
            j = first + u
            s_u = st_ref[u * blk:(u + 1) * blk, :2 * blk]
            if near:
                s_u = s_u + bias_ref[jnp.clip(i - j, 0, MOBA_NEAR - 1)]
                on = sel_scr[pl.ds(j, 1), :] > 0.5
                m_u = jnp.max(s_u, axis=0, keepdims=True)
                ms.append(jnp.where(on, m_u, NEG_INF))
            else:
                on = sel_scr[pl.ds(j, 1), :] > jnp.where(j < n_far, 0.5, 2.0)
                m_u = jnp.max(s_u, axis=0, keepdims=True)
                ms.append(jnp.where(on, m_u + cfar, NEG_INF))
            pt = jnp.exp2(s_u - m_u).astype(BF16)
            ko = pl.multiple_of(j * blk, blk)
            pvs.append([_dot(vt_ref[hh * VT_ROWS:(hh + 1) * VT_ROWS, pl.ds(ko, blk)], pt[:, hh * blk:(hh + 1) * blk])
                        for hh in range(2)])
        m_old = m_scr[...]
        m_new = m_old
        for m_u in ms:
            m_new = jnp.maximum(m_new, m_u)
        alpha = jnp.exp2(m_old - m_new)
        ws = [jnp.exp2(m_u - m_new) for m_u in ms]
        for hh in range(2):
            cols = slice(hh * blk, (hh + 1) * blk)
            acc = acc_scr[hh] * alpha[:, cols]
            for w_u, pv in zip(ws, pvs):
                acc = acc + pv[hh] * w_u[:, cols]
            acc_scr[hh] = acc
        m_scr[...] = m_new

    fu = MOBA_FAR_UNROLL
    n_groups = (n_far + fu - 1) // fu
    near0 = jnp.maximum(i - (MOBA_NEAR - 1), 0)
    bufs = (sta_scr, stb_scr)

    def item_start(g):
        return jnp.where(g < n_groups, g * fu, near0)

    def run_items(items, then_first=None):
        for k, (first, nb, near) in enumerate(items):
            if k + 1 < len(items):
                scores(items[k + 1][0], items[k + 1][1], bufs[(k + 1) % 2])
            elif then_first is not None:
                scores(then_first, fu, bufs[(k + 1) % 2])
            attend(first, nb, bufs[k % 2], near)

    near_items = [(near0, fu, True), (near0 + fu, 2, True), (near0 + fu + 2, MOBA_NEAR - fu - 2, True)]

    scores(item_start(0), fu, sta_scr)
    select_blocks()

    unroll = MOBA_LOOP_GROUPS

    def loop_body(t, carry):
        g0 = t * unroll
        run_items([((g0 + k) * fu, fu, False) for k in range(unroll)], then_first=item_start(g0 + unroll))
        return carry

    lax.fori_loop(0, n_groups // unroll, loop_body, 0)

    g0 = n_groups // unroll * unroll
    for rem in range(unroll):

        @pl.when(n_groups - g0 == rem)
        def _():
            run_items([((g0 + k) * fu, fu, False) for k in range(rem)] + near_items)

    outs = []
    for hh in range(2):
        a = acc_scr[hh]
        outs.append(a[:HEAD_DIM] / a[HEAD_DIM:HEAD_DIM + 1])
    o_ref[...] = jnp.concatenate(outs, axis=0).T.astype(BF16)


def _moba_call(cfar, qa, ka, vat, kmean, bias_t):
    s = qa.shape[0]
    blk = MOBA_BLOCK
    nblk = s // blk
    grid_spec = pltpu.PrefetchScalarGridSpec(
        num_scalar_prefetch=1,
        grid=(HEAD_PAIRS, nblk),
        in_specs=[
            pl.BlockSpec((blk, LANES), lambda p, i, c: (i, p)),
            pl.BlockSpec((s, LANES), lambda p, i, c: (0, p)),
            pl.BlockSpec((2 * VT_ROWS, s), lambda p, i, c: (p, 0)),
            pl.BlockSpec((nblk, LANES), lambda p, i, c: (0, p)),
            pl.BlockSpec((MOBA_NEAR, None, 2, 2 * blk), lambda p, i, c: (0, p, 0, 0)),
        ],
        out_specs=pl.BlockSpec((blk, LANES), lambda p, i, c: (i, p)),
        scratch_shapes=[
            pltpu.VMEM((2 * blk, LANES), BF16),
            pltpu.VMEM((nblk, 2 * blk), F32),
            pltpu.VMEM((1, 2 * blk), F32),
            pltpu.VMEM((2, VT_ROWS, blk), F32),
            pltpu.VMEM((MOBA_FAR_UNROLL * blk, 2 * blk + LANES), F32),
            pltpu.VMEM((MOBA_FAR_UNROLL * blk, 2 * blk + LANES), F32),
            pltpu.VMEM((MOBA_NEAR, blk, 2 * blk), F32),
        ],
    )
    return pl.pallas_call(
        functools.partial(_moba_kernel, nblk=nblk),
        grid_spec=grid_spec,
        out_shape=jax.ShapeDtypeStruct((s, MOBA_HEADS * HEAD_DIM), BF16),
        compiler_params=pltpu.CompilerParams(dimension_semantics=("arbitrary", "arbitrary"),
                                             vmem_limit_bytes=VMEM_LIMIT),
        name="moba",
    )(cfar, qa, ka, vat, kmean, bias_t)


def _swa_kernel(sink_ref, q_ref, kprev_ref, kcur_ref, vprev_ref, vcur_ref, bias_ref, o_ref, st_scr):
    b = pl.program_id(0)
    w = SWA_WINDOW
    kall = jnp.concatenate([kprev_ref[...], kcur_ref[...]], axis=0)
    vall = jnp.concatenate([vprev_ref[...], vcur_ref[...]], axis=1)
    row = lax.broadcasted_iota(jnp.int32, (2 * w, 1), 0)
    keep = row >= jnp.where(b > 0, 0, w)
    lane = lax.broadcasted_iota(jnp.int32, (1, LANES), 1)
    col = lax.broadcasted_iota(jnp.int32, (1, 2 * w), 1)
    group = SWA_Q_HEADS // SWA_KV_HEADS
    chains = [(sb, pr) for sb in range(SWA_STEP_BLOCKS) for pr in range(SWA_Q_HEADS // 2)]
    def scores(c):
        sb, pr = chains[c]
        g = (2 * pr) // group
        qp = q_ref[sb * w:(sb + 1) * w, pr * LANES:(pr + 1) * LANES]
        zero = jnp.zeros_like(qp)
        qq = jnp.concatenate([jnp.where(lane < HEAD_DIM, qp, zero),
                              jnp.where(lane >= HEAD_DIM, qp, zero)], axis=0)
        kd = kall[sb * w:(sb + 2) * w, g * LANES:(g + 1) * LANES]
        st_scr[c, :, :2 * w] = _nt_dot(kd, qq)

    ahead = SWA_SCORES_AHEAD
    for c in range(min(ahead, len(chains))):
        scores(c)
    for c, (sb, pr) in enumerate(chains):
        if c + ahead < len(chains):
            scores(c + ahead)
        g = (2 * pr) // group
        st = st_scr[c, :, :2 * w] + bias_ref[pr]
        if sb == 0:
            st = jnp.where(keep, st, NEG_INF)
        sink = jnp.where(col < w, sink_ref[2 * pr], sink_ref[2 * pr + 1])
        m = jnp.maximum(jnp.max(st, axis=0, keepdims=True), sink)
        pt = jnp.exp2(st - m).astype(BF16)
        acc = _dot(vall[g * VT_ROWS:(g + 1) * VT_ROWS, sb * w:(sb + 2) * w], pt)
        denom = acc[HEAD_DIM:HEAD_DIM + 1] + jnp.exp2(sink - m)
        o = acc[:HEAD_DIM] / denom
        oo = jnp.concatenate([o[:, :w], o[:, w:]], axis=0)
        o_ref[sb * w:(sb + 1) * w, pr * LANES:(pr + 1) * LANES] = oo.T.astype(BF16)


def _swa_call(sinks, qb, kbd, vbt, bias_t):
    s = qb.shape[0]
    w = SWA_WINDOW
    n = SWA_STEP_BLOCKS
    assert s % (n * w) == 0
    prev_blk = lambda i: jnp.maximum(n * i - 1, 0)
    grid_spec = pltpu.PrefetchScalarGridSpec(
        num_scalar_prefetch=1,
        grid=(s // (n * w),),
        in_specs=[
            pl.BlockSpec((n * w, _W_QB), lambda i, c: (i, 0)),
            pl.BlockSpec((w, _W_KB), lambda i, c: (prev_blk(i), 0)),
            pl.BlockSpec((n * w, _W_KB), lambda i, c: (i, 0)),
            pl.BlockSpec((SWA_KV_HEADS * VT_ROWS, w), lambda i, c: (0, prev_blk(i))),
            pl.BlockSpec((SWA_KV_HEADS * VT_ROWS, n * w), lambda i, c: (0, i)),
            pl.BlockSpec(bias_t.shape, lambda i, c: (0, 0, 0)),
        ],
        out_specs=pl.BlockSpec((n * w, _W_QB), lambda i, c: (i, 0)),
        scratch_shapes=[pltpu.VMEM((n * SWA_Q_HEADS // 2, 2 * w, 2 * w + LANES), F32)],
    )
    return pl.pallas_call(
        _swa_kernel,
        grid_spec=grid_spec,
        out_shape=jax.ShapeDtypeStruct((s, _W_QB), BF16),
        compiler_params=pltpu.CompilerParams(dimension_semantics=("arbitrary",), vmem_limit_bytes=VMEM_LIMIT),
        name="swa",
    )(sinks, qb, kbd, kbd, vbt, vbt, bias_t)


def _post_kernel(x_ref, ya_ref, yb_ref, ga_ref, gb_ref, wa_ref, wb_ref, wo_ref, gffn_ref, w1_ref, w2_ref, o_ref,
                 *, tf):
    merged = ga_ref[...] * _dot(ya_ref[...], wa_ref[...]) + gb_ref[...] * _dot(yb_ref[...], wb_ref[...])
    xn = x_ref[...] + _dot(merged.astype(BF16), wo_ref[...])
    h = _rms_norm_rows(xn, gffn_ref[...]).astype(BF16)
    acc = xn
    for c in range(w1_ref.shape[1] // tf):
        u = jnp.maximum(_dot(h, w1_ref[:, c * tf:(c + 1) * tf]), 0.0)
        acc = acc + _dot((u * u).astype(BF16), w2_ref[c * tf:(c + 1) * tf, :])
    o_ref[...] = acc


def _post_call(l, x, ya, yb, ga, gb, wa, wb, wo, gffn, w1, w2, tm=512, tf=1024):
    s, d = x.shape
    full = lambda a: _layer_spec(a, l, pipeline_mode=pl.Buffered(1))
    rows = lambda a: pl.BlockSpec((tm, a.shape[1]), lambda i: (i, 0))
    return pl.pallas_call(
        functools.partial(_post_kernel, tf=tf),
        grid=(s // tm,),
        in_specs=[rows(x), rows(ya), rows(yb), rows(ga), rows(gb),
                  full(wa), full(wb), full(wo), full(gffn), full(w1), full(w2)],
        out_specs=pl.BlockSpec((tm, d), lambda i: (i, 0)),
        out_shape=jax.ShapeDtypeStruct((s, d), F32),
        compiler_params=pltpu.CompilerParams(dimension_semantics=("arbitrary",), vmem_limit_bytes=VMEM_LIMIT),
        name="post",
    )(x, ya, yb, ga, gb, wa, wb, wo, gffn, w1, w2)


def _bias_by_distance(table, lo, hi, keep):
    dist = np.arange(lo, hi)
    vals = jnp.take(table, jnp.asarray(_rel_bucket_np(dist).astype(np.int32)), axis=0).T
    return jnp.where(jnp.asarray(keep(dist))[None, :], vals, NEG_INF)


def _toeplitz(w, n):
    lead = w.shape[:-1]
    w_pad = jnp.concatenate([w, jnp.zeros(lead + (1,), w.dtype)], axis=-1)
    skew = jnp.tile(w_pad, (1,) * len(lead) + (n,))[..., :n * (2 * n - 1)].reshape(lead + (n, 2 * n - 1))
    return skew[..., n - 1:]


def _bias_tables(rel_bias):
    blk, w = MOBA_BLOCK, SWA_WINDOW
    vec = _bias_by_distance(rel_bias[:, :MOBA_HEADS], 1 - blk, MOBA_NEAR * blk, lambda dist: dist >= 0)
    vec = jnp.concatenate([vec, jnp.zeros((MOBA_HEADS, 1), F32)], axis=1)
    moba = jnp.stack([jnp.roll(vec[:, d * blk:(d + 2) * blk], 1 - blk, axis=1) for d in range(MOBA_NEAR)])
    moba = moba.reshape(MOBA_NEAR, HEAD_PAIRS, 2, 2 * blk)
    cfar = rel_bias[N_REL_BUCKETS - 1, :MOBA_HEADS]

    vec = _bias_by_distance(rel_bias[:, MOBA_HEADS:], 1 - w, 3 * w, lambda dist: (dist >= 0) & (dist < w))
    swa = _toeplitz(vec, 2 * w)[:, :, :w]
    swa = swa.reshape(SWA_Q_HEADS // 2, 2, 2 * w, w).transpose(0, 2, 1, 3).reshape(SWA_Q_HEADS // 2, 2 * w, 2 * w)
    return (moba * LOG2E).astype(F32), (cfar * LOG2E).astype(F32), (swa * LOG2E).astype(F32)


def kernel(x, rel_bias, g_mix, w_in, b_gate, q_norm_a, k_norm_a, q_norm_b, k_norm_b, sinks,
           w_branch_a, w_branch_b, w_out, g_ffn, w_ff1, w_ff2):
    b, s, d = x.shape
    assert b == 1 and s % MOBA_BLOCK == 0 and s // MOBA_BLOCK >= MOBA_NEAR
    depth = w_in.shape[0]
    bias_moba, cfar, bias_swa = _bias_tables(rel_bias)
    row = lambda a: a[:, None, :]
    gains = lambda g, width: row(jnp.tile(g, (1, width // HEAD_DIM)))
    w_in_b, wa_b, wb_b, wo_b, w1_b, w2_b = (a.astype(BF16) for a in (w_in, w_branch_a, w_branch_b, w_out, w_ff1, w_ff2))
    gqa, gka = gains(q_norm_a, _W_QA), gains(k_norm_a, _W_KA)
    gqb, gkb = gains(q_norm_b, _W_QB), gains(k_norm_b, SWA_KV_HEADS * HEAD_DIM)
    sinks2 = sinks * LOG2E
    xs = x[0]
    for l in range(depth):
        qa, ka, kmean, vat, qb, kbd, vbt, ga, gb = _proj_call(l, xs, row(g_mix), w_in_b, row(b_gate), gqa, gka, gqb, gkb)
        ya = _moba_call(cfar, qa, ka, vat, kmean.reshape(kmean.shape[0], kmean.shape[2]), bias_moba)
        yb = _swa_call(sinks2[l], qb, kbd, vbt, bias_swa)
        xs = _post_call(l, xs, ya, yb, ga, gb, wa_b, wb_b, wo_b, row(g_ffn), w1_b, w2_b)
    return xs[None]
```

```python
import functools

import jax
import jax.numpy as jnp
import numpy as np
from jax import lax
from jax.experimental import pallas as pl
from jax.experimental.pallas import tpu as pltpu

HEAD_DIM = 64
MOBA_HEADS = 8
MOBA_BLOCK = 256
MOBA_TOPK = 3
SWA_Q_HEADS = 8
SWA_KV_HEADS = 2
SWA_WINDOW = 128
SWA_STEP_BLOCKS = 8
SWA_SCORES_AHEAD = 4
PROJ_STEP_BLOCKS = 4
N_REL_BUCKETS = 32
REL_MAX_DISTANCE = 2048
NORM_EPS = 1e-6
LOG2E = 1.4426950408889634
ATTN_SCALE = HEAD_DIM ** -0.5 * LOG2E

LANES = 128
HEAD_PAIRS = MOBA_HEADS // 2
VT_ROWS = 80
MOBA_NEAR = 7
MOBA_FAR_UNROLL = 2
assert MOBA_FAR_UNROLL < MOBA_NEAR
MOBA_LOOP_GROUPS = 8
VMEM_LIMIT = 56 * 1024 * 1024

BF16 = jnp.bfloat16
F32 = jnp.float32
NEG_INF = float("-inf")
M_INIT = -1e30


def _rel_bucket_np(dist):
    n = np.maximum(dist, 0)
    exact = N_REL_BUCKETS // 2
    nf = np.maximum(n, 1).astype(np.float32)
    large = exact + (np.log(nf / np.float32(exact)) / np.float32(np.log(REL_MAX_DISTANCE / exact))
                     * np.float32(N_REL_BUCKETS - exact)).astype(np.int32)
    large = np.minimum(large, N_REL_BUCKETS - 1)
    return np.where(n < exact, n, large)


def _nt_dot(a, b):
    return lax.dot_general(a, b, (((1,), (1,)), ((), ())), preferred_element_type=F32)


def _dot(a, b):
    return jnp.dot(a, b, preferred_element_type=F32)


def _rms_norm_rows(xf, g):
    ms = jnp.mean(xf * xf, axis=-1, keepdims=True)
    return xf * lax.rsqrt(ms + NORM_EPS) * g


def _head_norm(acc, g2):
    w = acc.shape[1]
    lane = lax.broadcasted_iota(jnp.int32, (1, LANES), 1)
    lo = lane < HEAD_DIM
    outs = []
    for b in range(w // LANES):
        xb = acc[:, b * LANES:(b + 1) * LANES]
        x2 = xb * xb
        s_lo = jnp.sum(jnp.where(lo, x2, 0.0), axis=-1, keepdims=True)
        s_hi = jnp.sum(jnp.where(lo, 0.0, x2), axis=-1, keepdims=True)
        r = jnp.where(lo, lax.rsqrt(s_lo / HEAD_DIM + NORM_EPS), lax.rsqrt(s_hi / HEAD_DIM + NORM_EPS))
        outs.append(xb * r * g2[:, b * LANES:(b + 1) * LANES])
    return outs[0] if len(outs) == 1 else jnp.concatenate(outs, axis=1)


def _v_transposed(v, n_heads):
    rows = v.shape[0]
    vt = v.T
    r = lax.broadcasted_iota(jnp.int32, (VT_ROWS - HEAD_DIM, rows), 0)
    aug = jnp.where(r == 0, 1.0, 0.0).astype(F32)
    parts = []
    for h in range(n_heads):
        parts.append(vt[h * HEAD_DIM:(h + 1) * HEAD_DIM])
        parts.append(aug)
    return jnp.concatenate(parts, axis=0).astype(BF16)


_W_QA, _W_KA, _W_VA, _W_QB, _W_KB, _W_VB, _W_GA, _W_GB = 512, 512, 512, 512, 256, 128, 1024, 1024
_PROJ_COLS = np.cumsum([0, _W_QA, _W_KA, _W_VA, _W_QB, SWA_KV_HEADS * HEAD_DIM, _W_VB, _W_GA, _W_GB]).tolist()


def _proj_kernel(x_ref, gmix_ref, w_ref, bgate_ref, gqa_ref, gka_ref, gqb_ref, gkb_ref,
                 qa_ref, ka_ref, kmean_ref, vat_ref, qb_ref, kb_ref, vbt_ref, ga_ref, gb_ref):
    h = _rms_norm_rows(x_ref[...], gmix_ref[...]).astype(BF16)
    c = _PROJ_COLS

    def seg(k):
        return _dot(h, w_ref[:, c[k]:c[k + 1]])

    qa_ref[...] = (_head_norm(seg(0), gqa_ref[...]) * ATTN_SCALE).astype(BF16)
    ka = _head_norm(seg(1), gka_ref[...])
    ka_ref[...] = ka.astype(BF16)
    for b in range(PROJ_STEP_BLOCKS):
        kmean_ref[b] = jnp.mean(ka[b * MOBA_BLOCK:(b + 1) * MOBA_BLOCK], axis=0, keepdims=True)
    vat_ref[...] = _v_transposed(seg(2), MOBA_HEADS)
    qb_ref[...] = (_head_norm(seg(3), gqb_ref[...]) * ATTN_SCALE).astype(BF16)
    kb = _head_norm(seg(4), gkb_ref[...])
    kb_swapped = pltpu.roll(kb, HEAD_DIM, axis=1)
    lo = lax.broadcasted_iota(jnp.int32, (1, LANES), 1) < HEAD_DIM
    kb_ref[...] = jnp.concatenate([jnp.where(lo, kb, kb_swapped), jnp.where(lo, kb_swapped, kb)], axis=1).astype(BF16)
    vbt_ref[...] = _v_transposed(seg(5), SWA_KV_HEADS)
    bg = bgate_ref[...]
    d = _W_GA
    ga_ref[...] = 1.0 / (1.0 + jnp.exp(-(seg(6) + bg[:, :d])))
    gb_ref[...] = 1.0 / (1.0 + jnp.exp(-(seg(7) + bg[:, d:])))


def _layer_spec(a, l, **kw):
    return pl.BlockSpec((None,) + a.shape[1:], lambda i: (l,) + (0,) * (a.ndim - 1), **kw)


def _proj_call(l, x, gmix, w, bgate, gqa, gka, gqb, gkb):
    s, d = x.shape
    tm = PROJ_STEP_BLOCKS * MOBA_BLOCK
    assert s % tm == 0
    nblk = s // MOBA_BLOCK
    full = lambda a: _layer_spec(a, l, pipeline_mode=pl.Buffered(1))
    rows = lambda width: pl.BlockSpec((tm, width), lambda i: (i, 0))
    cols = lambda height: pl.BlockSpec((height, tm), lambda i: (0, i))
    out_shape = (
        jax.ShapeDtypeStruct((s, _W_QA), BF16),
        jax.ShapeDtypeStruct((s, _W_KA), BF16),
        jax.ShapeDtypeStruct((nblk, 1, _W_KA), F32),
        jax.ShapeDtypeStruct((MOBA_HEADS * VT_ROWS, s), BF16),
        jax.ShapeDtypeStruct((s, _W_QB), BF16),
        jax.ShapeDtypeStruct((s, _W_KB), BF16),
        jax.ShapeDtypeStruct((SWA_KV_HEADS * VT_ROWS, s), BF16),
        jax.ShapeDtypeStruct((s, _W_GA), F32),
        jax.ShapeDtypeStruct((s, _W_GB), F32),
    )
    out_specs = (
        rows(_W_QA), rows(_W_KA), pl.BlockSpec((PROJ_STEP_BLOCKS, 1, _W_KA), lambda i: (i, 0, 0)),
        cols(MOBA_HEADS * VT_ROWS), rows(_W_QB), rows(_W_KB), cols(SWA_KV_HEADS * VT_ROWS),
        rows(_W_GA), rows(_W_GB),
    )
    return pl.pallas_call(
        _proj_kernel,
        grid=(s // tm,),
        in_specs=[rows(d), full(gmix), full(w), full(bgate), full(gqa), full(gka), full(gqb), full(gkb)],
        out_specs=out_specs,
        out_shape=out_shape,
        compiler_params=pltpu.CompilerParams(dimension_semantics=("arbitrary",), vmem_limit_bytes=VMEM_LIMIT),
        name="proj",
    )(x, gmix, w, bgate, gqa, gka, gqb, gkb)


def _moba_kernel(cfar_ref, q_ref, k_ref, vt_ref, kmean_ref, diag_ref, o_ref,
                 qq_scr, sel_scr, m_scr, acc_scr, sta_scr, stb_scr, bias_ref, *, nblk):
    p = pl.program_id(0)
    i = pl.program_id(1)
    blk = MOBA_BLOCK

    @pl.when(i == 0)
    def _():
        for d in range(MOBA_NEAR):
            for hh in range(2):
                skew = pltpu.roll(jnp.broadcast_to(diag_ref[d, hh:hh + 1, :], (blk, 2 * blk)), 0, 1,
                                  stride=1, stride_axis=0)
                bias_ref[d, :, hh * blk:(hh + 1) * blk] = skew[:, :blk]

    q = q_ref[...]
    lane = lax.broadcasted_iota(jnp.int32, (1, LANES), 1)
    km = kmean_ref[...]
    km_hi = km.astype(BF16)
    km_lo = (km - km_hi.astype(F32)).astype(BF16)
    n_iota = lax.broadcasted_iota(jnp.int32, (nblk, 2 * blk), 0)

    qq = jnp.concatenate([jnp.where(lane < HEAD_DIM, q, jnp.zeros_like(q)),
                          jnp.where(lane >= HEAD_DIM, q, jnp.zeros_like(q))], axis=0)
    qq_scr[...] = qq

    gate2 = _nt_dot(jnp.concatenate([km_hi, km_lo], axis=0), qq)
    gates = gate2[:nblk] + gate2[nblk:]

    def select_blocks():
        g = jnp.where(n_iota < i, gates, NEG_INF)
        sel = jnp.zeros((nblk, 2 * blk), F32)
        for _ in range(MOBA_TOPK):
            mx = jnp.max(g, axis=0, keepdims=True)
            idx = jnp.min(jnp.where(g == mx, n_iota, nblk), axis=0, keepdims=True)
            hit = n_iota == idx
            valid = jnp.where(mx > NEG_INF, 1.0, 0.0)
            sel = jnp.maximum(sel, jnp.where(hit, valid, 0.0))
            g = jnp.where(hit, NEG_INF, g)
        sel_scr[...] = jnp.where(n_iota == i, 1.0, sel)
        acc_scr[...] = jnp.zeros((2, VT_ROWS, blk), F32)
        m_scr[...] = jnp.full((1, 2 * blk), M_INIT, F32)

    col = lax.broadcasted_iota(jnp.int32, (1, 2 * blk), 1)
    cfar = jnp.where(col < blk, cfar_ref[2 * p], cfar_ref[2 * p + 1])
    n_far = jnp.maximum(i - (MOBA_NEAR - 1), 0)

    def scores(first, nb, st_ref):
        off = pl.multiple_of(first * blk, blk)
        st_ref[:nb * blk, :2 * blk] = _nt_dot(k_ref[pl.ds(off, nb * blk), :], qq_scr[...])

    def attend(first, nb, st_ref, near):
        ms, pvs = [], []
        for u in range(nb):
            j = first + u
            s_u = st_ref[u * blk:(u + 1) * blk, :2 * blk]
            if near:
                s_u = s_u + bias_ref[jnp.clip(i - j, 0, MOBA_NEAR - 1)]
                on = sel_scr[pl.ds(j, 1), :] > 0.5
                m_u = jnp.max(s_u, axis=0, keepdims=True)
                ms.append(jnp.where(on, m_u, NEG_INF))
            else:
                on = sel_scr[pl.ds(j, 1), :] > jnp.where(j < n_far, 0.5, 2.0)
                m_u = jnp.max(s_u, axis=0, keepdims=True)
                ms.append(jnp.where(on, m_u + cfar, NEG_INF))
            pt = jnp.exp2(s_u - m_u).astype(BF16)
            ko = pl.multiple_of(j * blk, blk)
            pvs.append([_dot(vt_ref[hh * VT_ROWS:(hh + 1) * VT_ROWS, pl.ds(ko, blk)], pt[:, hh * blk:(hh + 1) * blk])
                        for hh in range(2)])
        m_old = m_scr[...]
        m_new = m_old
        for m_u in ms:
            m_new = jnp.maximum(m_new, m_u)
        alpha = jnp.exp2(m_old - m_new)
        ws = [jnp.exp2(m_u - m_new) for m_u in ms]
        for hh in range(2):
            cols = slice(hh * blk, (hh + 1) * blk)
            acc = acc_scr[hh] * alpha[:, cols]
            for w_u, pv in zip(ws, pvs):
                acc = acc + pv[hh] * w_u[:, cols]
            acc_scr[hh] = acc
        m_scr[...] = m_new

    fu = MOBA_FAR_UNROLL
    n_groups = (n_far + fu - 1) // fu
    near0 = jnp.maximum(i - (MOBA_NEAR - 1), 0)
    bufs = (sta_scr, stb_scr)

    def item_start(g):
        return jnp.where(g < n_groups, g * fu, near0)

    def run_items(items, then_first=None):
        for k, (first, nb, near) in enumerate(items):
            if k + 1 < len(items):
                scores(items[k + 1][0], items[k + 1][1], bufs[(k + 1) % 2])
            elif then_first is not None:
                scores(then_first, fu, bufs[(k + 1) % 2])
            attend(first, nb, bufs[k % 2], near)

    near_sizes = [fu] + [2] * ((MOBA_NEAR - fu - 1) // 2) + [1] * (1 + (MOBA_NEAR - fu - 1) % 2)
    near_items = [(near0 + int(st), nb, True) for st, nb in zip(np.cumsum([0] + near_sizes[:-1]), near_sizes)]

    scores(item_start(0), fu, sta_scr)
    select_blocks()

    unroll = MOBA_LOOP_GROUPS

    def loop_body(t, carry):
        g0 = t * unroll
        run_items([((g0 + k) * fu, fu, False) for k in range(unroll)], then_first=item_start(g0 + unroll))
        return carry

    lax.fori_loop(0, n_groups // unroll, loop_body, 0)

    g0 = n_groups // unroll * unroll
    for rem in range(unroll):

        @pl.when(n_groups - g0 == rem)
        def _():
            run_items([((g0 + k) * fu, fu, False) for k in range(rem)] + near_items)

    outs = []
    for hh in range(2):
        a = acc_scr[hh]
        outs.append(a[:HEAD_DIM] / a[HEAD_DIM:HEAD_DIM + 1])
    o_ref[...] = jnp.concatenate(outs, axis=0).T.astype(BF16)


def _moba_call(cfar, qa, ka, vat, kmean, bias_t):
    s = qa.shape[0]
    blk = MOBA_BLOCK
    nblk = s // blk
    grid_spec = pltpu.PrefetchScalarGridSpec(
        num_scalar_prefetch=1,
        grid=(HEAD_PAIRS, nblk),
        in_specs=[
            pl.BlockSpec((blk, LANES), lambda p, i, c: (i, p)),
            pl.BlockSpec((s, LANES), lambda p, i, c: (0, p)),
            pl.BlockSpec((2 * VT_ROWS, s), lambda p, i, c: (p, 0)),
            pl.BlockSpec((nblk, LANES), lambda p, i, c: (0, p)),
            pl.BlockSpec((MOBA_NEAR, None, 2, 2 * blk), lambda p, i, c: (0, p, 0, 0)),
        ],
        out_specs=pl.BlockSpec((blk, LANES), lambda p, i, c: (i, p)),
        scratch_shapes=[
            pltpu.VMEM((2 * blk, LANES), BF16),
            pltpu.VMEM((nblk, 2 * blk), F32),
            pltpu.VMEM((1, 2 * blk), F32),
            pltpu.VMEM((2, VT_ROWS, blk), F32),
            pltpu.VMEM((MOBA_FAR_UNROLL * blk, 2 * blk + LANES), F32),
            pltpu.VMEM((MOBA_FAR_UNROLL * blk, 2 * blk + LANES), F32),
            pltpu.VMEM((MOBA_NEAR, blk, 2 * blk), F32),
        ],
    )
    return pl.pallas_call(
        functools.partial(_moba_kernel, nblk=nblk),
        grid_spec=grid_spec,
        out_shape=jax.ShapeDtypeStruct((s, MOBA_HEADS * HEAD_DIM), BF16),
        compiler_params=pltpu.CompilerParams(dimension_semantics=("arbitrary", "arbitrary"),
                                             vmem_limit_bytes=VMEM_LIMIT),
        name="moba",
    )(cfar, qa, ka, vat, kmean, bias_t)


def _swa_kernel(sink_ref, q_ref, kprev_ref, kcur_ref, vprev_ref, vcur_ref, bias_ref, o_ref, st_scr):
    b = pl.program_id(0)
    w = SWA_WINDOW
    kall = jnp.concatenate([kprev_ref[...], kcur_ref[...]], axis=0)
    vall = jnp.concatenate([vprev_ref[...], vcur_ref[...]], axis=1)
    row = lax.broadcasted_iota(jnp.int32, (2 * w, 1), 0)
    keep = row >= jnp.where(b > 0, 0, w)
    lane = lax.broadcasted_iota(jnp.int32, (1, LANES), 1)
    col = lax.broadcasted_iota(jnp.int32, (1, 2 * w), 1)
    group = SWA_Q_HEADS // SWA_KV_HEADS
    chains = [(sb, pr) for sb in range(SWA_STEP_BLOCKS) for pr in range(SWA_Q_HEADS // 2)]
    def scores(c):
        sb, pr = chains[c]
        g = (2 * pr) // group
        qp = q_ref[sb * w:(sb + 1) * w, pr * LANES:(pr + 1) * LANES]
        zero = jnp.zeros_like(qp)
        qq = jnp.concatenate([jnp.where(lane < HEAD_DIM, qp, zero),
                              jnp.where(lane >= HEAD_DIM, qp, zero)], axis=0)
        kd = kall[sb * w:(sb + 2) * w, g * LANES:(g + 1) * LANES]
        st_scr[c, :, :2 * w] = _nt_dot(kd, qq)

    ahead = SWA_SCORES_AHEAD
    for c in range(min(ahead, len(chains))):
        scores(c)
    for c, (sb, pr) in enumerate(chains):
        if c + ahead < len(chains):
            scores(c + ahead)
        g = (2 * pr) // group
        st = st_scr[c, :, :2 * w] + bias_ref[pr]
        if sb == 0:
            st = jnp.where(keep, st, NEG_INF)
        sink = jnp.where(col < w, sink_ref[2 * pr], sink_ref[2 * pr + 1])
        m = jnp.maximum(jnp.max(st, axis=0, keepdims=True), sink)
        pt = jnp.exp2(st - m).astype(BF16)
        acc = _dot(vall[g * VT_ROWS:(g + 1) * VT_ROWS, sb * w:(sb + 2) * w], pt)
        denom = acc[HEAD_DIM:HEAD_DIM + 1] + jnp.exp2(sink - m)
        o = acc[:HEAD_DIM] / denom
        oo = jnp.concatenate([o[:, :w], o[:, w:]], axis=0)
        o_ref[sb * w:(sb + 1) * w, pr * LANES:(pr + 1) * LANES] = oo.T.astype(BF16)


def _swa_call(sinks, qb, kbd, vbt, bias_t):
    s = qb.shape[0]
    w = SWA_WINDOW
    n = SWA_STEP_BLOCKS
    assert s % (n * w) == 0
    prev_blk = lambda i: jnp.maximum(n * i - 1, 0)
    grid_spec = pltpu.PrefetchScalarGridSpec(
        num_scalar_prefetch=1,
        grid=(s // (n * w),),
        in_specs=[
            pl.BlockSpec((n * w, _W_QB), lambda i, c: (i, 0)),
            pl.BlockSpec((w, _W_KB), lambda i, c: (prev_blk(i), 0)),
            pl.BlockSpec((n * w, _W_KB), lambda i, c: (i, 0)),
            pl.BlockSpec((SWA_KV_HEADS * VT_ROWS, w), lambda i, c: (0, prev_blk(i))),
            pl.BlockSpec((SWA_KV_HEADS * VT_ROWS, n * w), lambda i, c: (0, i)),
            pl.BlockSpec(bias_t.shape, lambda i, c: (0, 0, 0)),
        ],
        out_specs=pl.BlockSpec((n * w, _W_QB), lambda i, c: (i, 0)),
        scratch_shapes=[pltpu.VMEM((n * SWA_Q_HEADS // 2, 2 * w, 2 * w + LANES), F32)],
    )
    return pl.pallas_call(
        _swa_kernel,
        grid_spec=grid_spec,
        out_shape=jax.ShapeDtypeStruct((s, _W_QB), BF16),
        compiler_params=pltpu.CompilerParams(dimension_semantics=("arbitrary",), vmem_limit_bytes=VMEM_LIMIT),
        name="swa",
    )(sinks, qb, kbd, kbd, vbt, vbt, bias_t)


def _post_kernel(x_ref, ya_ref, yb_ref, ga_ref, gb_ref, wa_ref, wb_ref, wo_ref, gffn_ref, w1_ref, w2_ref, o_ref,
                 *, tf):
    merged = ga_ref[...] * _dot(ya_ref[...], wa_ref[...]) + gb_ref[...] * _dot(yb_ref[...], wb_ref[...])
    xn = x_ref[...] + _dot(merged.astype(BF16), wo_ref[...])
    h = _rms_norm_rows(xn, gffn_ref[...]).astype(BF16)
    acc = xn
    for c in range(w1_ref.shape[1] // tf):
        u = jnp.maximum(_dot(h, w1_ref[:, c * tf:(c + 1) * tf]), 0.0)
        acc = acc + _dot((u * u).astype(BF16), w2_ref[c * tf:(c + 1) * tf, :])
    o_ref[...] = acc


def _post_call(l, x, ya, yb, ga, gb, wa, wb, wo, gffn, w1, w2, tm=512, tf=1024):
    s, d = x.shape
    full = lambda a: _layer_spec(a, l, pipeline_mode=pl.Buffered(1))
    rows = lambda a: pl.BlockSpec((tm, a.shape[1]), lambda i: (i, 0))
    return pl.pallas_call(
        functools.partial(_post_kernel, tf=tf),
        grid=(s // tm,),
        in_specs=[rows(x), rows(ya), rows(yb), rows(ga), rows(gb),
                  full(wa), full(wb), full(wo), full(gffn), full(w1), full(w2)],
        out_specs=pl.BlockSpec((tm, d), lambda i: (i, 0)),
        out_shape=jax.ShapeDtypeStruct((s, d), F32),
        compiler_params=pltpu.CompilerParams(dimension_semantics=("arbitrary",), vmem_limit_bytes=VMEM_LIMIT),
        name="post",
    )(x, ya, yb, ga, gb, wa, wb, wo, gffn, w1, w2)


def _bias_by_distance(table, lo, hi, keep):
    dist = np.arange(lo, hi)
    vals = jnp.take(table, jnp.asarray(_rel_bucket_np(dist).astype(np.int32)), axis=0).T
    return jnp.where(jnp.asarray(keep(dist))[None, :], vals, NEG_INF)


def _toeplitz(w, n):
    lead = w.shape[:-1]
    w_pad = jnp.concatenate([w, jnp.zeros(lead + (1,), w.dtype)], axis=-1)
    skew = jnp.tile(w_pad, (1,) * len(lead) + (n,))[..., :n * (2 * n - 1)].reshape(lead + (n, 2 * n - 1))
    return skew[..., n - 1:]


def _bias_tables(rel_bias):
    blk, w = MOBA_BLOCK, SWA_WINDOW
    vec = _bias_by_distance(rel_bias[:, :MOBA_HEADS], 1 - blk, MOBA_NEAR * blk, lambda dist: dist >= 0)
    vec = jnp.concatenate([vec, jnp.zeros((MOBA_HEADS, 1), F32)], axis=1)
    moba = jnp.stack([jnp.roll(vec[:, d * blk:(d + 2) * blk], 1 - blk, axis=1) for d in range(MOBA_NEAR)])
    moba = moba.reshape(MOBA_NEAR, HEAD_PAIRS, 2, 2 * blk)
    cfar = rel_bias[N_REL_BUCKETS - 1, :MOBA_HEADS]

    vec = _bias_by_distance(rel_bias[:, MOBA_HEADS:], 1 - w, 3 * w, lambda dist: (dist >= 0) & (dist < w))
    swa = _toeplitz(vec, 2 * w)[:, :, :w]
    swa = swa.reshape(SWA_Q_HEADS // 2, 2, 2 * w, w).transpose(0, 2, 1, 3).reshape(SWA_Q_HEADS // 2, 2 * w, 2 * w)
    return (moba * LOG2E).astype(F32), (cfar * LOG2E).astype(F32), (swa * LOG2E).astype(F32)


def kernel(x, rel_bias, g_mix, w_in, b_gate, q_norm_a, k_norm_a, q_norm_b, k_norm_b, sinks,
           w_branch_a, w_branch_b, w_out, g_ffn, w_ff1, w_ff2):
    b, s, d = x.shape
    assert b == 1 and s % MOBA_BLOCK == 0 and s // MOBA_BLOCK >= MOBA_NEAR
    depth = w_in.shape[0]
    bias_moba, cfar, bias_swa = _bias_tables(rel_bias)
    row = lambda a: a[:, None, :]
    gains = lambda g, width: row(jnp.tile(g, (1, width // HEAD_DIM)))
    w_in_b, wa_b, wb_b, wo_b, w1_b, w2_b = (a.astype(BF16) for a in (w_in, w_branch_a, w_branch_b, w_out, w_ff1, w_ff2))
    gqa, gka = gains(q_norm_a, _W_QA), gains(k_norm_a, _W_KA)
    gqb, gkb = gains(q_norm_b, _W_QB), gains(k_norm_b, SWA_KV_HEADS * HEAD_DIM)
    sinks2 = sinks * LOG2E
    xs = x[0]
    for l in range(depth):
        qa, ka, kmean, vat, qb, kbd, vbt, ga, gb = _proj_call(l, xs, row(g_mix), w_in_b, row(b_gate), gqa, gka, gqb, gkb)
        ya = _moba_call(cfar, qa, ka, vat, kmean.reshape(kmean.shape[0], kmean.shape[2]), bias_moba)
        yb = _swa_call(sinks2[l], qb, kbd, vbt, bias_swa)
        xs = _post_call(l, xs, ya, yb, ga, gb, wa_b, wb_b, wo_b, row(g_ffn), w1_b, w2_b)
    return xs[None]
```

```python
import functools

import jax
import jax.numpy as jnp
import numpy as np
from jax import lax
from jax.experimental import pallas as pl
from jax.experimental.pallas import tpu as pltpu

HEAD_DIM = 64
MOBA_HEADS = 8
MOBA_BLOCK = 256
MOBA_TOPK = 3
SWA_Q_HEADS = 8
SWA_KV_HEADS = 2
SWA_WINDOW = 128
SWA_STEP_BLOCKS = 8
SWA_SCORES_AHEAD = 4
PROJ_STEP_BLOCKS = 4
N_REL_BUCKETS = 32
REL_MAX_DISTANCE = 2048
NORM_EPS = 1e-6
LOG2E = 1.4426950408889634
ATTN_SCALE = HEAD_DIM ** -0.5 * LOG2E

LANES = 128
HEAD_PAIRS = MOBA_HEADS // 2
VT_ROWS = 80
MOBA_NEAR = 7
MOBA_FAR_UNROLL = 2
assert 2 <= MOBA_FAR_UNROLL < MOBA_NEAR
MOBA_LOOP_GROUPS = 8
assert MOBA_LOOP_GROUPS % 2 == 0
VMEM_LIMIT = 56 * 1024 * 1024

BF16 = jnp.bfloat16
F32 = jnp.float32
NEG_INF = float("-inf")
M_INIT = -1e30


def _rel_bucket_np(dist):
    n = np.maximum(dist, 0)
    exact = N_REL_BUCKETS // 2
    nf = np.maximum(n, 1).astype(np.float32)
    large = exact + (np.log(nf / np.float32(exact)) / np.float32(np.log(REL_MAX_DISTANCE / exact))
                     * np.float32(N_REL_BUCKETS - exact)).astype(np.int32)
    large = np.minimum(large, N_REL_BUCKETS - 1)
    return np.where(n < exact, n, large)


def _nt_dot(a, b):
    return lax.dot_general(a, b, (((1,), (1,)), ((), ())), preferred_element_type=F32)


def _dot(a, b):
    return jnp.dot(a, b, preferred_element_type=F32)


def _rms_norm_rows(xf, g):
    ms = jnp.mean(xf * xf, axis=-1, keepdims=True)
    return xf * lax.rsqrt(ms + NORM_EPS) * g


def _head_norm(acc, g2):
    w = acc.shape[1]
    lane = lax.broadcasted_iota(jnp.int32, (1, LANES), 1)
    lo = lane < HEAD_DIM
    outs = []
    for b in range(w // LANES):
        xb = acc[:, b * LANES:(b + 1) * LANES]
        x2 = xb * xb
        s_lo = jnp.sum(jnp.where(lo, x2, 0.0), axis=-1, keepdims=True)
        s_hi = jnp.sum(jnp.where(lo, 0.0, x2), axis=-1, keepdims=True)
        r = jnp.where(lo, lax.rsqrt(s_lo / HEAD_DIM + NORM_EPS), lax.rsqrt(s_hi / HEAD_DIM + NORM_EPS))
        outs.append(xb * r * g2[:, b * LANES:(b + 1) * LANES])
    return outs[0] if len(outs) == 1 else jnp.concatenate(outs, axis=1)


def _v_transposed(v, n_heads):
    rows = v.shape[0]
    vt = v.T
    r = lax.broadcasted_iota(jnp.int32, (VT_ROWS - HEAD_DIM, rows), 0)
    aug = jnp.where(r == 0, 1.0, 0.0).astype(F32)
    parts = []
    for h in range(n_heads):
        parts.append(vt[h * HEAD_DIM:(h + 1) * HEAD_DIM])
        parts.append(aug)
    return jnp.concatenate(parts, axis=0).astype(BF16)


_W_QA, _W_KA, _W_VA, _W_QB, _W_KB, _W_VB, _W_GA, _W_GB = 512, 512, 512, 512, 256, 128, 1024, 1024
_PROJ_COLS = np.cumsum([0, _W_QA, _W_KA, _W_VA, _W_QB, SWA_KV_HEADS * HEAD_DIM, _W_VB, _W_GA, _W_GB]).tolist()


def _proj_kernel(x_ref, gmix_ref, w_ref, bgate_ref, gqa_ref, gka_ref, gqb_ref, gkb_ref,
                 qa_ref, ka_ref, kmean_ref, vat_ref, qb_ref, kb_ref, vbt_ref, ga_ref, gb_ref):
    h = _rms_norm_rows(x_ref[...], gmix_ref[...]).astype(BF16)
    c = _PROJ_COLS

    def seg(k):
        return _dot(h, w_ref[:, c[k]:c[k + 1]])

    qa_ref[...] = (_head_norm(seg(0), gqa_ref[...]) * ATTN_SCALE).astype(BF16)
    ka = _head_norm(seg(1), gka_ref[...])
    ka_ref[...] = ka.astype(BF16)
    for b in range(PROJ_STEP_BLOCKS):
        kmean_ref[b] = jnp.mean(ka[b * MOBA_BLOCK:(b + 1) * MOBA_BLOCK], axis=0, keepdims=True)
    vat_ref[...] = _v_transposed(seg(2), MOBA_HEADS)
    qb_ref[...] = (_head_norm(seg(3), gqb_ref[...]) * ATTN_SCALE).astype(BF16)
    kb = _head_norm(seg(4), gkb_ref[...])
    kb_swapped = pltpu.roll(kb, HEAD_DIM, axis=1)
    lo = lax.broadcasted_iota(jnp.int32, (1, LANES), 1) < HEAD_DIM
    kb_ref[...] = jnp.concatenate([jnp.where(lo, kb, kb_swapped), jnp.where(lo, kb_swapped, kb)], axis=1).astype(BF16)
    vbt_ref[...] = _v_transposed(seg(5), SWA_KV_HEADS)
    bg = bgate_ref[...]
    d = _W_GA
    ga_ref[...] = 1.0 / (1.0 + jnp.exp(-(seg(6) + bg[:, :d])))
    gb_ref[...] = 1.0 / (1.0 + jnp.exp(-(seg(7) + bg[:, d:])))


def _layer_spec(a, l, **kw):
    return pl.BlockSpec((None,) + a.shape[1:], lambda i: (l,) + (0,) * (a.ndim - 1), **kw)


def _proj_call(l, x, gmix, w, bgate, gqa, gka, gqb, gkb):
    s, d = x.shape
    tm = PROJ_STEP_BLOCKS * MOBA_BLOCK
    assert s % tm == 0
    nblk = s // MOBA_BLOCK
    full = lambda a: _layer_spec(a, l, pipeline_mode=pl.Buffered(1))
    rows = lambda width: pl.BlockSpec((tm, width), lambda i: (i, 0))
    cols = lambda height: pl.BlockSpec((height, tm), lambda i: (0, i))
    out_shape = (
        jax.ShapeDtypeStruct((s, _W_QA), BF16),
        jax.ShapeDtypeStruct((s, _W_KA), BF16),
        jax.ShapeDtypeStruct((nblk, 1, _W_KA), F32),
        jax.ShapeDtypeStruct((MOBA_HEADS * VT_ROWS, s), BF16),
        jax.ShapeDtypeStruct((s, _W_QB), BF16),
        jax.ShapeDtypeStruct((s, _W_KB), BF16),
        jax.ShapeDtypeStruct((SWA_KV_HEADS * VT_ROWS, s), BF16),
        jax.ShapeDtypeStruct((s, _W_GA), F32),
        jax.ShapeDtypeStruct((s, _W_GB), F32),
    )
    out_specs = (
        rows(_W_QA), rows(_W_KA), pl.BlockSpec((PROJ_STEP_BLOCKS, 1, _W_KA), lambda i: (i, 0, 0)),
        cols(MOBA_HEADS * VT_ROWS), rows(_W_QB), rows(_W_KB), cols(SWA_KV_HEADS * VT_ROWS),
        rows(_W_GA), rows(_W_GB),
    )
    return pl.pallas_call(
        _proj_kernel,
        grid=(s // tm,),
        in_specs=[rows(d), full(gmix), full(w), full(bgate), full(gqa), full(gka), full(gqb), full(gkb)],
        out_specs=out_specs,
        out_shape=out_shape,
        compiler_params=pltpu.CompilerParams(dimension_semantics=("arbitrary",), vmem_limit_bytes=VMEM_LIMIT),
        name="proj",
    )(x, gmix, w, bgate, gqa, gka, gqb, gkb)


def _moba_kernel(cfar_ref, q_ref, k_ref, vt_ref, kmean_ref, diag_ref, o_ref,
                 qq_scr, sel_scr, m_scr, acc_scr, sta_scr, stb_scr, bias_ref, *, nblk):
    p = pl.program_id(0)
    i = pl.program_id(1)
    blk = MOBA_BLOCK

    @pl.when(i == 0)
    def _():
        for d in range(MOBA_NEAR):
            for hh in range(2):
                skew = pltpu.roll(jnp.broadcast_to(diag_ref[d, hh:hh + 1, :], (blk, 2 * blk)), 0, 1,
                                  stride=1, stride_axis=0)
                bias_ref[d, :, hh * blk:(hh + 1) * blk] = skew[:, :blk]

    q = q_ref[...]
    lane = lax.broadcasted_iota(jnp.int32, (1, LANES), 1)
    km = kmean_ref[...]
    km_hi = km.astype(BF16)
    km_lo = (km - km_hi.astype(F32)).astype(BF16)
    n_iota = lax.broadcasted_iota(jnp.int32, (nblk, 2 * blk), 0)

    qq = jnp.concatenate([jnp.where(lane < HEAD_DIM, q, jnp.zeros_like(q)),
                          jnp.where(lane >= HEAD_DIM, q, jnp.zeros_like(q))], axis=0)
    qq_scr[...] = qq

    gate2 = _nt_dot(jnp.concatenate([km_hi, km_lo], axis=0), qq)
    gates = gate2[:nblk] + gate2[nblk:]

    def select_blocks():
        g = jnp.where(n_iota < i, gates, NEG_INF)
        sel = jnp.zeros((nblk, 2 * blk), F32)
        for _ in range(MOBA_TOPK):
            mx = jnp.max(g, axis=0, keepdims=True)
            idx = jnp.min(jnp.where(g == mx, n_iota, nblk), axis=0, keepdims=True)
            hit = n_iota == idx
            valid = jnp.where(mx > NEG_INF, 1.0, 0.0)
            sel = jnp.maximum(sel, jnp.where(hit, valid, 0.0))
            g = jnp.where(hit, NEG_INF, g)
        sel_scr[...] = jnp.where(n_iota == i, 1.0, sel)
        acc_scr[...] = jnp.zeros((2, VT_ROWS, blk), F32)
        m_scr[...] = jnp.full((1, 2 * blk), M_INIT, F32)

    col = lax.broadcasted_iota(jnp.int32, (1, 2 * blk), 1)
    cfar = jnp.where(col < blk, cfar_ref[2 * p], cfar_ref[2 * p + 1])
    n_far = jnp.maximum(i - (MOBA_NEAR - 1), 0)

    def scores(first, nb, st_ref):
        off = pl.multiple_of(first * blk, blk)
        st_ref[:nb * blk, :2 * blk] = _nt_dot(k_ref[pl.ds(off, nb * blk), :], qq_scr[...])

    def attend(first, nb, st_ref, near):
        ms, pvs = [], []
        for u in range(nb):
            j = first + u
            s_u = st_ref[u * blk:(u + 1) * blk, :2 * blk]
            if near:
                s_u = s_u + bias_ref[jnp.clip(i - j, 0, MOBA_NEAR - 1)]
                on = sel_scr[pl.ds(j, 1), :] > 0.5
                m_u = jnp.max(s_u, axis=0, keepdims=True)
                ms.append(jnp.where(on, m_u, NEG_INF))
            else:
                on = sel_scr[pl.ds(j, 1), :] > jnp.where(j < n_far, 0.5, 2.0)
                m_u = jnp.max(s_u, axis=0, keepdims=True)
                ms.append(jnp.where(on, m_u + cfar, NEG_INF))
            pt = jnp.exp2(s_u - m_u).astype(BF16)
            ko = pl.multiple_of(j * blk, blk)
            pvs.append([_dot(vt_ref[hh * VT_ROWS:(hh + 1) * VT_ROWS, pl.ds(ko, blk)], pt[:, hh * blk:(hh + 1) * blk])
                        for hh in range(2)])
        m_old = m_scr[...]
        m_new = m_old
        for m_u in ms:
            m_new = jnp.maximum(m_new, m_u)
        alpha = jnp.exp2(m_old - m_new)
        ws = [jnp.exp2(m_u - m_new) for m_u in ms]
        for hh in range(2):
            cols = slice(hh * blk, (hh + 1) * blk)
            acc = acc_scr[hh] * alpha[:, cols]
            for w_u, pv in zip(ws, pvs):
                acc = acc + pv[hh] * w_u[:, cols]
            acc_scr[hh] = acc
        m_scr[...] = m_new

    fu = MOBA_FAR_UNROLL
    n_groups = (n_far + fu - 1) // fu
    near0 = jnp.maximum(i - (MOBA_NEAR - 1), 0)
    bufs = (sta_scr, stb_scr)

    def item_start(g):
        return jnp.where(g < n_groups, g * fu, near0)

    def run_items(items, then_first=None):
        for k, (first, nb, near) in enumerate(items):
            if k + 1 < len(items):
                scores(items[k + 1][0], items[k + 1][1], bufs[(k + 1) % 2])
            elif then_first is not None:
                scores(then_first, fu, bufs[(k + 1) % 2])
            attend(first, nb, bufs[k % 2], near)

    near_sizes = [fu] + [2] * ((MOBA_NEAR - fu - 1) // 2) + [1] * (1 + (MOBA_NEAR - fu - 1) % 2)
    near_items = [(near0 + int(st), nb, True) for st, nb in zip(np.cumsum([0] + near_sizes[:-1]), near_sizes)]

    scores(item_start(0), fu, sta_scr)
    select_blocks()

    unroll = MOBA_LOOP_GROUPS

    def loop_body(t, carry):
        g0 = t * unroll
        run_items([((g0 + k) * fu, fu, False) for k in range(unroll)], then_first=item_start(g0 + unroll))
        return carry

    lax.fori_loop(0, n_groups // unroll, loop_body, 0)

    g0 = n_groups // unroll * unroll
    for rem in range(unroll):

        @pl.when(n_groups - g0 == rem)
        def _():
            run_items([((g0 + k) * fu, fu, False) for k in range(rem)] + near_items)

    outs = []
    for hh in range(2):
        a = acc_scr[hh]
        outs.append(a[:HEAD_DIM] / a[HEAD_DIM:HEAD_DIM + 1])
    o_ref[...] = jnp.concatenate(outs, axis=0).T.astype(BF16)


def _moba_call(cfar, qa, ka, vat, kmean, bias_t):
    s = qa.shape[0]
    blk = MOBA_BLOCK
    nblk = s // blk
    grid_spec = pltpu.PrefetchScalarGridSpec(
        num_scalar_prefetch=1,
        grid=(HEAD_PAIRS, nblk),
        in_specs=[
            pl.BlockSpec((blk, LANES), lambda p, i, c: (i, p)),
            pl.BlockSpec((s, LANES), lambda p, i, c: (0, p)),
            pl.BlockSpec((2 * VT_ROWS, s), lambda p, i, c: (p, 0)),
            pl.BlockSpec((nblk, LANES), lambda p, i, c: (0, p)),
            pl.BlockSpec((MOBA_NEAR, None, 2, 2 * blk), lambda p, i, c: (0, p, 0, 0)),
        ],
        out_specs=pl.BlockSpec((blk, LANES), lambda p, i, c: (i, p)),
        scratch_shapes=[
            pltpu.VMEM((2 * blk, LANES), BF16),
            pltpu.VMEM((nblk, 2 * blk), F32),
            pltpu.VMEM((1, 2 * blk), F32),
            pltpu.VMEM((2, VT_ROWS, blk), F32),
            pltpu.VMEM((MOBA_FAR_UNROLL * blk, 2 * blk + LANES), F32),
            pltpu.VMEM((MOBA_FAR_UNROLL * blk, 2 * blk + LANES), F32),
            pltpu.VMEM((MOBA_NEAR, blk, 2 * blk), F32),
        ],
    )
    return pl.pallas_call(
        functools.partial(_moba_kernel, nblk=nblk),
        grid_spec=grid_spec,
        out_shape=jax.ShapeDtypeStruct((s, MOBA_HEADS * HEAD_DIM), BF16),
        compiler_params=pltpu.CompilerParams(dimension_semantics=("arbitrary", "arbitrary"),
                                             vmem_limit_bytes=VMEM_LIMIT),
        name="moba",
    )(cfar, qa, ka, vat, kmean, bias_t)


def _swa_kernel(sink_ref, q_ref, kprev_ref, kcur_ref, vprev_ref, vcur_ref, bias_ref, o_ref, st_scr):
    b = pl.program_id(0)
    w = SWA_WINDOW
    kall = jnp.concatenate([kprev_ref[...], kcur_ref[...]], axis=0)
    vall = jnp.concatenate([vprev_ref[...], vcur_ref[...]], axis=1)
    row = lax.broadcasted_iota(jnp.int32, (2 * w, 1), 0)
    keep = row >= jnp.where(b > 0, 0, w)
    lane = lax.broadcasted_iota(jnp.int32, (1, LANES), 1)
    col = lax.broadcasted_iota(jnp.int32, (1, 2 * w), 1)
    group = SWA_Q_HEADS // SWA_KV_HEADS
    chains = [(sb, pr) for sb in range(SWA_STEP_BLOCKS) for pr in range(SWA_Q_HEADS // 2)]
    def scores(c):
        sb, pr = chains[c]
        g = (2 * pr) // group
        qp = q_ref[sb * w:(sb + 1) * w, pr * LANES:(pr + 1) * LANES]
        zero = jnp.zeros_like(qp)
        qq = jnp.concatenate([jnp.where(lane < HEAD_DIM, qp, zero),
                              jnp.where(lane >= HEAD_DIM, qp, zero)], axis=0)
        kd = kall[sb * w:(sb + 2) * w, g * LANES:(g + 1) * LANES]
        st_scr[c, :, :2 * w] = _nt_dot(kd, qq)

    ahead = SWA_SCORES_AHEAD
    for c in range(min(ahead, len(chains))):
        scores(c)
    for c, (sb, pr) in enumerate(chains):
        if c + ahead < len(chains):
            scores(c + ahead)
        g = (2 * pr) // group
        st = st_scr[c, :, :2 * w] + bias_ref[pr]
        if sb == 0:
            st = jnp.where(keep, st, NEG_INF)
        sink = jnp.where(col < w, sink_ref[2 * pr], sink_ref[2 * pr + 1])
        m = jnp.maximum(jnp.max(st, axis=0, keepdims=True), sink)
        pt = jnp.exp2(st - m).astype(BF16)
        acc = _dot(vall[g * VT_ROWS:(g + 1) * VT_ROWS, sb * w:(sb + 2) * w], pt)
        denom = acc[HEAD_DIM:HEAD_DIM + 1] + jnp.exp2(sink - m)
        o = acc[:HEAD_DIM] / denom
        oo = jnp.concatenate([o[:, :w], o[:, w:]], axis=0)
        o_ref[sb * w:(sb + 1) * w, pr * LANES:(pr + 1) * LANES] = oo.T.astype(BF16)


def _swa_call(sinks, qb, kbd, vbt, bias_t):
    s = qb.shape[0]
    w = SWA_WINDOW
    n = SWA_STEP_BLOCKS
    assert s % (n * w) == 0
    prev_blk = lambda i: jnp.maximum(n * i - 1, 0)
    grid_spec = pltpu.PrefetchScalarGridSpec(
        num_scalar_prefetch=1,
        grid=(s // (n * w),),
        in_specs=[
            pl.BlockSpec((n * w, _W_QB), lambda i, c: (i, 0)),
            pl.BlockSpec((w, _W_KB), lambda i, c: (prev_blk(i), 0)),
            pl.BlockSpec((n * w, _W_KB), lambda i, c: (i, 0)),
            pl.BlockSpec((SWA_KV_HEADS * VT_ROWS, w), lambda i, c: (0, prev_blk(i))),
            pl.BlockSpec((SWA_KV_HEADS * VT_ROWS, n * w), lambda i, c: (0, i)),
            pl.BlockSpec(bias_t.shape, lambda i, c: (0, 0, 0)),
        ],
        out_specs=pl.BlockSpec((n * w, _W_QB), lambda i, c: (i, 0)),
        scratch_shapes=[pltpu.VMEM((n * SWA_Q_HEADS // 2, 2 * w, 2 * w + LANES), F32)],
    )
    return pl.pallas_call(
        _swa_kernel,
        grid_spec=grid_spec,
        out_shape=jax.ShapeDtypeStruct((s, _W_QB), BF16),
        compiler_params=pltpu.CompilerParams(dimension_semantics=("arbitrary",), vmem_limit_bytes=VMEM_LIMIT),
        name="swa",
    )(sinks, qb, kbd, kbd, vbt, vbt, bias_t)


def _post_kernel(x_ref, ya_ref, yb_ref, ga_ref, gb_ref, wa_ref, wb_ref, wo_ref, gffn_ref, w1_ref, w2_ref, o_ref,
                 *, tf):
    merged = ga_ref[...] * _dot(ya_ref[...], wa_ref[...]) + gb_ref[...] * _dot(yb_ref[...], wb_ref[...])
    xn = x_ref[...] + _dot(merged.astype(BF16), wo_ref[...])
    h = _rms_norm_rows(xn, gffn_ref[...]).astype(BF16)
    acc = xn
    for c in range(w1_ref.shape[1] // tf):
        u = jnp.maximum(_dot(h, w1_ref[:, c * tf:(c + 1) * tf]), 0.0)
        acc = acc + _dot((u * u).astype(BF16), w2_ref[c * tf:(c + 1) * tf, :])
    o_ref[...] = acc


def _post_call(l, x, ya, yb, ga, gb, wa, wb, wo, gffn, w1, w2, tm=512, tf=1024):
    s, d = x.shape
    full = lambda a: _layer_spec(a, l, pipeline_mode=pl.Buffered(1))
    rows = lambda a: pl.BlockSpec((tm, a.shape[1]), lambda i: (i, 0))
    return pl.pallas_call(
        functools.partial(_post_kernel, tf=tf),
        grid=(s // tm,),
        in_specs=[rows(x), rows(ya), rows(yb), rows(ga), rows(gb),
                  full(wa), full(wb), full(wo), full(gffn), full(w1), full(w2)],
        out_specs=pl.BlockSpec((tm, d), lambda i: (i, 0)),
        out_shape=jax.ShapeDtypeStruct((s, d), F32),
        compiler_params=pltpu.CompilerParams(dimension_semantics=("arbitrary",), vmem_limit_bytes=VMEM_LIMIT),
        name="post",
    )(x, ya, yb, ga, gb, wa, wb, wo, gffn, w1, w2)


def _bias_by_distance(table, lo, hi, keep):
    dist = np.arange(lo, hi)
    vals = jnp.take(table, jnp.asarray(_rel_bucket_np(dist).astype(np.int32)), axis=0).T
    return jnp.where(jnp.asarray(keep(dist))[None, :], vals, NEG_INF)


def _toeplitz(w, n):
    lead = w.shape[:-1]
    w_pad = jnp.concatenate([w, jnp.zeros(lead + (1,), w.dtype)], axis=-1)
    skew = jnp.tile(w_pad, (1,) * len(lead) + (n,))[..., :n * (2 * n - 1)].reshape(lead + (n, 2 * n - 1))
    return skew[..., n - 1:]


def _bias_tables(rel_bias):
    blk, w = MOBA_BLOCK, SWA_WINDOW
    vec = _bias_by_distance(rel_bias[:, :MOBA_HEADS], 1 - blk, MOBA_NEAR * blk, lambda dist: dist >= 0)
    vec = jnp.concatenate([vec, jnp.zeros((MOBA_HEADS, 1), F32)], axis=1)
    moba = jnp.stack([jnp.roll(vec[:, d * blk:(d + 2) * blk], 1 - blk, axis=1) for d in range(MOBA_NEAR)])
    moba = moba.reshape(MOBA_NEAR, HEAD_PAIRS, 2, 2 * blk)
    cfar = rel_bias[N_REL_BUCKETS - 1, :MOBA_HEADS]

    vec = _bias_by_distance(rel_bias[:, MOBA_HEADS:], 1 - w, 3 * w, lambda dist: (dist >= 0) & (dist < w))
    swa = _toeplitz(vec, 2 * w)[:, :, :w]
    swa = swa.reshape(SWA_Q_HEADS // 2, 2, 2 * w, w).transpose(0, 2, 1, 3).reshape(SWA_Q_HEADS // 2, 2 * w, 2 * w)
    return (moba * LOG2E).astype(F32), (cfar * LOG2E).astype(F32), (swa * LOG2E).astype(F32)


def kernel(x, rel_bias, g_mix, w_in, b_gate, q_norm_a, k_norm_a, q_norm_b, k_norm_b, sinks,
           w_branch_a, w_branch_b, w_out, g_ffn, w_ff1, w_ff2):
    b, s, d = x.shape
    assert b == 1 and s % MOBA_BLOCK == 0 and s // MOBA_BLOCK >= MOBA_NEAR
    depth = w_in.shape[0]
    bias_moba, cfar, bias_swa = _bias_tables(rel_bias)
    row = lambda a: a[:, None, :]
    gains = lambda g, width: row(jnp.tile(g, (1, width // HEAD_DIM)))
    w_in_b, wa_b, wb_b, wo_b, w1_b, w2_b = (a.astype(BF16) for a in (w_in, w_branch_a, w_branch_b, w_out, w_ff1, w_ff2))
    gqa, gka = gains(q_norm_a, _W_QA), gains(k_norm_a, _W_KA)
    gqb, gkb = gains(q_norm_b, _W_QB), gains(k_norm_b, SWA_KV_HEADS * HEAD_DIM)
    sinks2 = sinks * LOG2E
    xs = x[0]
    for l in range(depth):
        qa, ka, kmean, vat, qb, kbd, vbt, ga, gb = _proj_call(l, xs, row(g_mix), w_in_b, row(b_gate), gqa, gka, gqb, gkb)
        ya = _moba_call(cfar, qa, ka, vat, kmean.reshape(kmean.shape[0], kmean.shape[2]), bias_moba)
        yb = _swa_call(sinks2[l], qb, kbd, vbt, bias_swa)
        xs = _post_call(l, xs, ya, yb, ga, gb, wa_b, wb_b, wo_b, row(g_ffn), w1_b, w2_b)
    return xs[None]
```

```python
import functools

import jax
import jax.numpy as jnp
import numpy as np
from jax import lax
from jax.experimental import pallas as pl
from jax.experimental.pallas import tpu as pltpu

HEAD_DIM = 64
MOBA_HEADS = 8
MOBA_BLOCK = 256
MOBA_TOPK = 3
SWA_Q_HEADS = 8
SWA_KV_HEADS = 2
SWA_WINDOW = 128
SWA_STEP_BLOCKS = 8
SWA_SCORES_AHEAD = 4
PROJ_STEP_BLOCKS = 4
N_REL_BUCKETS = 32
REL_MAX_DISTANCE = 2048
NORM_EPS = 1e-6
LOG2E = 1.4426950408889634
ATTN_SCALE = HEAD_DIM ** -0.5 * LOG2E

LANES = 128
HEAD_PAIRS = MOBA_HEADS // 2
VT_ROWS = 80
MOBA_FAR_UNROLL = 2
MOBA_LOOP_GROUPS = 8
assert MOBA_LOOP_GROUPS % 2 == 0
VMEM_LIMIT = 56 * 1024 * 1024

BF16 = jnp.bfloat16
F32 = jnp.float32
NEG_INF = float("-inf")
M_INIT = -1e30


def _rel_bucket_np(dist):
    n = np.maximum(dist, 0)
    exact = N_REL_BUCKETS // 2
    nf = np.maximum(n, 1).astype(np.float32)
    large = exact + (np.log(nf / np.float32(exact)) / np.float32(np.log(REL_MAX_DISTANCE / exact))
                     * np.float32(N_REL_BUCKETS - exact)).astype(np.int32)
    large = np.minimum(large, N_REL_BUCKETS - 1)
    return np.where(n < exact, n, large)


def _moba_near_blocks():
    dist = np.arange(1, 2 * REL_MAX_DISTANCE)
    first_last = int(dist[_rel_bucket_np(dist) == N_REL_BUCKETS - 1][0])
    return -(-(first_last - 1) // MOBA_BLOCK) + 1


MOBA_NEAR = _moba_near_blocks()
assert 2 <= MOBA_FAR_UNROLL < MOBA_NEAR


def _nt_dot(a, b):
    return lax.dot_general(a, b, (((1,), (1,)), ((), ())), preferred_element_type=F32)


def _dot(a, b):
    return jnp.dot(a, b, preferred_element_type=F32)


def _rms_norm_rows(xf, g):
    ms = jnp.mean(xf * xf, axis=-1, keepdims=True)
    return xf * lax.rsqrt(ms + NORM_EPS) * g


def _head_norm(acc, g2):
    w = acc.shape[1]
    lane = lax.broadcasted_iota(jnp.int32, (1, LANES), 1)
    lo = lane < HEAD_DIM
    outs = []
    for b in range(w // LANES):
        xb = acc[:, b * LANES:(b + 1) * LANES]
        x2 = xb * xb
        s_lo = jnp.sum(jnp.where(lo, x2, 0.0), axis=-1, keepdims=True)
        s_hi = jnp.sum(jnp.where(lo, 0.0, x2), axis=-1, keepdims=True)
        r = jnp.where(lo, lax.rsqrt(s_lo / HEAD_DIM + NORM_EPS), lax.rsqrt(s_hi / HEAD_DIM + NORM_EPS))
        outs.append(xb * r * g2[:, b * LANES:(b + 1) * LANES])
    return outs[0] if len(outs) == 1 else jnp.concatenate(outs, axis=1)


def _v_transposed(v, n_heads):
    rows = v.shape[0]
    vt = v.T
    r = lax.broadcasted_iota(jnp.int32, (VT_ROWS - HEAD_DIM, rows), 0)
    aug = jnp.where(r == 0, 1.0, 0.0).astype(F32)
    parts = []
    for h in range(n_heads):
        parts.append(vt[h * HEAD_DIM:(h + 1) * HEAD_DIM])
        parts.append(aug)
    return jnp.concatenate(parts, axis=0).astype(BF16)


_W_QA, _W_KA, _W_VA, _W_QB, _W_KB, _W_VB, _W_GA, _W_GB = 512, 512, 512, 512, 256, 128, 1024, 1024
_PROJ_COLS = np.cumsum([0, _W_QA, _W_KA, _W_VA, _W_QB, SWA_KV_HEADS * HEAD_DIM, _W_VB, _W_GA, _W_GB]).tolist()


def _proj_kernel(x_ref, gmix_ref, w_ref, bgate_ref, gqa_ref, gka_ref, gqb_ref, gkb_ref,
                 qa_ref, ka_ref, kmean_ref, vat_ref, qb_ref, kb_ref, vbt_ref, ga_ref, gb_ref):
    h = _rms_norm_rows(x_ref[...], gmix_ref[...]).astype(BF16)
    c = _PROJ_COLS

    def seg(k):
        return _dot(h, w_ref[:, c[k]:c[k + 1]])

    qa_ref[...] = (_head_norm(seg(0), gqa_ref[...]) * ATTN_SCALE).astype(BF16)
    ka = _head_norm(seg(1), gka_ref[...])
    ka_ref[...] = ka.astype(BF16)
    for b in range(PROJ_STEP_BLOCKS):
        kmean_ref[b] = jnp.mean(ka[b * MOBA_BLOCK:(b + 1) * MOBA_BLOCK], axis=0, keepdims=True)
    vat_ref[...] = _v_transposed(seg(2), MOBA_HEADS)
    qb_ref[...] = (_head_norm(seg(3), gqb_ref[...]) * ATTN_SCALE).astype(BF16)
    kb = _head_norm(seg(4), gkb_ref[...])
    kb_swapped = pltpu.roll(kb, HEAD_DIM, axis=1)
    lo = lax.broadcasted_iota(jnp.int32, (1, LANES), 1) < HEAD_DIM
    kb_ref[...] = jnp.concatenate([jnp.where(lo, kb, kb_swapped), jnp.where(lo, kb_swapped, kb)], axis=1).astype(BF16)
    vbt_ref[...] = _v_transposed(seg(5), SWA_KV_HEADS)
    bg = bgate_ref[...]
    d = _W_GA
    ga_ref[...] = 1.0 / (1.0 + jnp.exp(-(seg(6) + bg[:, :d])))
    gb_ref[...] = 1.0 / (1.0 + jnp.exp(-(seg(7) + bg[:, d:])))


def _layer_spec(a, l, **kw):
    return pl.BlockSpec((None,) + a.shape[1:], lambda i: (l,) + (0,) * (a.ndim - 1), **kw)


def _proj_call(l, x, gmix, w, bgate, gqa, gka, gqb, gkb):
    s, d = x.shape
    tm = PROJ_STEP_BLOCKS * MOBA_BLOCK
    assert s % tm == 0
    nblk = s // MOBA_BLOCK
    full = lambda a: _layer_spec(a, l, pipeline_mode=pl.Buffered(1))
    rows = lambda width: pl.BlockSpec((tm, width), lambda i: (i, 0))
    cols = lambda height: pl.BlockSpec((height, tm), lambda i: (0, i))
    out_shape = (
        jax.ShapeDtypeStruct((s, _W_QA), BF16),
        jax.ShapeDtypeStruct((s, _W_KA), BF16),
        jax.ShapeDtypeStruct((nblk, 1, _W_KA), F32),
        jax.ShapeDtypeStruct((MOBA_HEADS * VT_ROWS, s), BF16),
        jax.ShapeDtypeStruct((s, _W_QB), BF16),
        jax.ShapeDtypeStruct((s, _W_KB), BF16),
        jax.ShapeDtypeStruct((SWA_KV_HEADS * VT_ROWS, s), BF16),
        jax.ShapeDtypeStruct((s, _W_GA), F32),
        jax.ShapeDtypeStruct((s, _W_GB), F32),
    )
    out_specs = (
        rows(_W_QA), rows(_W_KA), pl.BlockSpec((PROJ_STEP_BLOCKS, 1, _W_KA), lambda i: (i, 0, 0)),
        cols(MOBA_HEADS * VT_ROWS), rows(_W_QB), rows(_W_KB), cols(SWA_KV_HEADS * VT_ROWS),
        rows(_W_GA), rows(_W_GB),
    )
    return pl.pallas_call(
        _proj_kernel,
        grid=(s // tm,),
        in_specs=[rows(d), full(gmix), full(w), full(bgate), full(gqa), full(gka), full(gqb), full(gkb)],
        out_specs=out_specs,
        out_shape=out_shape,
        compiler_params=pltpu.CompilerParams(dimension_semantics=("arbitrary",), vmem_limit_bytes=VMEM_LIMIT),
        name="proj",
    )(x, gmix, w, bgate, gqa, gka, gqb, gkb)


def _moba_kernel(cfar_ref, q_ref, k_ref, vt_ref, kmean_ref, diag_ref, o_ref,
                 qq_scr, sel_scr, m_scr, acc_scr, sta_scr, stb_scr, bias_ref, *, nblk):
    p = pl.program_id(0)
    i = pl.program_id(1)
    blk = MOBA_BLOCK

    @pl.when(i == 0)
    def _():
        for d in range(MOBA_NEAR):
            for hh in range(2):
                skew = pltpu.roll(jnp.broadcast_to(diag_ref[d, hh:hh + 1, :], (blk, 2 * blk)), 0, 1,
                                  stride=1, stride_axis=0)
                bias_ref[d, :, hh * blk:(hh + 1) * blk] = skew[:, :blk]

    q = q_ref[...]
    lane = lax.broadcasted_iota(jnp.int32, (1, LANES), 1)
    km = kmean_ref[...]
    km_hi = km.astype(BF16)
    km_lo = (km - km_hi.astype(F32)).astype(BF16)
    n_iota = lax.broadcasted_iota(jnp.int32, (nblk, 2 * blk), 0)

    qq = jnp.concatenate([jnp.where(lane < HEAD_DIM, q, jnp.zeros_like(q)),
                          jnp.where(lane >= HEAD_DIM, q, jnp.zeros_like(q))], axis=0)
    qq_scr[...] = qq

    gate2 = _nt_dot(jnp.concatenate([km_hi, km_lo], axis=0), qq)
    gates = gate2[:nblk] + gate2[nblk:]

    def select_blocks():
        g = jnp.where(n_iota < i, gates, NEG_INF)
        sel = jnp.zeros((nblk, 2 * blk), F32)
        for _ in range(MOBA_TOPK):
            mx = jnp.max(g, axis=0, keepdims=True)
            idx = jnp.min(jnp.where(g == mx, n_iota, nblk), axis=0, keepdims=True)
            hit = n_iota == idx
            valid = jnp.where(mx > NEG_INF, 1.0, 0.0)
            sel = jnp.maximum(sel, jnp.where(hit, valid, 0.0))
            g = jnp.where(hit, NEG_INF, g)
        sel_scr[...] = jnp.where(n_iota == i, 1.0, sel)
        acc_scr[...] = jnp.zeros((2, VT_ROWS, blk), F32)
        m_scr[...] = jnp.full((1, 2 * blk), M_INIT, F32)

    col = lax.broadcasted_iota(jnp.int32, (1, 2 * blk), 1)
    cfar = jnp.where(col < blk, cfar_ref[2 * p], cfar_ref[2 * p + 1])
    n_far = jnp.maximum(i - (MOBA_NEAR - 1), 0)

    def scores(first, nb, st_ref):
        off = pl.multiple_of(first * blk, blk)
        st_ref[:nb * blk, :2 * blk] = _nt_dot(k_ref[pl.ds(off, nb * blk), :], qq_scr[...])

    def attend(first, nb, st_ref, near):
        ms, pvs = [], []
        for u in range(nb):
            j = first + u
            s_u = st_ref[u * blk:(u + 1) * blk, :2 * blk]
            if near:
                s_u = s_u + bias_ref[jnp.clip(i - j, 0, MOBA_NEAR - 1)]
                on = sel_scr[pl.ds(j, 1), :] > 0.5
                m_u = jnp.max(s_u, axis=0, keepdims=True)
                ms.append(jnp.where(on, m_u, NEG_INF))
            else:
                on = sel_scr[pl.ds(j, 1), :] > jnp.where(j < n_far, 0.5, 2.0)
                m_u = jnp.max(s_u, axis=0, keepdims=True)
                ms.append(jnp.where(on, m_u + cfar, NEG_INF))
            pt = jnp.exp2(s_u - m_u).astype(BF16)
            ko = pl.multiple_of(j * blk, blk)
            pvs.append([_dot(vt_ref[hh * VT_ROWS:(hh + 1) * VT_ROWS, pl.ds(ko, blk)], pt[:, hh * blk:(hh + 1) * blk])
                        for hh in range(2)])
        m_old = m_scr[...]
        m_new = m_old
        for m_u in ms:
            m_new = jnp.maximum(m_new, m_u)
        alpha = jnp.exp2(m_old - m_new)
        ws = [jnp.exp2(m_u - m_new) for m_u in ms]
        for hh in range(2):
            cols = slice(hh * blk, (hh + 1) * blk)
            acc = acc_scr[hh] * alpha[:, cols]
            for w_u, pv in zip(ws, pvs):
                acc = acc + pv[hh] * w_u[:, cols]
            acc_scr[hh] = acc
        m_scr[...] = m_new

    fu = MOBA_FAR_UNROLL
    n_groups = (n_far + fu - 1) // fu
    near0 = jnp.maximum(i - (MOBA_NEAR - 1), 0)
    bufs = (sta_scr, stb_scr)

    def item_start(g):
        return jnp.where(g < n_groups, g * fu, near0)

    def run_items(items, then_first=None):
        for k, (first, nb, near) in enumerate(items):
            if k + 1 < len(items):
                scores(items[k + 1][0], items[k + 1][1], bufs[(k + 1) % 2])
            elif then_first is not None:
                scores(then_first, fu, bufs[(k + 1) % 2])
            attend(first, nb, bufs[k % 2], near)

    near_sizes = [fu] + [2] * ((MOBA_NEAR - fu - 1) // 2) + [1] * (1 + (MOBA_NEAR - fu - 1) % 2)
    near_items = [(near0 + int(st), nb, True) for st, nb in zip(np.cumsum([0] + near_sizes[:-1]), near_sizes)]

    scores(item_start(0), fu, sta_scr)
    select_blocks()

    unroll = MOBA_LOOP_GROUPS

    def loop_body(t, carry):
        g0 = t * unroll
        run_items([((g0 + k) * fu, fu, False) for k in range(unroll)], then_first=item_start(g0 + unroll))
        return carry

    lax.fori_loop(0, n_groups // unroll, loop_body, 0)

    g0 = n_groups // unroll * unroll
    for rem in range(unroll):

        @pl.when(n_groups - g0 == rem)
        def _():
            run_items([((g0 + k) * fu, fu, False) for k in range(rem)] + near_items)

    outs = []
    for hh in range(2):
        a = acc_scr[hh]
        outs.append(a[:HEAD_DIM] / a[HEAD_DIM:HEAD_DIM + 1])
    o_ref[...] = jnp.concatenate(outs, axis=0).T.astype(BF16)


def _moba_call(cfar, qa, ka, vat, kmean, bias_t):
    s = qa.shape[0]
    blk = MOBA_BLOCK
    nblk = s // blk
    grid_spec = pltpu.PrefetchScalarGridSpec(
        num_scalar_prefetch=1,
        grid=(HEAD_PAIRS, nblk),
        in_specs=[
            pl.BlockSpec((blk, LANES), lambda p, i, c: (i, p)),
            pl.BlockSpec((s, LANES), lambda p, i, c: (0, p)),
            pl.BlockSpec((2 * VT_ROWS, s), lambda p, i, c: (p, 0)),
            pl.BlockSpec((nblk, LANES), lambda p, i, c: (0, p)),
            pl.BlockSpec((MOBA_NEAR, None, 2, 2 * blk), lambda p, i, c: (0, p, 0, 0)),
        ],
        out_specs=pl.BlockSpec((blk, LANES), lambda p, i, c: (i, p)),
        scratch_shapes=[
            pltpu.VMEM((2 * blk, LANES), BF16),
            pltpu.VMEM((nblk, 2 * blk), F32),
            pltpu.VMEM((1, 2 * blk), F32),
            pltpu.VMEM((2, VT_ROWS, blk), F32),
            pltpu.VMEM((MOBA_FAR_UNROLL * blk, 2 * blk + LANES), F32),
            pltpu.VMEM((MOBA_FAR_UNROLL * blk, 2 * blk + LANES), F32),
            pltpu.VMEM((MOBA_NEAR, blk, 2 * blk), F32),
        ],
    )
    return pl.pallas_call(
        functools.partial(_moba_kernel, nblk=nblk),
        grid_spec=grid_spec,
        out_shape=jax.ShapeDtypeStruct((s, MOBA_HEADS * HEAD_DIM), BF16),
        compiler_params=pltpu.CompilerParams(dimension_semantics=("arbitrary", "arbitrary"),
                                             vmem_limit_bytes=VMEM_LIMIT),
        name="moba",
    )(cfar, qa, ka, vat, kmean, bias_t)


def _swa_kernel(sink_ref, q_ref, kprev_ref, kcur_ref, vprev_ref, vcur_ref, bias_ref, o_ref, st_scr):
    b = pl.program_id(0)
    w = SWA_WINDOW
    kall = jnp.concatenate([kprev_ref[...], kcur_ref[...]], axis=0)
    vall = jnp.concatenate([vprev_ref[...], vcur_ref[...]], axis=1)
    row = lax.broadcasted_iota(jnp.int32, (2 * w, 1), 0)
    keep = row >= jnp.where(b > 0, 0, w)
    lane = lax.broadcasted_iota(jnp.int32, (1, LANES), 1)
    col = lax.broadcasted_iota(jnp.int32, (1, 2 * w), 1)
    group = SWA_Q_HEADS // SWA_KV_HEADS
    chains = [(sb, pr) for sb in range(SWA_STEP_BLOCKS) for pr in range(SWA_Q_HEADS // 2)]
    def scores(c):
        sb, pr = chains[c]
        g = (2 * pr) // group
        qp = q_ref[sb * w:(sb + 1) * w, pr * LANES:(pr + 1) * LANES]
        zero = jnp.zeros_like(qp)
        qq = jnp.concatenate([jnp.where(lane < HEAD_DIM, qp, zero),
                              jnp.where(lane >= HEAD_DIM, qp, zero)], axis=0)
        kd = kall[sb * w:(sb + 2) * w, g * LANES:(g + 1) * LANES]
        st_scr[c, :, :2 * w] = _nt_dot(kd, qq)

    ahead = SWA_SCORES_AHEAD
    for c in range(min(ahead, len(chains))):
        scores(c)
    for c, (sb, pr) in enumerate(chains):
        if c + ahead < len(chains):
            scores(c + ahead)
        g = (2 * pr) // group
        st = st_scr[c, :, :2 * w] + bias_ref[pr]
        if sb == 0:
            st = jnp.where(keep, st, NEG_INF)
        sink = jnp.where(col < w, sink_ref[2 * pr], sink_ref[2 * pr + 1])
        m = jnp.maximum(jnp.max(st, axis=0, keepdims=True), sink)
        pt = jnp.exp2(st - m).astype(BF16)
        acc = _dot(vall[g * VT_ROWS:(g + 1) * VT_ROWS, sb * w:(sb + 2) * w], pt)
        denom = acc[HEAD_DIM:HEAD_DIM + 1] + jnp.exp2(sink - m)
        o = acc[:HEAD_DIM] / denom
        oo = jnp.concatenate([o[:, :w], o[:, w:]], axis=0)
        o_ref[sb * w:(sb + 1) * w, pr * LANES:(pr + 1) * LANES] = oo.T.astype(BF16)


def _swa_call(sinks, qb, kbd, vbt, bias_t):
    s = qb.shape[0]
    w = SWA_WINDOW
    n = SWA_STEP_BLOCKS
    assert s % (n * w) == 0
    prev_blk = lambda i: jnp.maximum(n * i - 1, 0)
    grid_spec = pltpu.PrefetchScalarGridSpec(
        num_scalar_prefetch=1,
        grid=(s // (n * w),),
        in_specs=[
            pl.BlockSpec((n * w, _W_QB), lambda i, c: (i, 0)),
            pl.BlockSpec((w, _W_KB), lambda i, c: (prev_blk(i), 0)),
            pl.BlockSpec((n * w, _W_KB), lambda i, c: (i, 0)),
            pl.BlockSpec((SWA_KV_HEADS * VT_ROWS, w), lambda i, c: (0, prev_blk(i))),
            pl.BlockSpec((SWA_KV_HEADS * VT_ROWS, n * w), lambda i, c: (0, i)),
            pl.BlockSpec(bias_t.shape, lambda i, c: (0, 0, 0)),
        ],
        out_specs=pl.BlockSpec((n * w, _W_QB), lambda i, c: (i, 0)),
        scratch_shapes=[pltpu.VMEM((n * SWA_Q_HEADS // 2, 2 * w, 2 * w + LANES), F32)],
    )
    return pl.pallas_call(
        _swa_kernel,
        grid_spec=grid_spec,
        out_shape=jax.ShapeDtypeStruct((s, _W_QB), BF16),
        compiler_params=pltpu.CompilerParams(dimension_semantics=("arbitrary",), vmem_limit_bytes=VMEM_LIMIT),
        name="swa",
    )(sinks, qb, kbd, kbd, vbt, vbt, bias_t)


def _post_kernel(x_ref, ya_ref, yb_ref, ga_ref, gb_ref, wa_ref, wb_ref, wo_ref, gffn_ref, w1_ref, w2_ref, o_ref,
                 *, tf):
    merged = ga_ref[...] * _dot(ya_ref[...], wa_ref[...]) + gb_ref[...] * _dot(yb_ref[...], wb_ref[...])
    xn = x_ref[...] + _dot(merged.astype(BF16), wo_ref[...])
    h = _rms_norm_rows(xn, gffn_ref[...]).astype(BF16)
    acc = xn
    for c in range(w1_ref.shape[1] // tf):
        u = jnp.maximum(_dot(h, w1_ref[:, c * tf:(c + 1) * tf]), 0.0)
        acc = acc + _dot((u * u).astype(BF16), w2_ref[c * tf:(c + 1) * tf, :])
    o_ref[...] = acc


def _post_call(l, x, ya, yb, ga, gb, wa, wb, wo, gffn, w1, w2, tm=512, tf=1024):
    s, d = x.shape
    full = lambda a: _layer_spec(a, l, pipeline_mode=pl.Buffered(1))
    rows = lambda a: pl.BlockSpec((tm, a.shape[1]), lambda i: (i, 0))
    return pl.pallas_call(
        functools.partial(_post_kernel, tf=tf),
        grid=(s // tm,),
        in_specs=[rows(x), rows(ya), rows(yb), rows(ga), rows(gb),
                  full(wa), full(wb), full(wo), full(gffn), full(w1), full(w2)],
        out_specs=pl.BlockSpec((tm, d), lambda i: (i, 0)),
        out_shape=jax.ShapeDtypeStruct((s, d), F32),
        compiler_params=pltpu.CompilerParams(dimension_semantics=("arbitrary",), vmem_limit_bytes=VMEM_LIMIT),
        name="post",
    )(x, ya, yb, ga, gb, wa, wb, wo, gffn, w1, w2)


def _bias_by_distance(table, lo, hi, keep):
    dist = np.arange(lo, hi)
    vals = jnp.take(table, jnp.asarray(_rel_bucket_np(dist).astype(np.int32)), axis=0).T
    return jnp.where(jnp.asarray(keep(dist))[None, :], vals, NEG_INF)


def _toeplitz(w, n):
    lead = w.shape[:-1]
    w_pad = jnp.concatenate([w, jnp.zeros(lead + (1,), w.dtype)], axis=-1)
    skew = jnp.tile(w_pad, (1,) * len(lead) + (n,))[..., :n * (2 * n - 1)].reshape(lead + (n, 2 * n - 1))
    return skew[..., n - 1:]


def _bias_tables(rel_bias):
    blk, w = MOBA_BLOCK, SWA_WINDOW
    vec = _bias_by_distance(rel_bias[:, :MOBA_HEADS], 1 - blk, MOBA_NEAR * blk, lambda dist: dist >= 0)
    vec = jnp.concatenate([vec, jnp.zeros((MOBA_HEADS, 1), F32)], axis=1)
    moba = jnp.stack([jnp.roll(vec[:, d * blk:(d + 2) * blk], 1 - blk, axis=1) for d in range(MOBA_NEAR)])
    moba = moba.reshape(MOBA_NEAR, HEAD_PAIRS, 2, 2 * blk)
    cfar = rel_bias[N_REL_BUCKETS - 1, :MOBA_HEADS]

    vec = _bias_by_distance(rel_bias[:, MOBA_HEADS:], 1 - w, 3 * w, lambda dist: (dist >= 0) & (dist < w))
    swa = _toeplitz(vec, 2 * w)[:, :, :w]
    swa = swa.reshape(SWA_Q_HEADS // 2, 2, 2 * w, w).transpose(0, 2, 1, 3).reshape(SWA_Q_HEADS // 2, 2 * w, 2 * w)
    return (moba * LOG2E).astype(F32), (cfar * LOG2E).astype(F32), (swa * LOG2E).astype(F32)


def kernel(x, rel_bias, g_mix, w_in, b_gate, q_norm_a, k_norm_a, q_norm_b, k_norm_b, sinks,
           w_branch_a, w_branch_b, w_out, g_ffn, w_ff1, w_ff2):
    b, s, d = x.shape
    assert b == 1 and s % MOBA_BLOCK == 0 and s // MOBA_BLOCK >= MOBA_NEAR
    depth = w_in.shape[0]
    bias_moba, cfar, bias_swa = _bias_tables(rel_bias)
    row = lambda a: a[:, None, :]
    gains = lambda g, width: row(jnp.tile(g, (1, width // HEAD_DIM)))
    w_in_b, wa_b, wb_b, wo_b, w1_b, w2_b = (a.astype(BF16) for a in (w_in, w_branch_a, w_branch_b, w_out, w_ff1, w_ff2))
    gqa, gka = gains(q_norm_a, _W_QA), gains(k_norm_a, _W_KA)
    gqb, gkb = gains(q_norm_b, _W_QB), gains(k_norm_b, SWA_KV_HEADS * HEAD_DIM)
    sinks2 = sinks * LOG2E
    xs = x[0]
    for l in range(depth):
        qa, ka, kmean, vat, qb, kbd, vbt, ga, gb = _proj_call(l, xs, row(g_mix), w_in_b, row(b_gate), gqa, gka, gqb, gkb)
        ya = _moba_call(cfar, qa, ka, vat, kmean.reshape(kmean.shape[0], kmean.shape[2]), bias_moba)
        yb = _swa_call(sinks2[l], qb, kbd, vbt, bias_swa)
        xs = _post_call(l, xs, ya, yb, ga, gb, wa_b, wb_b, wo_b, row(g_ffn), w1_b, w2_b)
    return xs[None]
```

```python
import functools

import jax
import jax.numpy as jnp
import numpy as np
from jax import lax
from jax.experimental import pallas as pl
from jax.experimental.pallas import tpu as pltpu

HEAD_DIM = 64
MOBA_HEADS = 8
MOBA_BLOCK = 256
MOBA_TOPK = 3
SWA_Q_HEADS = 8
SWA_KV_HEADS = 2
SWA_WINDOW = 128
SWA_STEP_BLOCKS = 8
SWA_SCORES_AHEAD = 4
PROJ_STEP_BLOCKS = 4
N_REL_BUCKETS = 32
REL_MAX_DISTANCE = 2048
NORM_EPS = 1e-6
LOG2E = 1.4426950408889634
ATTN_SCALE = HEAD_DIM ** -0.5 * LOG2E

LANES = 128
HEAD_PAIRS = MOBA_HEADS // 2
VT_ROWS = 80
MOBA_FAR_UNROLL = 2
MOBA_LOOP_GROUPS = 4
assert MOBA_LOOP_GROUPS % 2 == 0
VMEM_LIMIT = 56 * 1024 * 1024

BF16 = jnp.bfloat16
F32 = jnp.float32
NEG_INF = float("-inf")
M_INIT = -1e30


def _rel_bucket_np(dist):
    n = np.maximum(dist, 0)
    exact = N_REL_BUCKETS // 2
    nf = np.maximum(n, 1).astype(np.float32)
    large = exact + (np.log(nf / np.float32(exact)) / np.float32(np.log(REL_MAX_DISTANCE / exact))
                     * np.float32(N_REL_BUCKETS - exact)).astype(np.int32)
    large = np.minimum(large, N_REL_BUCKETS - 1)
    return np.where(n < exact, n, large)


def _moba_near_blocks():
    dist = np.arange(1, 2 * REL_MAX_DISTANCE)
    first_last = int(dist[_rel_bucket_np(dist) == N_REL_BUCKETS - 1][0])
    return -(-(first_last - 1) // MOBA_BLOCK) + 1


MOBA_NEAR = _moba_near_blocks()
assert 2 <= MOBA_FAR_UNROLL < MOBA_NEAR


def _nt_dot(a, b):
    return lax.dot_general(a, b, (((1,), (1,)), ((), ())), preferred_element_type=F32)


def _dot(a, b):
    return jnp.dot(a, b, preferred_element_type=F32)


def _rms_norm_rows(xf, g):
    ms = jnp.mean(xf * xf, axis=-1, keepdims=True)
    return xf * lax.rsqrt(ms + NORM_EPS) * g


def _head_norm(acc, g2):
    w = acc.shape[1]
    lane = lax.broadcasted_iota(jnp.int32, (1, LANES), 1)
    lo = lane < HEAD_DIM
    outs = []
    for b in range(w // LANES):
        xb = acc[:, b * LANES:(b + 1) * LANES]
        x2 = xb * xb
        s_lo = jnp.sum(jnp.where(lo, x2, 0.0), axis=-1, keepdims=True)
        s_hi = jnp.sum(jnp.where(lo, 0.0, x2), axis=-1, keepdims=True)
        r = jnp.where(lo, lax.rsqrt(s_lo / HEAD_DIM + NORM_EPS), lax.rsqrt(s_hi / HEAD_DIM + NORM_EPS))
        outs.append(xb * r * g2[:, b * LANES:(b + 1) * LANES])
    return outs[0] if len(outs) == 1 else jnp.concatenate(outs, axis=1)


def _v_transposed(v, n_heads):
    rows = v.shape[0]
    vt = v.T
    r = lax.broadcasted_iota(jnp.int32, (VT_ROWS - HEAD_DIM, rows), 0)
    aug = jnp.where(r == 0, 1.0, 0.0).astype(F32)
    parts = []
    for h in range(n_heads):
        parts.append(vt[h * HEAD_DIM:(h + 1) * HEAD_DIM])
        parts.append(aug)
    return jnp.concatenate(parts, axis=0).astype(BF16)


_W_QA, _W_KA, _W_VA, _W_QB, _W_KB, _W_VB, _W_GA, _W_GB = 512, 512, 512, 512, 256, 128, 1024, 1024
_PROJ_COLS = np.cumsum([0, _W_QA, _W_KA, _W_VA, _W_QB, SWA_KV_HEADS * HEAD_DIM, _W_VB, _W_GA, _W_GB]).tolist()


def _proj_kernel(x_ref, gmix_ref, w_ref, bgate_ref, gqa_ref, gka_ref, gqb_ref, gkb_ref,
                 qa_ref, ka_ref, kmean_ref, vat_ref, qb_ref, kb_ref, vbt_ref, ga_ref, gb_ref):
    h = _rms_norm_rows(x_ref[...], gmix_ref[...]).astype(BF16)
    c = _PROJ_COLS

    def seg(k):
        return _dot(h, w_ref[:, c[k]:c[k + 1]])

    qa_ref[...] = (_head_norm(seg(0), gqa_ref[...]) * ATTN_SCALE).astype(BF16)
    ka = _head_norm(seg(1), gka_ref[...])
    ka_ref[...] = ka.astype(BF16)
    for b in range(PROJ_STEP_BLOCKS):
        kmean_ref[b] = jnp.mean(ka[b * MOBA_BLOCK:(b + 1) * MOBA_BLOCK], axis=0, keepdims=True)
    vat_ref[...] = _v_transposed(seg(2), MOBA_HEADS)
    qb_ref[...] = (_head_norm(seg(3), gqb_ref[...]) * ATTN_SCALE).astype(BF16)
    kb = _head_norm(seg(4), gkb_ref[...])
    kb_swapped = pltpu.roll(kb, HEAD_DIM, axis=1)
    lo = lax.broadcasted_iota(jnp.int32, (1, LANES), 1) < HEAD_DIM
    kb_ref[...] = jnp.concatenate([jnp.where(lo, kb, kb_swapped), jnp.where(lo, kb_swapped, kb)], axis=1).astype(BF16)
    vbt_ref[...] = _v_transposed(seg(5), SWA_KV_HEADS)
    bg = bgate_ref[...]
    d = _W_GA
    ga_ref[...] = 1.0 / (1.0 + jnp.exp(-(seg(6) + bg[:, :d])))
    gb_ref[...] = 1.0 / (1.0 + jnp.exp(-(seg(7) + bg[:, d:])))


def _layer_spec(a, l, **kw):
    return pl.BlockSpec((None,) + a.shape[1:], lambda i: (l,) + (0,) * (a.ndim - 1), **kw)


def _proj_call(l, x, gmix, w, bgate, gqa, gka, gqb, gkb):
    s, d = x.shape
    tm = PROJ_STEP_BLOCKS * MOBA_BLOCK
    assert s % tm == 0
    nblk = s // MOBA_BLOCK
    full = lambda a: _layer_spec(a, l, pipeline_mode=pl.Buffered(1))
    rows = lambda width: pl.BlockSpec((tm, width), lambda i: (i, 0))
    cols = lambda height: pl.BlockSpec((height, tm), lambda i: (0, i))
    out_shape = (
        jax.ShapeDtypeStruct((s, _W_QA), BF16),
        jax.ShapeDtypeStruct((s, _W_KA), BF16),
        jax.ShapeDtypeStruct((nblk, 1, _W_KA), F32),
        jax.ShapeDtypeStruct((MOBA_HEADS * VT_ROWS, s), BF16),
        jax.ShapeDtypeStruct((s, _W_QB), BF16),
        jax.ShapeDtypeStruct((s, _W_KB), BF16),
        jax.ShapeDtypeStruct((SWA_KV_HEADS * VT_ROWS, s), BF16),
        jax.ShapeDtypeStruct((s, _W_GA), F32),
        jax.ShapeDtypeStruct((s, _W_GB), F32),
    )
    out_specs = (
        rows(_W_QA), rows(_W_KA), pl.BlockSpec((PROJ_STEP_BLOCKS, 1, _W_KA), lambda i: (i, 0, 0)),
        cols(MOBA_HEADS * VT_ROWS), rows(_W_QB), rows(_W_KB), cols(SWA_KV_HEADS * VT_ROWS),
        rows(_W_GA), rows(_W_GB),
    )
    return pl.pallas_call(
        _proj_kernel,
        grid=(s // tm,),
        in_specs=[rows(d), full(gmix), full(w), full(bgate), full(gqa), full(gka), full(gqb), full(gkb)],
        out_specs=out_specs,
        out_shape=out_shape,
        compiler_params=pltpu.CompilerParams(dimension_semantics=("arbitrary",), vmem_limit_bytes=VMEM_LIMIT),
        name="proj",
    )(x, gmix, w, bgate, gqa, gka, gqb, gkb)


def _moba_kernel(cfar_ref, q_ref, k_ref, vt_ref, kmean_ref, diag_ref, o_ref,
                 qq_scr, sel_scr, m_scr, acc_scr, sta_scr, stb_scr, bias_ref, *, nblk):
    p = pl.program_id(0)
    i = pl.program_id(1)
    blk = MOBA_BLOCK

    @pl.when(i == 0)
    def _():
        for d in range(MOBA_NEAR):
            for hh in range(2):
                skew = pltpu.roll(jnp.broadcast_to(diag_ref[d, hh:hh + 1, :], (blk, 2 * blk)), 0, 1,
                                  stride=1, stride_axis=0)
                bias_ref[d, :, hh * blk:(hh + 1) * blk] = skew[:, :blk]

    q = q_ref[...]
    lane = lax.broadcasted_iota(jnp.int32, (1, LANES), 1)
    km = kmean_ref[...]
    km_hi = km.astype(BF16)
    km_lo = (km - km_hi.astype(F32)).astype(BF16)
    n_iota = lax.broadcasted_iota(jnp.int32, (nblk, 2 * blk), 0)

    qq = jnp.concatenate([jnp.where(lane < HEAD_DIM, q, jnp.zeros_like(q)),
                          jnp.where(lane >= HEAD_DIM, q, jnp.zeros_like(q))], axis=0)
    qq_scr[...] = qq

    gate2 = _nt_dot(jnp.concatenate([km_hi, km_lo], axis=0), qq)
    gates = gate2[:nblk] + gate2[nblk:]

    def select_blocks():
        g = jnp.where(n_iota < i, gates, NEG_INF)
        sel = jnp.zeros((nblk, 2 * blk), F32)
        for _ in range(MOBA_TOPK):
            mx = jnp.max(g, axis=0, keepdims=True)
            idx = jnp.min(jnp.where(g == mx, n_iota, nblk), axis=0, keepdims=True)
            hit = n_iota == idx
            valid = jnp.where(mx > NEG_INF, 1.0, 0.0)
            sel = jnp.maximum(sel, jnp.where(hit, valid, 0.0))
            g = jnp.where(hit, NEG_INF, g)
        sel_scr[...] = jnp.where(n_iota == i, 1.0, sel)
        acc_scr[...] = jnp.zeros((2, VT_ROWS, blk), F32)
        m_scr[...] = jnp.full((1, 2 * blk), M_INIT, F32)

    col = lax.broadcasted_iota(jnp.int32, (1, 2 * blk), 1)
    cfar = jnp.where(col < blk, cfar_ref[2 * p], cfar_ref[2 * p + 1])
    n_far = jnp.maximum(i - (MOBA_NEAR - 1), 0)

    def scores(first, nb, st_ref):
        off = pl.multiple_of(first * blk, blk)
        st_ref[:nb * blk, :2 * blk] = _nt_dot(k_ref[pl.ds(off, nb * blk), :], qq_scr[...])

    def attend(first, nb, st_ref, near):
        ms, pvs = [], []
        for u in range(nb):
            j = first + u
            s_u = st_ref[u * blk:(u + 1) * blk, :2 * blk]
            if near:
                s_u = s_u + bias_ref[jnp.clip(i - j, 0, MOBA_NEAR - 1)]
                on = sel_scr[pl.ds(j, 1), :] > 0.5
                m_u = jnp.max(s_u, axis=0, keepdims=True)
                ms.append(jnp.where(on, m_u, NEG_INF))
            else:
                on = sel_scr[pl.ds(j, 1), :] > jnp.where(j < n_far, 0.5, 2.0)
                m_u = jnp.max(s_u, axis=0, keepdims=True)
                ms.append(jnp.where(on, m_u + cfar, NEG_INF))
            pt = jnp.exp2(s_u - m_u).astype(BF16)
            ko = pl.multiple_of(j * blk, blk)
            pvs.append([_dot(vt_ref[hh * VT_ROWS:(hh + 1) * VT_ROWS, pl.ds(ko, blk)], pt[:, hh * blk:(hh + 1) * blk])
                        for hh in range(2)])
        m_old = m_scr[...]
        m_new = m_old
        for m_u in ms:
            m_new = jnp.maximum(m_new, m_u)
        alpha = jnp.exp2(m_old - m_new)
        ws = [jnp.exp2(m_u - m_new) for m_u in ms]
        for hh in range(2):
            cols = slice(hh * blk, (hh + 1) * blk)
            acc = acc_scr[hh] * alpha[:, cols]
            for w_u, pv in zip(ws, pvs):
                acc = acc + pv[hh] * w_u[:, cols]
            acc_scr[hh] = acc
        m_scr[...] = m_new

    fu = MOBA_FAR_UNROLL
    n_groups = (n_far + fu - 1) // fu
    near0 = jnp.maximum(i - (MOBA_NEAR - 1), 0)
    bufs = (sta_scr, stb_scr)

    def item_start(g):
        return jnp.where(g < n_groups, g * fu, near0)

    def run_items(items, then_first=None):
        for k, (first, nb, near) in enumerate(items):
            if k + 1 < len(items):
                scores(items[k + 1][0], items[k + 1][1], bufs[(k + 1) % 2])
            elif then_first is not None:
                scores(then_first, fu, bufs[(k + 1) % 2])
            attend(first, nb, bufs[k % 2], near)

    near_sizes = [fu] + [2] * ((MOBA_NEAR - fu - 1) // 2) + [1] * (1 + (MOBA_NEAR - fu - 1) % 2)
    near_items = [(near0 + int(st), nb, True) for st, nb in zip(np.cumsum([0] + near_sizes[:-1]), near_sizes)]

    scores(item_start(0), fu, sta_scr)
    select_blocks()

    unroll = MOBA_LOOP_GROUPS

    def loop_body(t, carry):
        g0 = t * unroll
        run_items([((g0 + k) * fu, fu, False) for k in range(unroll)], then_first=item_start(g0 + unroll))
        return carry

    lax.fori_loop(0, n_groups // unroll, loop_body, 0)

    g0 = n_groups // unroll * unroll
    for rem in range(unroll):

        @pl.when(n_groups - g0 == rem)
        def _():
            run_items([((g0 + k) * fu, fu, False) for k in range(rem)] + near_items)

    outs = []
    for hh in range(2):
        a = acc_scr[hh]
        outs.append(a[:HEAD_DIM] / a[HEAD_DIM:HEAD_DIM + 1])
    o_ref[...] = jnp.concatenate(outs, axis=0).T.astype(BF16)


def _moba_call(cfar, qa, ka, vat, kmean, bias_t):
    s = qa.shape[0]
    blk = MOBA_BLOCK
    nblk = s // blk
    grid_spec = pltpu.PrefetchScalarGridSpec(
        num_scalar_prefetch=1,
        grid=(HEAD_PAIRS, nblk),
        in_specs=[
            pl.BlockSpec((blk, LANES), lambda p, i, c: (i, p)),
            pl.BlockSpec((s, LANES), lambda p, i, c: (0, p)),
            pl.BlockSpec((2 * VT_ROWS, s), lambda p, i, c: (p, 0)),
            pl.BlockSpec((nblk, LANES), lambda p, i, c: (0, p)),
            pl.BlockSpec((MOBA_NEAR, None, 2, 2 * blk), lambda p, i, c: (0, p, 0, 0)),
        ],
        out_specs=pl.BlockSpec((blk, LANES), lambda p, i, c: (i, p)),
        scratch_shapes=[
            pltpu.VMEM((2 * blk, LANES), BF16),
            pltpu.VMEM((nblk, 2 * blk), F32),
            pltpu.VMEM((1, 2 * blk), F32),
            pltpu.VMEM((2, VT_ROWS, blk), F32),
            pltpu.VMEM((MOBA_FAR_UNROLL * blk, 2 * blk + LANES), F32),
            pltpu.VMEM((MOBA_FAR_UNROLL * blk, 2 * blk + LANES), F32),
            pltpu.VMEM((MOBA_NEAR, blk, 2 * blk), F32),
        ],
    )
    return pl.pallas_call(
        functools.partial(_moba_kernel, nblk=nblk),
        grid_spec=grid_spec,
        out_shape=jax.ShapeDtypeStruct((s, MOBA_HEADS * HEAD_DIM), BF16),
        compiler_params=pltpu.CompilerParams(dimension_semantics=("arbitrary", "arbitrary"),
                                             vmem_limit_bytes=VMEM_LIMIT),
        name="moba",
    )(cfar, qa, ka, vat, kmean, bias_t)


def _swa_kernel(sink_ref, q_ref, kprev_ref, kcur_ref, vprev_ref, vcur_ref, bias_ref, o_ref, st_scr):
    b = pl.program_id(0)
    w = SWA_WINDOW
    kall = jnp.concatenate([kprev_ref[...], kcur_ref[...]], axis=0)
    vall = jnp.concatenate([vprev_ref[...], vcur_ref[...]], axis=1)
    row = lax.broadcasted_iota(jnp.int32, (2 * w, 1), 0)
    keep = row >= jnp.where(b > 0, 0, w)
    lane = lax.broadcasted_iota(jnp.int32, (1, LANES), 1)
    col = lax.broadcasted_iota(jnp.int32, (1, 2 * w), 1)
    group = SWA_Q_HEADS // SWA_KV_HEADS
    chains = [(sb, pr) for sb in range(SWA_STEP_BLOCKS) for pr in range(SWA_Q_HEADS // 2)]
    def scores(c):
        sb, pr = chains[c]
        g = (2 * pr) // group
        qp = q_ref[sb * w:(sb + 1) * w, pr * LANES:(pr + 1) * LANES]
        zero = jnp.zeros_like(qp)
        qq = jnp.concatenate([jnp.where(lane < HEAD_DIM, qp, zero),
                              jnp.where(lane >= HEAD_DIM, qp, zero)], axis=0)
        kd = kall[sb * w:(sb + 2) * w, g * LANES:(g + 1) * LANES]
        st_scr[c, :, :2 * w] = _nt_dot(kd, qq)

    ahead = SWA_SCORES_AHEAD
    for c in range(min(ahead, len(chains))):
        scores(c)
    for c, (sb, pr) in enumerate(chains):
        if c + ahead < len(chains):
            scores(c + ahead)
        g = (2 * pr) // group
        st = st_scr[c, :, :2 * w] + bias_ref[pr]
        if sb == 0:
            st = jnp.where(keep, st, NEG_INF)
        sink = jnp.where(col < w, sink_ref[2 * pr], sink_ref[2 * pr + 1])
        m = jnp.maximum(jnp.max(st, axis=0, keepdims=True), sink)
        pt = jnp.exp2(st - m).astype(BF16)
        acc = _dot(vall[g * VT_ROWS:(g + 1) * VT_ROWS, sb * w:(sb + 2) * w], pt)
        denom = acc[HEAD_DIM:HEAD_DIM + 1] + jnp.exp2(sink - m)
        o = acc[:HEAD_DIM] / denom
        oo = jnp.concatenate([o[:, :w], o[:, w:]], axis=0)
        o_ref[sb * w:(sb + 1) * w, pr * LANES:(pr + 1) * LANES] = oo.T.astype(BF16)


def _swa_call(sinks, qb, kbd, vbt, bias_t):
    s = qb.shape[0]
    w = SWA_WINDOW
    n = SWA_STEP_BLOCKS
    assert s % (n * w) == 0
    prev_blk = lambda i: jnp.maximum(n * i - 1, 0)
    grid_spec = pltpu.PrefetchScalarGridSpec(
        num_scalar_prefetch=1,
        grid=(s // (n * w),),
        in_specs=[
            pl.BlockSpec((n * w, _W_QB), lambda i, c: (i, 0)),
            pl.BlockSpec((w, _W_KB), lambda i, c: (prev_blk(i), 0)),
            pl.BlockSpec((n * w, _W_KB), lambda i, c: (i, 0)),
            pl.BlockSpec((SWA_KV_HEADS * VT_ROWS, w), lambda i, c: (0, prev_blk(i))),
            pl.BlockSpec((SWA_KV_HEADS * VT_ROWS, n * w), lambda i, c: (0, i)),
            pl.BlockSpec(bias_t.shape, lambda i, c: (0, 0, 0)),
        ],
        out_specs=pl.BlockSpec((n * w, _W_QB), lambda i, c: (i, 0)),
        scratch_shapes=[pltpu.VMEM((n * SWA_Q_HEADS // 2, 2 * w, 2 * w + LANES), F32)],
    )
    return pl.pallas_call(
        _swa_kernel,
        grid_spec=grid_spec,
        out_shape=jax.ShapeDtypeStruct((s, _W_QB), BF16),
        compiler_params=pltpu.CompilerParams(dimension_semantics=("arbitrary",), vmem_limit_bytes=VMEM_LIMIT),
        name="swa",
    )(sinks, qb, kbd, kbd, vbt, vbt, bias_t)


def _post_kernel(x_ref, ya_ref, yb_ref, ga_ref, gb_ref, wa_ref, wb_ref, wo_ref, gffn_ref, w1_ref, w2_ref, o_ref,
                 *, tf):
    merged = ga_ref[...] * _dot(ya_ref[...], wa_ref[...]) + gb_ref[...] * _dot(yb_ref[...], wb_ref[...])
    xn = x_ref[...] + _dot(merged.astype(BF16), wo_ref[...])
    h = _rms_norm_rows(xn, gffn_ref[...]).astype(BF16)
    acc = xn
    for c in range(w1_ref.shape[1] // tf):
        u = jnp.maximum(_dot(h, w1_ref[:, c * tf:(c + 1) * tf]), 0.0)
        acc = acc + _dot((u * u).astype(BF16), w2_ref[c * tf:(c + 1) * tf, :])
    o_ref[...] = acc


def _post_call(l, x, ya, yb, ga, gb, wa, wb, wo, gffn, w1, w2, tm=512, tf=1024):
    s, d = x.shape
    full = lambda a: _layer_spec(a, l, pipeline_mode=pl.Buffered(1))
    rows = lambda a: pl.BlockSpec((tm, a.shape[1]), lambda i: (i, 0))
    return pl.pallas_call(
        functools.partial(_post_kernel, tf=tf),
        grid=(s // tm,),
        in_specs=[rows(x), rows(ya), rows(yb), rows(ga), rows(gb),
                  full(wa), full(wb), full(wo), full(gffn), full(w1), full(w2)],
        out_specs=pl.BlockSpec((tm, d), lambda i: (i, 0)),
        out_shape=jax.ShapeDtypeStruct((s, d), F32),
        compiler_params=pltpu.CompilerParams(dimension_semantics=("arbitrary",), vmem_limit_bytes=VMEM_LIMIT),
        name="post",
    )(x, ya, yb, ga, gb, wa, wb, wo, gffn, w1, w2)


def _bias_by_distance(table, lo, hi, keep):
    dist = np.arange(lo, hi)
    vals = jnp.take(table, jnp.asarray(_rel_bucket_np(dist).astype(np.int32)), axis=0).T
    return jnp.where(jnp.asarray(keep(dist))[None, :], vals, NEG_INF)


def _toeplitz(w, n):
    lead = w.shape[:-1]
    w_pad = jnp.concatenate([w, jnp.zeros(lead + (1,), w.dtype)], axis=-1)
    skew = jnp.tile(w_pad, (1,) * len(lead) + (n,))[..., :n * (2 * n - 1)].reshape(lead + (n, 2 * n - 1))
    return skew[..., n - 1:]


def _bias_tables(rel_bias):
    blk, w = MOBA_BLOCK, SWA_WINDOW
    vec = _bias_by_distance(rel_bias[:, :MOBA_HEADS], 1 - blk, MOBA_NEAR * blk, lambda dist: dist >= 0)
    vec = jnp.concatenate([vec, jnp.zeros((MOBA_HEADS, 1), F32)], axis=1)
    moba = jnp.stack([jnp.roll(vec[:, d * blk:(d + 2) * blk], 1 - blk, axis=1) for d in range(MOBA_NEAR)])
    moba = moba.reshape(MOBA_NEAR, HEAD_PAIRS, 2, 2 * blk)
    cfar = rel_bias[N_REL_BUCKETS - 1, :MOBA_HEADS]

    vec = _bias_by_distance(rel_bias[:, MOBA_HEADS:], 1 - w, 3 * w, lambda dist: (dist >= 0) & (dist < w))
    swa = _toeplitz(vec, 2 * w)[:, :, :w]
    swa = swa.reshape(SWA_Q_HEADS // 2, 2, 2 * w, w).transpose(0, 2, 1, 3).reshape(SWA_Q_HEADS // 2, 2 * w, 2 * w)
    return (moba * LOG2E).astype(F32), (cfar * LOG2E).astype(F32), (swa * LOG2E).astype(F32)


def kernel(x, rel_bias, g_mix, w_in, b_gate, q_norm_a, k_norm_a, q_norm_b, k_norm_b, sinks,
           w_branch_a, w_branch_b, w_out, g_ffn, w_ff1, w_ff2):
    b, s, d = x.shape
    assert b == 1 and s % MOBA_BLOCK == 0 and s // MOBA_BLOCK >= MOBA_NEAR
    depth = w_in.shape[0]
    bias_moba, cfar, bias_swa = _bias_tables(rel_bias)
    row = lambda a: a[:, None, :]
    gains = lambda g, width: row(jnp.tile(g, (1, width // HEAD_DIM)))
    w_in_b, wa_b, wb_b, wo_b, w1_b, w2_b = (a.astype(BF16) for a in (w_in, w_branch_a, w_branch_b, w_out, w_ff1, w_ff2))
    gqa, gka = gains(q_norm_a, _W_QA), gains(k_norm_a, _W_KA)
    gqb, gkb = gains(q_norm_b, _W_QB), gains(k_norm_b, SWA_KV_HEADS * HEAD_DIM)
    sinks2 = sinks * LOG2E
    xs = x[0]
    for l in range(depth):
        qa, ka, kmean, vat, qb, kbd, vbt, ga, gb = _proj_call(l, xs, row(g_mix), w_in_b, row(b_gate), gqa, gka, gqb, gkb)
        ya = _moba_call(cfar, qa, ka, vat, kmean.reshape(kmean.shape[0], kmean.shape[2]), bias_moba)
        yb = _swa_call(sinks2[l], qb, kbd, vbt, bias_swa)
        xs = _post_call(l, xs, ya, yb, ga, gb, wa_b, wb_b, wo_b, row(g_ffn), w1_b, w2_b)
    return xs[None]
```

```python
import functools

import jax
import jax.numpy as jnp
import numpy as np
from jax import lax
from jax.experimental import pallas as pl
from jax.experimental.pallas import tpu as pltpu

HEAD_DIM = 64
MOBA_HEADS = 8
MOBA_BLOCK = 256
MOBA_TOPK = 3
SWA_Q_HEADS = 8
SWA_KV_HEADS = 2
SWA_WINDOW = 128
SWA_STEP_BLOCKS = 8
SWA_SCORES_AHEAD = 4
PROJ_STEP_BLOCKS = 4
N_REL_BUCKETS = 32
REL_MAX_DISTANCE = 2048
NORM_EPS = 1e-6
LOG2E = 1.4426950408889634
ATTN_SCALE = HEAD_DIM ** -0.5 * LOG2E

LANES = 128
HEAD_PAIRS = MOBA_HEADS // 2
VT_ROWS = 80
MOBA_FAR_UNROLL = 2
MOBA_LOOP_GROUPS = 12
assert MOBA_LOOP_GROUPS % 2 == 0
VMEM_LIMIT = 56 * 1024 * 1024

BF16 = jnp.bfloat16
F32 = jnp.float32
NEG_INF = float("-inf")
M_INIT = -1e30


def _rel_bucket_np(dist):
    n = np.maximum(dist, 0)
    exact = N_REL_BUCKETS // 2
    nf = np.maximum(n, 1).astype(np.float32)
    large = exact + (np.log(nf / np.float32(exact)) / np.float32(np.log(REL_MAX_DISTANCE / exact))
                     * np.float32(N_REL_BUCKETS - exact)).astype(np.int32)
    large = np.minimum(large, N_REL_BUCKETS - 1)
    return np.where(n < exact, n, large)


def _moba_near_blocks():
    dist = np.arange(1, 2 * REL_MAX_DISTANCE)
    first_last = int(dist[_rel_bucket_np(dist) == N_REL_BUCKETS - 1][0])
    return -(-(first_last - 1) // MOBA_BLOCK) + 1


MOBA_NEAR = _moba_near_blocks()
assert 2 <= MOBA_FAR_UNROLL < MOBA_NEAR


def _nt_dot(a, b):
    return lax.dot_general(a, b, (((1,), (1,)), ((), ())), preferred_element_type=F32)


def _dot(a, b):
    return jnp.dot(a, b, preferred_element_type=F32)


def _rms_norm_rows(xf, g):
    ms = jnp.mean(xf * xf, axis=-1, keepdims=True)
    return xf * lax.rsqrt(ms + NORM_EPS) * g


def _head_norm(acc, g2):
    w = acc.shape[1]
    lane = lax.broadcasted_iota(jnp.int32, (1, LANES), 1)
    lo = lane < HEAD_DIM
    outs = []
    for b in range(w // LANES):
        xb = acc[:, b * LANES:(b + 1) * LANES]
        x2 = xb * xb
        s_lo = jnp.sum(jnp.where(lo, x2, 0.0), axis=-1, keepdims=True)
        s_hi = jnp.sum(jnp.where(lo, 0.0, x2), axis=-1, keepdims=True)
        r = jnp.where(lo, lax.rsqrt(s_lo / HEAD_DIM + NORM_EPS), lax.rsqrt(s_hi / HEAD_DIM + NORM_EPS))
        outs.append(xb * r * g2[:, b * LANES:(b + 1) * LANES])
    return outs[0] if len(outs) == 1 else jnp.concatenate(outs, axis=1)


def _v_transposed(v, n_heads):
    rows = v.shape[0]
    vt = v.T
    r = lax.broadcasted_iota(jnp.int32, (VT_ROWS - HEAD_DIM, rows), 0)
    aug = jnp.where(r == 0, 1.0, 0.0).astype(F32)
    parts = []
    for h in range(n_heads):
        parts.append(vt[h * HEAD_DIM:(h + 1) * HEAD_DIM])
        parts.append(aug)
    return jnp.concatenate(parts, axis=0).astype(BF16)


_W_QA, _W_KA, _W_VA, _W_QB, _W_KB, _W_VB, _W_GA, _W_GB = 512, 512, 512, 512, 256, 128, 1024, 1024
_PROJ_COLS = np.cumsum([0, _W_QA, _W_KA, _W_VA, _W_QB, SWA_KV_HEADS * HEAD_DIM, _W_VB, _W_GA, _W_GB]).tolist()


def _proj_kernel(x_ref, gmix_ref, w_ref, bgate_ref, gqa_ref, gka_ref, gqb_ref, gkb_ref,
                 qa_ref, ka_ref, kmean_ref, vat_ref, qb_ref, kb_ref, vbt_ref, ga_ref, gb_ref):
    h = _rms_norm_rows(x_ref[...], gmix_ref[...]).astype(BF16)
    c = _PROJ_COLS

    def seg(k):
        return _dot(h, w_ref[:, c[k]:c[k + 1]])

    qa_ref[...] = (_head_norm(seg(0), gqa_ref[...]) * ATTN_SCALE).astype(BF16)
    ka = _head_norm(seg(1), gka_ref[...])
    ka_ref[...] = ka.astype(BF16)
    for b in range(PROJ_STEP_BLOCKS):
        kmean_ref[b] = jnp.mean(ka[b * MOBA_BLOCK:(b + 1) * MOBA_BLOCK], axis=0, keepdims=True)
    vat_ref[...] = _v_transposed(seg(2), MOBA_HEADS)
    qb_ref[...] = (_head_norm(seg(3), gqb_ref[...]) * ATTN_SCALE).astype(BF16)
    kb = _head_norm(seg(4), gkb_ref[...])
    kb_swapped = pltpu.roll(kb, HEAD_DIM, axis=1)
    lo = lax.broadcasted_iota(jnp.int32, (1, LANES), 1) < HEAD_DIM
    kb_ref[...] = jnp.concatenate([jnp.where(lo, kb, kb_swapped), jnp.where(lo, kb_swapped, kb)], axis=1).astype(BF16)
    vbt_ref[...] = _v_transposed(seg(5), SWA_KV_HEADS)
    bg = bgate_ref[...]
    d = _W_GA
    ga_ref[...] = 1.0 / (1.0 + jnp.exp(-(seg(6) + bg[:, :d])))
    gb_ref[...] = 1.0 / (1.0 + jnp.exp(-(seg(7) + bg[:, d:])))


def _layer_spec(a, l, **kw):
    return pl.BlockSpec((None,) + a.shape[1:], lambda i: (l,) + (0,) * (a.ndim - 1), **kw)


def _proj_call(l, x, gmix, w, bgate, gqa, gka, gqb, gkb):
    s, d = x.shape
    tm = PROJ_STEP_BLOCKS * MOBA_BLOCK
    assert s % tm == 0
    nblk = s // MOBA_BLOCK
    full = lambda a: _layer_spec(a, l, pipeline_mode=pl.Buffered(1))
    rows = lambda width: pl.BlockSpec((tm, width), lambda i: (i, 0))
    cols = lambda height: pl.BlockSpec((height, tm), lambda i: (0, i))
    out_shape = (
        jax.ShapeDtypeStruct((s, _W_QA), BF16),
        jax.ShapeDtypeStruct((s, _W_KA), BF16),
        jax.ShapeDtypeStruct((nblk, 1, _W_KA), F32),
        jax.ShapeDtypeStruct((MOBA_HEADS * VT_ROWS, s), BF16),
        jax.ShapeDtypeStruct((s, _W_QB), BF16),
        jax.ShapeDtypeStruct((s, _W_KB), BF16),
        jax.ShapeDtypeStruct((SWA_KV_HEADS * VT_ROWS, s), BF16),
        jax.ShapeDtypeStruct((s, _W_GA), F32),
        jax.ShapeDtypeStruct((s, _W_GB), F32),
    )
    out_specs = (
        rows(_W_QA), rows(_W_KA), pl.BlockSpec((PROJ_STEP_BLOCKS, 1, _W_KA), lambda i: (i, 0, 0)),
        cols(MOBA_HEADS * VT_ROWS), rows(_W_QB), rows(_W_KB), cols(SWA_KV_HEADS * VT_ROWS),
        rows(_W_GA), rows(_W_GB),
    )
    return pl.pallas_call(
        _proj_kernel,
        grid=(s // tm,),
        in_specs=[rows(d), full(gmix), full(w), full(bgate), full(gqa), full(gka), full(gqb), full(gkb)],
        out_specs=out_specs,
        out_shape=out_shape,
        compiler_params=pltpu.CompilerParams(dimension_semantics=("arbitrary",), vmem_limit_bytes=VMEM_LIMIT),
        name="proj",
    )(x, gmix, w, bgate, gqa, gka, gqb, gkb)


def _moba_kernel(cfar_ref, q_ref, k_ref, vt_ref, kmean_ref, diag_ref, o_ref,
                 qq_scr, sel_scr, m_scr, acc_scr, sta_scr, stb_scr, bias_ref, *, nblk):
    p = pl.program_id(0)
    i = pl.program_id(1)
    blk = MOBA_BLOCK

    @pl.when(i == 0)
    def _():
        for d in range(MOBA_NEAR):
            for hh in range(2):
                skew = pltpu.roll(jnp.broadcast_to(diag_ref[d, hh:hh + 1, :], (blk, 2 * blk)), 0, 1,
                                  stride=1, stride_axis=0)
                bias_ref[d, :, hh * blk:(hh + 1) * blk] = skew[:, :blk]

    q = q_ref[...]
    lane = lax.broadcasted_iota(jnp.int32, (1, LANES), 1)
    km = kmean_ref[...]
    km_hi = km.astype(BF16)
    km_lo = (km - km_hi.astype(F32)).astype(BF16)
    n_iota = lax.broadcasted_iota(jnp.int32, (nblk, 2 * blk), 0)

    qq = jnp.concatenate([jnp.where(lane < HEAD_DIM, q, jnp.zeros_like(q)),
                          jnp.where(lane >= HEAD_DIM, q, jnp.zeros_like(q))], axis=0)
    qq_scr[...] = qq

    gate2 = _nt_dot(jnp.concatenate([km_hi, km_lo], axis=0), qq)
    gates = gate2[:nblk] + gate2[nblk:]

    def select_blocks():
        g = jnp.where(n_iota < i, gates, NEG_INF)
        sel = jnp.zeros((nblk, 2 * blk), F32)
        for _ in range(MOBA_TOPK):
            mx = jnp.max(g, axis=0, keepdims=True)
            idx = jnp.min(jnp.where(g == mx, n_iota, nblk), axis=0, keepdims=True)
            hit = n_iota == idx
            valid = jnp.where(mx > NEG_INF, 1.0, 0.0)
            sel = jnp.maximum(sel, jnp.where(hit, valid, 0.0))
            g = jnp.where(hit, NEG_INF, g)
        sel_scr[...] = jnp.where(n_iota == i, 1.0, sel)
        acc_scr[...] = jnp.zeros((2, VT_ROWS, blk), F32)
        m_scr[...] = jnp.full((1, 2 * blk), M_INIT, F32)

    col = lax.broadcasted_iota(jnp.int32, (1, 2 * blk), 1)
    cfar = jnp.where(col < blk, cfar_ref[2 * p], cfar_ref[2 * p + 1])
    n_far = jnp.maximum(i - (MOBA_NEAR - 1), 0)

    def scores(first, nb, st_ref):
        off = pl.multiple_of(first * blk, blk)
        st_ref[:nb * blk, :2 * blk] = _nt_dot(k_ref[pl.ds(off, nb * blk), :], qq_scr[...])

    def attend(first, nb, st_ref, near):
        ms, pvs = [], []
        for u in range(nb):
            j = first + u
            s_u = st_ref[u * blk:(u + 1) * blk, :2 * blk]
            if near:
                s_u = s_u + bias_ref[jnp.clip(i - j, 0, MOBA_NEAR - 1)]
                on = sel_scr[pl.ds(j, 1), :] > 0.5
                m_u = jnp.max(s_u, axis=0, keepdims=True)
                ms.append(jnp.where(on, m_u, NEG_INF))
            else:
                on = sel_scr[pl.ds(j, 1), :] > jnp.where(j < n_far, 0.5, 2.0)
                m_u = jnp.max(s_u, axis=0, keepdims=True)
                ms.append(jnp.where(on, m_u + cfar, NEG_INF))
            pt = jnp.exp2(s_u - m_u).astype(BF16)
            ko = pl.multiple_of(j * blk, blk)
            pvs.append([_dot(vt_ref[hh * VT_ROWS:(hh + 1) * VT_ROWS, pl.ds(ko, blk)], pt[:, hh * blk:(hh + 1) * blk])
                        for hh in range(2)])
        m_old = m_scr[...]
        m_new = m_old
        for m_u in ms:
            m_new = jnp.maximum(m_new, m_u)
        alpha = jnp.exp2(m_old - m_new)
        ws = [jnp.exp2(m_u - m_new) for m_u in ms]
        for hh in range(2):
            cols = slice(hh * blk, (hh + 1) * blk)
            acc = acc_scr[hh] * alpha[:, cols]
            for w_u, pv in zip(ws, pvs):
                acc = acc + pv[hh] * w_u[:, cols]
            acc_scr[hh] = acc
        m_scr[...] = m_new

    fu = MOBA_FAR_UNROLL
    n_groups = (n_far + fu - 1) // fu
    near0 = jnp.maximum(i - (MOBA_NEAR - 1), 0)
    bufs = (sta_scr, stb_scr)

    def item_start(g):
        return jnp.where(g < n_groups, g * fu, near0)

    def run_items(items, then_first=None):
        for k, (first, nb, near) in enumerate(items):
            if k + 1 < len(items):
                scores(items[k + 1][0], items[k + 1][1], bufs[(k + 1) % 2])
            elif then_first is not None:
                scores(then_first, fu, bufs[(k + 1) % 2])
            attend(first, nb, bufs[k % 2], near)

    near_sizes = [fu] + [2] * ((MOBA_NEAR - fu - 1) // 2) + [1] * (1 + (MOBA_NEAR - fu - 1) % 2)
    near_items = [(near0 + int(st), nb, True) for st, nb in zip(np.cumsum([0] + near_sizes[:-1]), near_sizes)]

    scores(item_start(0), fu, sta_scr)
    select_blocks()

    unroll = MOBA_LOOP_GROUPS

    def loop_body(t, carry):
        g0 = t * unroll
        run_items([((g0 + k) * fu, fu, False) for k in range(unroll)], then_first=item_start(g0 + unroll))
        return carry

    lax.fori_loop(0, n_groups // unroll, loop_body, 0)

    g0 = n_groups // unroll * unroll
    for rem in range(unroll):

        @pl.when(n_groups - g0 == rem)
        def _():
            run_items([((g0 + k) * fu, fu, False) for k in range(rem)] + near_items)

    outs = []
    for hh in range(2):
        a = acc_scr[hh]
        outs.append(a[:HEAD_DIM] / a[HEAD_DIM:HEAD_DIM + 1])
    o_ref[...] = jnp.concatenate(outs, axis=0).T.astype(BF16)


def _moba_call(cfar, qa, ka, vat, kmean, bias_t):
    s = qa.shape[0]
    blk = MOBA_BLOCK
    nblk = s // blk
    grid_spec = pltpu.PrefetchScalarGridSpec(
        num_scalar_prefetch=1,
        grid=(HEAD_PAIRS, nblk),
        in_specs=[
            pl.BlockSpec((blk, LANES), lambda p, i, c: (i, p)),
            pl.BlockSpec((s, LANES), lambda p, i, c: (0, p)),
            pl.BlockSpec((2 * VT_ROWS, s), lambda p, i, c: (p, 0)),
            pl.BlockSpec((nblk, LANES), lambda p, i, c: (0, p)),
            pl.BlockSpec((MOBA_NEAR, None, 2, 2 * blk), lambda p, i, c: (0, p, 0, 0)),
        ],
        out_specs=pl.BlockSpec((blk, LANES), lambda p, i, c: (i, p)),
        scratch_shapes=[
            pltpu.VMEM((2 * blk, LANES), BF16),
            pltpu.VMEM((nblk, 2 * blk), F32),
            pltpu.VMEM((1, 2 * blk), F32),
            pltpu.VMEM((2, VT_ROWS, blk), F32),
            pltpu.VMEM((MOBA_FAR_UNROLL * blk, 2 * blk + LANES), F32),
            pltpu.VMEM((MOBA_FAR_UNROLL * blk, 2 * blk + LANES), F32),
            pltpu.VMEM((MOBA_NEAR, blk, 2 * blk), F32),
        ],
    )
    return pl.pallas_call(
        functools.partial(_moba_kernel, nblk=nblk),
        grid_spec=grid_spec,
        out_shape=jax.ShapeDtypeStruct((s, MOBA_HEADS * HEAD_DIM), BF16),
        compiler_params=pltpu.CompilerParams(dimension_semantics=("arbitrary", "arbitrary"),
                                             vmem_limit_bytes=VMEM_LIMIT),
        name="moba",
    )(cfar, qa, ka, vat, kmean, bias_t)


def _swa_kernel(sink_ref, q_ref, kprev_ref, kcur_ref, vprev_ref, vcur_ref, bias_ref, o_ref, st_scr):
    b = pl.program_id(0)
    w = SWA_WINDOW
    kall = jnp.concatenate([kprev_ref[...], kcur_ref[...]], axis=0)
    vall = jnp.concatenate([vprev_ref[...], vcur_ref[...]], axis=1)
    row = lax.broadcasted_iota(jnp.int32, (2 * w, 1), 0)
    keep = row >= jnp.where(b > 0, 0, w)
    lane = lax.broadcasted_iota(jnp.int32, (1, LANES), 1)
    col = lax.broadcasted_iota(jnp.int32, (1, 2 * w), 1)
    group = SWA_Q_HEADS // SWA_KV_HEADS
    chains = [(sb, pr) for sb in range(SWA_STEP_BLOCKS) for pr in range(SWA_Q_HEADS // 2)]
    def scores(c):
        sb, pr = chains[c]
        g = (2 * pr) // group
        qp = q_ref[sb * w:(sb + 1) * w, pr * LANES:(pr + 1) * LANES]
        zero = jnp.zeros_like(qp)
        qq = jnp.concatenate([jnp.where(lane < HEAD_DIM, qp, zero),
                              jnp.where(lane >= HEAD_DIM, qp, zero)], axis=0)
        kd = kall[sb * w:(sb + 2) * w, g * LANES:(g + 1) * LANES]
        st_scr[c, :, :2 * w] = _nt_dot(kd, qq)

    ahead = SWA_SCORES_AHEAD
    for c in range(min(ahead, len(chains))):
        scores(c)
    for c, (sb, pr) in enumerate(chains):
        if c + ahead < len(chains):
            scores(c + ahead)
        g = (2 * pr) // group
        st = st_scr[c, :, :2 * w] + bias_ref[pr]
        if sb == 0:
            st = jnp.where(keep, st, NEG_INF)
        sink = jnp.where(col < w, sink_ref[2 * pr], sink_ref[2 * pr + 1])
        m = jnp.maximum(jnp.max(st, axis=0, keepdims=True), sink)
        pt = jnp.exp2(st - m).astype(BF16)
        acc = _dot(vall[g * VT_ROWS:(g + 1) * VT_ROWS, sb * w:(sb + 2) * w], pt)
        denom = acc[HEAD_DIM:HEAD_DIM + 1] + jnp.exp2(sink - m)
        o = acc[:HEAD_DIM] / denom
        oo = jnp.concatenate([o[:, :w], o[:, w:]], axis=0)
        o_ref[sb * w:(sb + 1) * w, pr * LANES:(pr + 1) * LANES] = oo.T.astype(BF16)


def _swa_call(sinks, qb, kbd, vbt, bias_t):
    s = qb.shape[0]
    w = SWA_WINDOW
    n = SWA_STEP_BLOCKS
    assert s % (n * w) == 0
    prev_blk = lambda i: jnp.maximum(n * i - 1, 0)
    grid_spec = pltpu.PrefetchScalarGridSpec(
        num_scalar_prefetch=1,
        grid=(s // (n * w),),
        in_specs=[
            pl.BlockSpec((n * w, _W_QB), lambda i, c: (i, 0)),
            pl.BlockSpec((w, _W_KB), lambda i, c: (prev_blk(i), 0)),
            pl.BlockSpec((n * w, _W_KB), lambda i, c: (i, 0)),
            pl.BlockSpec((SWA_KV_HEADS * VT_ROWS, w), lambda i, c: (0, prev_blk(i))),
            pl.BlockSpec((SWA_KV_HEADS * VT_ROWS, n * w), lambda i, c: (0, i)),
            pl.BlockSpec(bias_t.shape, lambda i, c: (0, 0, 0)),
        ],
        out_specs=pl.BlockSpec((n * w, _W_QB), lambda i, c: (i, 0)),
        scratch_shapes=[pltpu.VMEM((n * SWA_Q_HEADS // 2, 2 * w, 2 * w + LANES), F32)],
    )
    return pl.pallas_call(
        _swa_kernel,
        grid_spec=grid_spec,
        out_shape=jax.ShapeDtypeStruct((s, _W_QB), BF16),
        compiler_params=pltpu.CompilerParams(dimension_semantics=("arbitrary",), vmem_limit_bytes=VMEM_LIMIT),
        name="swa",
    )(sinks, qb, kbd, kbd, vbt, vbt, bias_t)


def _post_kernel(x_ref, ya_ref, yb_ref, ga_ref, gb_ref, wa_ref, wb_ref, wo_ref, gffn_ref, w1_ref, w2_ref, o_ref,
                 *, tf):
    merged = ga_ref[...] * _dot(ya_ref[...], wa_ref[...]) + gb_ref[...] * _dot(yb_ref[...], wb_ref[...])
    xn = x_ref[...] + _dot(merged.astype(BF16), wo_ref[...])
    h = _rms_norm_rows(xn, gffn_ref[...]).astype(BF16)
    acc = xn
    for c in range(w1_ref.shape[1] // tf):
        u = jnp.maximum(_dot(h, w1_ref[:, c * tf:(c + 1) * tf]), 0.0)
        acc = acc + _dot((u * u).astype(BF16), w2_ref[c * tf:(c + 1) * tf, :])
    o_ref[...] = acc


def _post_call(l, x, ya, yb, ga, gb, wa, wb, wo, gffn, w1, w2, tm=512, tf=1024):
    s, d = x.shape
    full = lambda a: _layer_spec(a, l, pipeline_mode=pl.Buffered(1))
    rows = lambda a: pl.BlockSpec((tm, a.shape[1]), lambda i: (i, 0))
    return pl.pallas_call(
        functools.partial(_post_kernel, tf=tf),
        grid=(s // tm,),
        in_specs=[rows(x), rows(ya), rows(yb), rows(ga), rows(gb),
                  full(wa), full(wb), full(wo), full(gffn), full(w1), full(w2)],
        out_specs=pl.BlockSpec((tm, d), lambda i: (i, 0)),
        out_shape=jax.ShapeDtypeStruct((s, d), F32),
        compiler_params=pltpu.CompilerParams(dimension_semantics=("arbitrary",), vmem_limit_bytes=VMEM_LIMIT),
        name="post",
    )(x, ya, yb, ga, gb, wa, wb, wo, gffn, w1, w2)


def _bias_by_distance(table, lo, hi, keep):
    dist = np.arange(lo, hi)
    vals = jnp.take(table, jnp.asarray(_rel_bucket_np(dist).astype(np.int32)), axis=0).T
    return jnp.where(jnp.asarray(keep(dist))[None, :], vals, NEG_INF)


def _toeplitz(w, n):
    lead = w.shape[:-1]
    w_pad = jnp.concatenate([w, jnp.zeros(lead + (1,), w.dtype)], axis=-1)
    skew = jnp.tile(w_pad, (1,) * len(lead) + (n,))[..., :n * (2 * n - 1)].reshape(lead + (n, 2 * n - 1))
    return skew[..., n - 1:]


def _bias_tables(rel_bias):
    blk, w = MOBA_BLOCK, SWA_WINDOW
    vec = _bias_by_distance(rel_bias[:, :MOBA_HEADS], 1 - blk, MOBA_NEAR * blk, lambda dist: dist >= 0)
    vec = jnp.concatenate([vec, jnp.zeros((MOBA_HEADS, 1), F32)], axis=1)
    moba = jnp.stack([jnp.roll(vec[:, d * blk:(d + 2) * blk], 1 - blk, axis=1) for d in range(MOBA_NEAR)])
    moba = moba.reshape(MOBA_NEAR, HEAD_PAIRS, 2, 2 * blk)
    cfar = rel_bias[N_REL_BUCKETS - 1, :MOBA_HEADS]

    vec = _bias_by_distance(rel_bias[:, MOBA_HEADS:], 1 - w, 3 * w, lambda dist: (dist >= 0) & (dist < w))
    swa = _toeplitz(vec, 2 * w)[:, :, :w]
    swa = swa.reshape(SWA_Q_HEADS // 2, 2, 2 * w, w).transpose(0, 2, 1, 3).reshape(SWA_Q_HEADS // 2, 2 * w, 2 * w)
    return (moba * LOG2E).astype(F32), (cfar * LOG2E).astype(F32), (swa * LOG2E).astype(F32)


def kernel(x, rel_bias, g_mix, w_in, b_gate, q_norm_a, k_norm_a, q_norm_b, k_norm_b, sinks,
           w_branch_a, w_branch_b, w_out, g_ffn, w_ff1, w_ff2):
    b, s, d = x.shape
    assert b == 1 and s % MOBA_BLOCK == 0 and s // MOBA_BLOCK >= MOBA_NEAR
    depth = w_in.shape[0]
    bias_moba, cfar, bias_swa = _bias_tables(rel_bias)
    row = lambda a: a[:, None, :]
    gains = lambda g, width: row(jnp.tile(g, (1, width // HEAD_DIM)))
    w_in_b, wa_b, wb_b, wo_b, w1_b, w2_b = (a.astype(BF16) for a in (w_in, w_branch_a, w_branch_b, w_out, w_ff1, w_ff2))
    gqa, gka = gains(q_norm_a, _W_QA), gains(k_norm_a, _W_KA)
    gqb, gkb = gains(q_norm_b, _W_QB), gains(k_norm_b, SWA_KV_HEADS * HEAD_DIM)
    sinks2 = sinks * LOG2E
    xs = x[0]
    for l in range(depth):
        qa, ka, kmean, vat, qb, kbd, vbt, ga, gb = _proj_call(l, xs, row(g_mix), w_in_b, row(b_gate), gqa, gka, gqb, gkb)
        ya = _moba_call(cfar, qa, ka, vat, kmean.reshape(kmean.shape[0], kmean.shape[2]), bias_moba)
        yb = _swa_call(sinks2[l], qb, kbd, vbt, bias_swa)
        xs = _post_call(l, xs, ya, yb, ga, gb, wa_b, wb_b, wo_b, row(g_ffn), w1_b, w2_b)
    return xs[None]
```

```python
import functools

import jax
import jax.numpy as jnp
import numpy as np
from jax import lax
from jax.experimental import pallas as pl
from jax.experimental.pallas import tpu as pltpu

HEAD_DIM = 64
MOBA_HEADS = 8
MOBA_BLOCK = 256
MOBA_TOPK = 3
SWA_Q_HEADS = 8
SWA_KV_HEADS = 2
SWA_WINDOW = 128
SWA_STEP_BLOCKS = 8
SWA_SCORES_AHEAD = 4
PROJ_STEP_BLOCKS = 4
N_REL_BUCKETS = 32
REL_MAX_DISTANCE = 2048
NORM_EPS = 1e-6
LOG2E = 1.4426950408889634
ATTN_SCALE = HEAD_DIM ** -0.5 * LOG2E

LANES = 128
HEAD_PAIRS = MOBA_HEADS // 2
VT_ROWS = 80
MOBA_FAR_UNROLL = 2
MOBA_LOOP_GROUPS = 8
assert MOBA_LOOP_GROUPS % 2 == 0
VMEM_LIMIT = 56 * 1024 * 1024

BF16 = jnp.bfloat16
F32 = jnp.float32
NEG_INF = float("-inf")
M_INIT = -1e30


def _rel_bucket_np(dist):
    n = np.maximum(dist, 0)
    exact = N_REL_BUCKETS // 2
    nf = np.maximum(n, 1).astype(np.float32)
    large = exact + (np.log(nf / np.float32(exact)) / np.float32(np.log(REL_MAX_DISTANCE / exact))
                     * np.float32(N_REL_BUCKETS - exact)).astype(np.int32)
    large = np.minimum(large, N_REL_BUCKETS - 1)
    return np.where(n < exact, n, large)


def _moba_near_blocks():
    dist = np.arange(1, 2 * REL_MAX_DISTANCE)
    first_last = int(dist[_rel_bucket_np(dist) == N_REL_BUCKETS - 1][0])
    return -(-(first_last - 1) // MOBA_BLOCK) + 1


MOBA_NEAR = _moba_near_blocks()
assert 2 <= MOBA_FAR_UNROLL < MOBA_NEAR


def _nt_dot(a, b):
    return lax.dot_general(a, b, (((1,), (1,)), ((), ())), preferred_element_type=F32)


def _dot(a, b):
    return jnp.dot(a, b, preferred_element_type=F32)


def _rms_norm_rows(xf, g):
    ms = jnp.mean(xf * xf, axis=-1, keepdims=True)
    return xf * lax.rsqrt(ms + NORM_EPS) * g


def _head_norm(acc, g2):
    w = acc.shape[1]
    lane = lax.broadcasted_iota(jnp.int32, (1, LANES), 1)
    lo = lane < HEAD_DIM
    outs = []
    for b in range(w // LANES):
        xb = acc[:, b * LANES:(b + 1) * LANES]
        x2 = xb * xb
        s_lo = jnp.sum(jnp.where(lo, x2, 0.0), axis=-1, keepdims=True)
        s_hi = jnp.sum(jnp.where(lo, 0.0, x2), axis=-1, keepdims=True)
        r = jnp.where(lo, lax.rsqrt(s_lo / HEAD_DIM + NORM_EPS), lax.rsqrt(s_hi / HEAD_DIM + NORM_EPS))
        outs.append(xb * r * g2[:, b * LANES:(b + 1) * LANES])
    return outs[0] if len(outs) == 1 else jnp.concatenate(outs, axis=1)


def _v_transposed(v, n_heads):
    rows = v.shape[0]
    vt = v.T
    r = lax.broadcasted_iota(jnp.int32, (VT_ROWS - HEAD_DIM, rows), 0)
    aug = jnp.where(r == 0, 1.0, 0.0).astype(F32)
    parts = []
    for h in range(n_heads):
        parts.append(vt[h * HEAD_DIM:(h + 1) * HEAD_DIM])
        parts.append(aug)
    return jnp.concatenate(parts, axis=0).astype(BF16)


_W_QA, _W_KA, _W_VA, _W_QB, _W_KB, _W_VB, _W_GA, _W_GB = 512, 512, 512, 512, 256, 128, 1024, 1024
_PROJ_COLS = np.cumsum([0, _W_QA, _W_KA, _W_VA, _W_QB, SWA_KV_HEADS * HEAD_DIM, _W_VB, _W_GA, _W_GB]).tolist()


def _proj_kernel(x_ref, gmix_ref, w_ref, bgate_ref, gqa_ref, gka_ref, gqb_ref, gkb_ref,
                 qa_ref, ka_ref, kmean_ref, vat_ref, qb_ref, kb_ref, vbt_ref, ga_ref, gb_ref):
    h = _rms_norm_rows(x_ref[...], gmix_ref[...]).astype(BF16)
    c = _PROJ_COLS

    def seg(k):
        return _dot(h, w_ref[:, c[k]:c[k + 1]])

    qa_ref[...] = (_head_norm(seg(0), gqa_ref[...]) * ATTN_SCALE).astype(BF16)
    ka = _head_norm(seg(1), gka_ref[...])
    ka_ref[...] = ka.astype(BF16)
    for b in range(PROJ_STEP_BLOCKS):
        kmean_ref[b] = jnp.mean(ka[b * MOBA_BLOCK:(b + 1) * MOBA_BLOCK], axis=0, keepdims=True)
    vat_ref[...] = _v_transposed(seg(2), MOBA_HEADS)
    qb_ref[...] = (_head_norm(seg(3), gqb_ref[...]) * ATTN_SCALE).astype(BF16)
    kb = _head_norm(seg(4), gkb_ref[...])
    kb_swapped = pltpu.roll(kb, HEAD_DIM, axis=1)
    lo = lax.broadcasted_iota(jnp.int32, (1, LANES), 1) < HEAD_DIM
    kb_ref[...] = jnp.concatenate([jnp.where(lo, kb, kb_swapped), jnp.where(lo, kb_swapped, kb)], axis=1).astype(BF16)
    vbt_ref[...] = _v_transposed(seg(5), SWA_KV_HEADS)
    bg = bgate_ref[...]
    d = _W_GA
    ga_ref[...] = 1.0 / (1.0 + jnp.exp(-(seg(6) + bg[:, :d])))
    gb_ref[...] = 1.0 / (1.0 + jnp.exp(-(seg(7) + bg[:, d:])))


def _layer_spec(a, l, **kw):
    return pl.BlockSpec((None,) + a.shape[1:], lambda i: (l,) + (0,) * (a.ndim - 1), **kw)


def _proj_call(l, x, gmix, w, bgate, gqa, gka, gqb, gkb):
    s, d = x.shape
    tm = PROJ_STEP_BLOCKS * MOBA_BLOCK
    assert s % tm == 0
    nblk = s // MOBA_BLOCK
    full = lambda a: _layer_spec(a, l, pipeline_mode=pl.Buffered(1))
    rows = lambda width: pl.BlockSpec((tm, width), lambda i: (i, 0))
    cols = lambda height: pl.BlockSpec((height, tm), lambda i: (0, i))
    out_shape = (
        jax.ShapeDtypeStruct((s, _W_QA), BF16),
        jax.ShapeDtypeStruct((s, _W_KA), BF16),
        jax.ShapeDtypeStruct((nblk, 1, _W_KA), F32),
        jax.ShapeDtypeStruct((MOBA_HEADS * VT_ROWS, s), BF16),
        jax.ShapeDtypeStruct((s, _W_QB), BF16),
        jax.ShapeDtypeStruct((s, _W_KB), BF16),
        jax.ShapeDtypeStruct((SWA_KV_HEADS * VT_ROWS, s), BF16),
        jax.ShapeDtypeStruct((s, _W_GA), F32),
        jax.ShapeDtypeStruct((s, _W_GB), F32),
    )
    out_specs = (
        rows(_W_QA), rows(_W_KA), pl.BlockSpec((PROJ_STEP_BLOCKS, 1, _W_KA), lambda i: (i, 0, 0)),
        cols(MOBA_HEADS * VT_ROWS), rows(_W_QB), rows(_W_KB), cols(SWA_KV_HEADS * VT_ROWS),
        rows(_W_GA), rows(_W_GB),
    )
    return pl.pallas_call(
        _proj_kernel,
        grid=(s // tm,),
        in_specs=[rows(d), full(gmix), full(w), full(bgate), full(gqa), full(gka), full(gqb), full(gkb)],
        out_specs=out_specs,
        out_shape=out_shape,
        compiler_params=pltpu.CompilerParams(dimension_semantics=("arbitrary",), vmem_limit_bytes=VMEM_LIMIT),
        name="proj",
    )(x, gmix, w, bgate, gqa, gka, gqb, gkb)


def _moba_kernel(cfar_ref, q_ref, k_ref, vt_ref, kmean_ref, diag_ref, o_ref,
                 qq_scr, sel_scr, m_scr, acc_scr, sta_scr, stb_scr, bias_ref, *, nblk):
    p = pl.program_id(0)
    i = pl.program_id(1)
    blk = MOBA_BLOCK

    @pl.when(i == 0)
    def _():
        for d in range(MOBA_NEAR):
            for hh in range(2):
                skew = pltpu.roll(jnp.broadcast_to(diag_ref[d, hh:hh + 1, :], (blk, 2 * blk)), 0, 1,
                                  stride=1, stride_axis=0)
                bias_ref[d, :, hh * blk:(hh + 1) * blk] = skew[:, :blk]

    q = q_ref[...]
    lane = lax.broadcasted_iota(jnp.int32, (1, LANES), 1)
    km = kmean_ref[...]
    km_hi = km.astype(BF16)
    km_lo = (km - km_hi.astype(F32)).astype(BF16)
    n_iota = lax.broadcasted_iota(jnp.int32, (nblk, 2 * blk), 0)

    qq = jnp.concatenate([jnp.where(lane < HEAD_DIM, q, jnp.zeros_like(q)),
                          jnp.where(lane >= HEAD_DIM, q, jnp.zeros_like(q))], axis=0)
    qq_scr[...] = qq

    gate2 = _nt_dot(jnp.concatenate([km_hi, km_lo], axis=0), qq)
    gates = gate2[:nblk] + gate2[nblk:]

    def select_blocks():
        g = jnp.where(n_iota < i, gates, NEG_INF)
        sel = jnp.zeros((nblk, 2 * blk), F32)
        for _ in range(MOBA_TOPK):
            mx = jnp.max(g, axis=0, keepdims=True)
            idx = jnp.min(jnp.where(g == mx, n_iota, nblk), axis=0, keepdims=True)
            hit = n_iota == idx
            valid = jnp.where(mx > NEG_INF, 1.0, 0.0)
            sel = jnp.maximum(sel, jnp.where(hit, valid, 0.0))
            g = jnp.where(hit, NEG_INF, g)
        sel_scr[...] = jnp.where(n_iota == i, 1.0, sel)
        acc_scr[...] = jnp.zeros((2, VT_ROWS, blk), F32)
        m_scr[...] = jnp.full((1, 2 * blk), M_INIT, F32)

    col = lax.broadcasted_iota(jnp.int32, (1, 2 * blk), 1)
    cfar = jnp.where(col < blk, cfar_ref[2 * p], cfar_ref[2 * p + 1])
    n_far = jnp.maximum(i - (MOBA_NEAR - 1), 0)

    def scores(first, nb, st_ref):
        off = pl.multiple_of(first * blk, blk)
        st_ref[:nb * blk, :2 * blk] = _nt_dot(k_ref[pl.ds(off, nb * blk), :], qq_scr[...])

    def attend(first, nb, st_ref, near):
        for hh in range(2):
            cols = slice(hh * blk, (hh + 1) * blk)
            ms, pvs = [], []
            for u in range(nb):
                j = first + u
                s_u = st_ref[u * blk:(u + 1) * blk, cols]
                if near:
                    s_u = s_u + bias_ref[jnp.clip(i - j, 0, MOBA_NEAR - 1), :, cols]
                    on = sel_scr[pl.ds(j, 1), cols] > 0.5
                    m_u = jnp.max(s_u, axis=0, keepdims=True)
                    ms.append(jnp.where(on, m_u, NEG_INF))
                else:
                    on = sel_scr[pl.ds(j, 1), cols] > jnp.where(j < n_far, 0.5, 2.0)
                    m_u = jnp.max(s_u, axis=0, keepdims=True)
                    ms.append(jnp.where(on, m_u + cfar[:, cols], NEG_INF))
                pt = jnp.exp2(s_u - m_u).astype(BF16)
                ko = pl.multiple_of(j * blk, blk)
                pvs.append(_dot(vt_ref[hh * VT_ROWS:(hh + 1) * VT_ROWS, pl.ds(ko, blk)], pt))
            m_old = m_scr[:, cols]
            m_new = m_old
            for m_u in ms:
                m_new = jnp.maximum(m_new, m_u)
            acc = acc_scr[hh] * jnp.exp2(m_old - m_new)
            for m_u, pv in zip(ms, pvs):
                acc = acc + pv * jnp.exp2(m_u - m_new)
            acc_scr[hh] = acc
            m_scr[:, cols] = m_new

    fu = MOBA_FAR_UNROLL
    n_groups = (n_far + fu - 1) // fu
    near0 = jnp.maximum(i - (MOBA_NEAR - 1), 0)
    bufs = (sta_scr, stb_scr)

    def item_start(g):
        return jnp.where(g < n_groups, g * fu, near0)

    def run_items(items, then_first=None):
        for k, (first, nb, near) in enumerate(items):
            if k + 1 < len(items):
                scores(items[k + 1][0], items[k + 1][1], bufs[(k + 1) % 2])
            elif then_first is not None:
                scores(then_first, fu, bufs[(k + 1) % 2])
            attend(first, nb, bufs[k % 2], near)

    near_sizes = [fu] + [2] * ((MOBA_NEAR - fu - 1) // 2) + [1] * (1 + (MOBA_NEAR - fu - 1) % 2)
    near_items = [(near0 + int(st), nb, True) for st, nb in zip(np.cumsum([0] + near_sizes[:-1]), near_sizes)]

    scores(item_start(0), fu, sta_scr)
    select_blocks()

    unroll = MOBA_LOOP_GROUPS

    def loop_body(t, carry):
        g0 = t * unroll
        run_items([((g0 + k) * fu, fu, False) for k in range(unroll)], then_first=item_start(g0 + unroll))
        return carry

    lax.fori_loop(0, n_groups // unroll, loop_body, 0)

    g0 = n_groups // unroll * unroll
    for rem in range(unroll):

        @pl.when(n_groups - g0 == rem)
        def _():
            run_items([((g0 + k) * fu, fu, False) for k in range(rem)] + near_items)

    outs = []
    for hh in range(2):
        a = acc_scr[hh]
        outs.append(a[:HEAD_DIM] / a[HEAD_DIM:HEAD_DIM + 1])
    o_ref[...] = jnp.concatenate(outs, axis=0).T.astype(BF16)


def _moba_call(cfar, qa, ka, vat, kmean, bias_t):
    s = qa.shape[0]
    blk = MOBA_BLOCK
    nblk = s // blk
    grid_spec = pltpu.PrefetchScalarGridSpec(
        num_scalar_prefetch=1,
        grid=(HEAD_PAIRS, nblk),
        in_specs=[
            pl.BlockSpec((blk, LANES), lambda p, i, c: (i, p)),
            pl.BlockSpec((s, LANES), lambda p, i, c: (0, p)),
            pl.BlockSpec((2 * VT_ROWS, s), lambda p, i, c: (p, 0)),
            pl.BlockSpec((nblk, LANES), lambda p, i, c: (0, p)),
            pl.BlockSpec((MOBA_NEAR, None, 2, 2 * blk), lambda p, i, c: (0, p, 0, 0)),
        ],
        out_specs=pl.BlockSpec((blk, LANES), lambda p, i, c: (i, p)),
        scratch_shapes=[
            pltpu.VMEM((2 * blk, LANES), BF16),
            pltpu.VMEM((nblk, 2 * blk), F32),
            pltpu.VMEM((1, 2 * blk), F32),
            pltpu.VMEM((2, VT_ROWS, blk), F32),
            pltpu.VMEM((MOBA_FAR_UNROLL * blk, 2 * blk + LANES), F32),
            pltpu.VMEM((MOBA_FAR_UNROLL * blk, 2 * blk + LANES), F32),
            pltpu.VMEM((MOBA_NEAR, blk, 2 * blk), F32),
        ],
    )
    return pl.pallas_call(
        functools.partial(_moba_kernel, nblk=nblk),
        grid_spec=grid_spec,
        out_shape=jax.ShapeDtypeStruct((s, MOBA_HEADS * HEAD_DIM), BF16),
        compiler_params=pltpu.CompilerParams(dimension_semantics=("arbitrary", "arbitrary"),
                                             vmem_limit_bytes=VMEM_LIMIT),
        name="moba",
    )(cfar, qa, ka, vat, kmean, bias_t)


def _swa_kernel(sink_ref, q_ref, kprev_ref, kcur_ref, vprev_ref, vcur_ref, bias_ref, o_ref, st_scr):
    b = pl.program_id(0)
    w = SWA_WINDOW
    kall = jnp.concatenate([kprev_ref[...], kcur_ref[...]], axis=0)
    vall = jnp.concatenate([vprev_ref[...], vcur_ref[...]], axis=1)
    row = lax.broadcasted_iota(jnp.int32, (2 * w, 1), 0)
    keep = row >= jnp.where(b > 0, 0, w)
    lane = lax.broadcasted_iota(jnp.int32, (1, LANES), 1)
    col = lax.broadcasted_iota(jnp.int32, (1, 2 * w), 1)
    group = SWA_Q_HEADS // SWA_KV_HEADS
    chains = [(sb, pr) for sb in range(SWA_STEP_BLOCKS) for pr in range(SWA_Q_HEADS // 2)]
    def scores(c):
        sb, pr = chains[c]
        g = (2 * pr) // group
        qp = q_ref[sb * w:(sb + 1) * w, pr * LANES:(pr + 1) * LANES]
        zero = jnp.zeros_like(qp)
        qq = jnp.concatenate([jnp.where(lane < HEAD_DIM, qp, zero),
                              jnp.where(lane >= HEAD_DIM, qp, zero)], axis=0)
        kd = kall[sb * w:(sb + 2) * w, g * LANES:(g + 1) * LANES]
        st_scr[c, :, :2 * w] = _nt_dot(kd, qq)

    ahead = SWA_SCORES_AHEAD
    for c in range(min(ahead, len(chains))):
        scores(c)
    for c, (sb, pr) in enumerate(chains):
        if c + ahead < len(chains):
            scores(c + ahead)
        g = (2 * pr) // group
        st = st_scr[c, :, :2 * w] + bias_ref[pr]
        if sb == 0:
            st = jnp.where(keep, st, NEG_INF)
        sink = jnp.where(col < w, sink_ref[2 * pr], sink_ref[2 * pr + 1])
        m = jnp.maximum(jnp.max(st, axis=0, keepdims=True), sink)
        pt = jnp.exp2(st - m).astype(BF16)
        acc = _dot(vall[g * VT_ROWS:(g + 1) * VT_ROWS, sb * w:(sb + 2) * w], pt)
        denom = acc[HEAD_DIM:HEAD_DIM + 1] + jnp.exp2(sink - m)
        o = acc[:HEAD_DIM] / denom
        oo = jnp.concatenate([o[:, :w], o[:, w:]], axis=0)
        o_ref[sb * w:(sb + 1) * w, pr * LANES:(pr + 1) * LANES] = oo.T.astype(BF16)


def _swa_call(sinks, qb, kbd, vbt, bias_t):
    s = qb.shape[0]
    w = SWA_WINDOW
    n = SWA_STEP_BLOCKS
    assert s % (n * w) == 0
    prev_blk = lambda i: jnp.maximum(n * i - 1, 0)
    grid_spec = pltpu.PrefetchScalarGridSpec(
        num_scalar_prefetch=1,
        grid=(s // (n * w),),
        in_specs=[
            pl.BlockSpec((n * w, _W_QB), lambda i, c: (i, 0)),
            pl.BlockSpec((w, _W_KB), lambda i, c: (prev_blk(i), 0)),
            pl.BlockSpec((n * w, _W_KB), lambda i, c: (i, 0)),
            pl.BlockSpec((SWA_KV_HEADS * VT_ROWS, w), lambda i, c: (0, prev_blk(i))),
            pl.BlockSpec((SWA_KV_HEADS * VT_ROWS, n * w), lambda i, c: (0, i)),
            pl.BlockSpec(bias_t.shape, lambda i, c: (0, 0, 0)),
        ],
        out_specs=pl.BlockSpec((n * w, _W_QB), lambda i, c: (i, 0)),
        scratch_shapes=[pltpu.VMEM((n * SWA_Q_HEADS // 2, 2 * w, 2 * w + LANES), F32)],
    )
    return pl.pallas_call(
        _swa_kernel,
        grid_spec=grid_spec,
        out_shape=jax.ShapeDtypeStruct((s, _W_QB), BF16),
        compiler_params=pltpu.CompilerParams(dimension_semantics=("arbitrary",), vmem_limit_bytes=VMEM_LIMIT),
        name="swa",
    )(sinks, qb, kbd, kbd, vbt, vbt, bias_t)


def _post_kernel(x_ref, ya_ref, yb_ref, ga_ref, gb_ref, wa_ref, wb_ref, wo_ref, gffn_ref, w1_ref, w2_ref, o_ref,
                 *, tf):
    merged = ga_ref[...] * _dot(ya_ref[...], wa_ref[...]) + gb_ref[...] * _dot(yb_ref[...], wb_ref[...])
    xn = x_ref[...] + _dot(merged.astype(BF16), wo_ref[...])
    h = _rms_norm_rows(xn, gffn_ref[...]).astype(BF16)
    acc = xn
    for c in range(w1_ref.shape[1] // tf):
        u = jnp.maximum(_dot(h, w1_ref[:, c * tf:(c + 1) * tf]), 0.0)
        acc = acc + _dot((u * u).astype(BF16), w2_ref[c * tf:(c + 1) * tf, :])
    o_ref[...] = acc


def _post_call(l, x, ya, yb, ga, gb, wa, wb, wo, gffn, w1, w2, tm=512, tf=1024):
    s, d = x.shape
    full = lambda a: _layer_spec(a, l, pipeline_mode=pl.Buffered(1))
    rows = lambda a: pl.BlockSpec((tm, a.shape[1]), lambda i: (i, 0))
    return pl.pallas_call(
        functools.partial(_post_kernel, tf=tf),
        grid=(s // tm,),
        in_specs=[rows(x), rows(ya), rows(yb), rows(ga), rows(gb),
                  full(wa), full(wb), full(wo), full(gffn), full(w1), full(w2)],
        out_specs=pl.BlockSpec((tm, d), lambda i: (i, 0)),
        out_shape=jax.ShapeDtypeStruct((s, d), F32),
        compiler_params=pltpu.CompilerParams(dimension_semantics=("arbitrary",), vmem_limit_bytes=VMEM_LIMIT),
        name="post",
    )(x, ya, yb, ga, gb, wa, wb, wo, gffn, w1, w2)


def _bias_by_distance(table, lo, hi, keep):
    dist = np.arange(lo, hi)
    vals = jnp.take(table, jnp.asarray(_rel_bucket_np(dist).astype(np.int32)), axis=0).T
    return jnp.where(jnp.asarray(keep(dist))[None, :], vals, NEG_INF)


def _toeplitz(w, n):
    lead = w.shape[:-1]
    w_pad = jnp.concatenate([w, jnp.zeros(lead + (1,), w.dtype)], axis=-1)
    skew = jnp.tile(w_pad, (1,) * len(lead) + (n,))[..., :n * (2 * n - 1)].reshape(lead + (n, 2 * n - 1))
    return skew[..., n - 1:]


def _bias_tables(rel_bias):
    blk, w = MOBA_BLOCK, SWA_WINDOW
    vec = _bias_by_distance(rel_bias[:, :MOBA_HEADS], 1 - blk, MOBA_NEAR * blk, lambda dist: dist >= 0)
    vec = jnp.concatenate([vec, jnp.zeros((MOBA_HEADS, 1), F32)], axis=1)
    moba = jnp.stack([jnp.roll(vec[:, d * blk:(d + 2) * blk], 1 - blk, axis=1) for d in range(MOBA_NEAR)])
    moba = moba.reshape(MOBA_NEAR, HEAD_PAIRS, 2, 2 * blk)
    cfar = rel_bias[N_REL_BUCKETS - 1, :MOBA_HEADS]

    vec = _bias_by_distance(rel_bias[:, MOBA_HEADS:], 1 - w, 3 * w, lambda dist: (dist >= 0) & (dist < w))
    swa = _toeplitz(vec, 2 * w)[:, :, :w]
    swa = swa.reshape(SWA_Q_HEADS // 2, 2, 2 * w, w).transpose(0, 2, 1, 3).reshape(SWA_Q_HEADS // 2, 2 * w, 2 * w)
    return (moba * LOG2E).astype(F32), (cfar * LOG2E).astype(F32), (swa * LOG2E).astype(F32)


def kernel(x, rel_bias, g_mix, w_in, b_gate, q_norm_a, k_norm_a, q_norm_b, k_norm_b, sinks,
           w_branch_a, w_branch_b, w_out, g_ffn, w_ff1, w_ff2):
    b, s, d = x.shape
    assert b == 1 and s % MOBA_BLOCK == 0 and s // MOBA_BLOCK >= MOBA_NEAR
    depth = w_in.shape[0]
    bias_moba, cfar, bias_swa = _bias_tables(rel_bias)
    row = lambda a: a[:, None, :]
    gains = lambda g, width: row(jnp.tile(g, (1, width // HEAD_DIM)))
    w_in_b, wa_b, wb_b, wo_b, w1_b, w2_b = (a.astype(BF16) for a in (w_in, w_branch_a, w_branch_b, w_out, w_ff1, w_ff2))
    gqa, gka = gains(q_norm_a, _W_QA), gains(k_norm_a, _W_KA)
    gqb, gkb = gains(q_norm_b, _W_QB), gains(k_norm_b, SWA_KV_HEADS * HEAD_DIM)
    sinks2 = sinks * LOG2E
    xs = x[0]
    for l in range(depth):
        qa, ka, kmean, vat, qb, kbd, vbt, ga, gb = _proj_call(l, xs, row(g_mix), w_in_b, row(b_gate), gqa, gka, gqb, gkb)
        ya = _moba_call(cfar, qa, ka, vat, kmean.reshape(kmean.shape[0], kmean.shape[2]), bias_moba)
        yb = _swa_call(sinks2[l], qb, kbd, vbt, bias_swa)
        xs = _post_call(l, xs, ya, yb, ga, gb, wa_b, wb_b, wo_b, row(g_ffn), w1_b, w2_b)
    return xs[None]
```

```python
import functools

import jax
import jax.numpy as jnp
import numpy as np
from jax import lax
from jax.experimental import pallas as pl
from jax.experimental.pallas import tpu as pltpu

HEAD_DIM = 64
MOBA_HEADS = 8
MOBA_BLOCK = 256
MOBA_TOPK = 3
SWA_Q_HEADS = 8
SWA_KV_HEADS = 2
SWA_WINDOW = 128
SWA_STEP_BLOCKS = 8
SWA_SCORES_AHEAD = 4
PROJ_STEP_BLOCKS = 4
N_REL_BUCKETS = 32
REL_MAX_DISTANCE = 2048
NORM_EPS = 1e-6
LOG2E = 1.4426950408889634
ATTN_SCALE = HEAD_DIM ** -0.5 * LOG2E

LANES = 128
HEAD_PAIRS = MOBA_HEADS // 2
VT_ROWS = 80
MOBA_FAR_UNROLL = 2
MOBA_LOOP_GROUPS = 8
assert MOBA_LOOP_GROUPS % 2 == 0
VMEM_LIMIT = 56 * 1024 * 1024

BF16 = jnp.bfloat16
F32 = jnp.float32
NEG_INF = float("-inf")
M_INIT = -1e30


def _rel_bucket_np(dist):
    n = np.maximum(dist, 0)
    exact = N_REL_BUCKETS // 2
    nf = np.maximum(n, 1).astype(np.float32)
    large = exact + (np.log(nf / np.float32(exact)) / np.float32(np.log(REL_MAX_DISTANCE / exact))
                     * np.float32(N_REL_BUCKETS - exact)).astype(np.int32)
    large = np.minimum(large, N_REL_BUCKETS - 1)
    return np.where(n < exact, n, large)


def _moba_near_blocks():
    dist = np.arange(1, 2 * REL_MAX_DISTANCE)
    first_last = int(dist[_rel_bucket_np(dist) == N_REL_BUCKETS - 1][0])
    return -(-(first_last - 1) // MOBA_BLOCK) + 1


MOBA_NEAR = _moba_near_blocks()
assert 2 <= MOBA_FAR_UNROLL < MOBA_NEAR


def _nt_dot(a, b):
    return lax.dot_general(a, b, (((1,), (1,)), ((), ())), preferred_element_type=F32)


def _dot(a, b):
    return jnp.dot(a, b, preferred_element_type=F32)


def _rms_norm_rows(xf, g):
    ms = jnp.mean(xf * xf, axis=-1, keepdims=True)
    return xf * lax.rsqrt(ms + NORM_EPS) * g


def _head_norm(acc, g2):
    w = acc.shape[1]
    lane = lax.broadcasted_iota(jnp.int32, (1, LANES), 1)
    lo = lane < HEAD_DIM
    outs = []
    for b in range(w // LANES):
        xb = acc[:, b * LANES:(b + 1) * LANES]
        x2 = xb * xb
        s_lo = jnp.sum(jnp.where(lo, x2, 0.0), axis=-1, keepdims=True)
        s_hi = jnp.sum(jnp.where(lo, 0.0, x2), axis=-1, keepdims=True)
        r = jnp.where(lo, lax.rsqrt(s_lo / HEAD_DIM + NORM_EPS), lax.rsqrt(s_hi / HEAD_DIM + NORM_EPS))
        outs.append(xb * r * g2[:, b * LANES:(b + 1) * LANES])
    return outs[0] if len(outs) == 1 else jnp.concatenate(outs, axis=1)


def _v_transposed(v, n_heads):
    rows = v.shape[0]
    vt = v.T
    r = lax.broadcasted_iota(jnp.int32, (VT_ROWS - HEAD_DIM, rows), 0)
    aug = jnp.where(r == 0, 1.0, 0.0).astype(F32)
    parts = []
    for h in range(n_heads):
        parts.append(vt[h * HEAD_DIM:(h + 1) * HEAD_DIM])
        parts.append(aug)
    return jnp.concatenate(parts, axis=0).astype(BF16)


_W_QA, _W_KA, _W_VA, _W_QB, _W_KB, _W_VB, _W_GA, _W_GB = 512, 512, 512, 512, 256, 128, 1024, 1024
_PROJ_COLS = np.cumsum([0, _W_QA, _W_KA, _W_VA, _W_QB, SWA_KV_HEADS * HEAD_DIM, _W_VB, _W_GA, _W_GB]).tolist()


def _proj_kernel(x_ref, gmix_ref, w_ref, bgate_ref, gqa_ref, gka_ref, gqb_ref, gkb_ref,
                 qa_ref, ka_ref, kmean_ref, vat_ref, qb_ref, kb_ref, vbt_ref, ga_ref, gb_ref):
    h = _rms_norm_rows(x_ref[...], gmix_ref[...]).astype(BF16)
    c = _PROJ_COLS

    def seg(k):
        return _dot(h, w_ref[:, c[k]:c[k + 1]])

    qa_ref[...] = (_head_norm(seg(0), gqa_ref[...]) * ATTN_SCALE).astype(BF16)
    ka = _head_norm(seg(1), gka_ref[...])
    ka_ref[...] = ka.astype(BF16)
    for b in range(PROJ_STEP_BLOCKS):
        kmean_ref[b] = jnp.mean(ka[b * MOBA_BLOCK:(b + 1) * MOBA_BLOCK], axis=0, keepdims=True)
    vat_ref[...] = _v_transposed(seg(2), MOBA_HEADS)
    qb_ref[...] = (_head_norm(seg(3), gqb_ref[...]) * ATTN_SCALE).astype(BF16)
    kb = _head_norm(seg(4), gkb_ref[...])
    kb_swapped = pltpu.roll(kb, HEAD_DIM, axis=1)
    lo = lax.broadcasted_iota(jnp.int32, (1, LANES), 1) < HEAD_DIM
    kb_ref[...] = jnp.concatenate([jnp.where(lo, kb, kb_swapped), jnp.where(lo, kb_swapped, kb)], axis=1).astype(BF16)
    vbt_ref[...] = _v_transposed(seg(5), SWA_KV_HEADS)
    bg = bgate_ref[...]
    d = _W_GA
    ga_ref[...] = 1.0 / (1.0 + jnp.exp(-(seg(6) + bg[:, :d])))
    gb_ref[...] = 1.0 / (1.0 + jnp.exp(-(seg(7) + bg[:, d:])))


def _layer_spec(a, l, **kw):
    return pl.BlockSpec((None,) + a.shape[1:], lambda i: (l,) + (0,) * (a.ndim - 1), **kw)


def _proj_call(l, x, gmix, w, bgate, gqa, gka, gqb, gkb):
    s, d = x.shape
    tm = PROJ_STEP_BLOCKS * MOBA_BLOCK
    assert s % tm == 0
    nblk = s // MOBA_BLOCK
    full = lambda a: _layer_spec(a, l, pipeline_mode=pl.Buffered(1))
    rows = lambda width: pl.BlockSpec((tm, width), lambda i: (i, 0))
    cols = lambda height: pl.BlockSpec((height, tm), lambda i: (0, i))
    out_shape = (
        jax.ShapeDtypeStruct((s, _W_QA), BF16),
        jax.ShapeDtypeStruct((s, _W_KA), BF16),
        jax.ShapeDtypeStruct((nblk, 1, _W_KA), F32),
        jax.ShapeDtypeStruct((MOBA_HEADS * VT_ROWS, s), BF16),
        jax.ShapeDtypeStruct((s, _W_QB), BF16),
        jax.ShapeDtypeStruct((s, _W_KB), BF16),
        jax.ShapeDtypeStruct((SWA_KV_HEADS * VT_ROWS, s), BF16),
        jax.ShapeDtypeStruct((s, _W_GA), F32),
        jax.ShapeDtypeStruct((s, _W_GB), F32),
    )
    out_specs = (
        rows(_W_QA), rows(_W_KA), pl.BlockSpec((PROJ_STEP_BLOCKS, 1, _W_KA), lambda i: (i, 0, 0)),
        cols(MOBA_HEADS * VT_ROWS), rows(_W_QB), rows(_W_KB), cols(SWA_KV_HEADS * VT_ROWS),
        rows(_W_GA), rows(_W_GB),
    )
    return pl.pallas_call(
        _proj_kernel,
        grid=(s // tm,),
        in_specs=[rows(d), full(gmix), full(w), full(bgate), full(gqa), full(gka), full(gqb), full(gkb)],
        out_specs=out_specs,
        out_shape=out_shape,
        compiler_params=pltpu.CompilerParams(dimension_semantics=("arbitrary",), vmem_limit_bytes=VMEM_LIMIT,
                                             allow_input_fusion=[False, False, True] + [False] * 5),
        name="proj",
    )(x, gmix, w, bgate, gqa, gka, gqb, gkb)


def _moba_kernel(cfar_ref, q_ref, k_ref, vt_ref, kmean_ref, diag_ref, o_ref,
                 qq_scr, sel_scr, m_scr, acc_scr, sta_scr, stb_scr, bias_ref, *, nblk):
    p = pl.program_id(0)
    i = pl.program_id(1)
    blk = MOBA_BLOCK

    @pl.when(i == 0)
    def _():
        for d in range(MOBA_NEAR):
            for hh in range(2):
                skew = pltpu.roll(jnp.broadcast_to(diag_ref[d, hh:hh + 1, :], (blk, 2 * blk)), 0, 1,
                                  stride=1, stride_axis=0)
                bias_ref[d, :, hh * blk:(hh + 1) * blk] = skew[:, :blk]

    q = q_ref[...]
    lane = lax.broadcasted_iota(jnp.int32, (1, LANES), 1)
    km = kmean_ref[...]
    km_hi = km.astype(BF16)
    km_lo = (km - km_hi.astype(F32)).astype(BF16)
    n_iota = lax.broadcasted_iota(jnp.int32, (nblk, 2 * blk), 0)

    qq = jnp.concatenate([jnp.where(lane < HEAD_DIM, q, jnp.zeros_like(q)),
                          jnp.where(lane >= HEAD_DIM, q, jnp.zeros_like(q))], axis=0)
    qq_scr[...] = qq

    gate2 = _nt_dot(jnp.concatenate([km_hi, km_lo], axis=0), qq)
    gates = gate2[:nblk] + gate2[nblk:]

    def select_blocks():
        g = jnp.where(n_iota < i, gates, NEG_INF)
        sel = jnp.zeros((nblk, 2 * blk), F32)
        for _ in range(MOBA_TOPK):
            mx = jnp.max(g, axis=0, keepdims=True)
            idx = jnp.min(jnp.where(g == mx, n_iota, nblk), axis=0, keepdims=True)
            hit = n_iota == idx
            valid = jnp.where(mx > NEG_INF, 1.0, 0.0)
            sel = jnp.maximum(sel, jnp.where(hit, valid, 0.0))
            g = jnp.where(hit, NEG_INF, g)
        sel_scr[...] = jnp.where(n_iota == i, 1.0, sel)
        acc_scr[...] = jnp.zeros((2, VT_ROWS, blk), F32)
        m_scr[...] = jnp.full((1, 2 * blk), M_INIT, F32)

    col = lax.broadcasted_iota(jnp.int32, (1, 2 * blk), 1)
    cfar = jnp.where(col < blk, cfar_ref[2 * p], cfar_ref[2 * p + 1])
    n_far = jnp.maximum(i - (MOBA_NEAR - 1), 0)

    def scores(first, nb, st_ref):
        off = pl.multiple_of(first * blk, blk)
        st_ref[:nb * blk, :2 * blk] = _nt_dot(k_ref[pl.ds(off, nb * blk), :], qq_scr[...])

    def attend(first, nb, st_ref, near):
        ms, pvs = [], []
        for u in range(nb):
            j = first + u
            s_u = st_ref[u * blk:(u + 1) * blk, :2 * blk]
            if near:
                s_u = s_u + bias_ref[jnp.clip(i - j, 0, MOBA_NEAR - 1)]
                on = sel_scr[pl.ds(j, 1), :] > 0.5
                m_u = jnp.max(s_u, axis=0, keepdims=True)
                ms.append(jnp.where(on, m_u, NEG_INF))
            else:
                on = sel_scr[pl.ds(j, 1), :] > jnp.where(j < n_far, 0.5, 2.0)
                m_u = jnp.max(s_u, axis=0, keepdims=True)
                ms.append(jnp.where(on, m_u + cfar, NEG_INF))
            pt = jnp.exp2(s_u - m_u).astype(BF16)
            ko = pl.multiple_of(j * blk, blk)
            pvs.append([_dot(vt_ref[hh * VT_ROWS:(hh + 1) * VT_ROWS, pl.ds(ko, blk)], pt[:, hh * blk:(hh + 1) * blk])
                        for hh in range(2)])
        m_old = m_scr[...]
        m_new = m_old
        for m_u in ms:
            m_new = jnp.maximum(m_new, m_u)
        alpha = jnp.exp2(m_old - m_new)
        ws = [jnp.exp2(m_u - m_new) for m_u in ms]
        for hh in range(2):
            cols = slice(hh * blk, (hh + 1) * blk)
            acc = acc_scr[hh] * alpha[:, cols]
            for w_u, pv in zip(ws, pvs):
                acc = acc + pv[hh] * w_u[:, cols]
            acc_scr[hh] = acc
        m_scr[...] = m_new

    fu = MOBA_FAR_UNROLL
    n_groups = (n_far + fu - 1) // fu
    near0 = jnp.maximum(i - (MOBA_NEAR - 1), 0)
    bufs = (sta_scr, stb_scr)

    def item_start(g):
        return jnp.where(g < n_groups, g * fu, near0)

    def run_items(items, then_first=None):
        for k, (first, nb, near) in enumerate(items):
            if k + 1 < len(items):
                scores(items[k + 1][0], items[k + 1][1], bufs[(k + 1) % 2])
            elif then_first is not None:
                scores(then_first, fu, bufs[(k + 1) % 2])
            attend(first, nb, bufs[k % 2], near)

    near_sizes = [fu] + [2] * ((MOBA_NEAR - fu - 1) // 2) + [1] * (1 + (MOBA_NEAR - fu - 1) % 2)
    near_items = [(near0 + int(st), nb, True) for st, nb in zip(np.cumsum([0] + near_sizes[:-1]), near_sizes)]

    scores(item_start(0), fu, sta_scr)
    select_blocks()

    unroll = MOBA_LOOP_GROUPS

    def loop_body(t, carry):
        g0 = t * unroll
        run_items([((g0 + k) * fu, fu, False) for k in range(unroll)], then_first=item_start(g0 + unroll))
        return carry

    lax.fori_loop(0, n_groups // unroll, loop_body, 0)

    g0 = n_groups // unroll * unroll
    for rem in range(unroll):

        @pl.when(n_groups - g0 == rem)
        def _():
            run_items([((g0 + k) * fu, fu, False) for k in range(rem)] + near_items)

    outs = []
    for hh in range(2):
        a = acc_scr[hh]
        outs.append(a[:HEAD_DIM] / a[HEAD_DIM:HEAD_DIM + 1])
    o_ref[...] = jnp.concatenate(outs, axis=0).T.astype(BF16)


def _moba_call(cfar, qa, ka, vat, kmean, bias_t):
    s = qa.shape[0]
    blk = MOBA_BLOCK
    nblk = s // blk
    grid_spec = pltpu.PrefetchScalarGridSpec(
        num_scalar_prefetch=1,
        grid=(HEAD_PAIRS, nblk),
        in_specs=[
            pl.BlockSpec((blk, LANES), lambda p, i, c: (i, p)),
            pl.BlockSpec((s, LANES), lambda p, i, c: (0, p)),
            pl.BlockSpec((2 * VT_ROWS, s), lambda p, i, c: (p, 0)),
            pl.BlockSpec((nblk, LANES), lambda p, i, c: (0, p)),
            pl.BlockSpec((MOBA_NEAR, None, 2, 2 * blk), lambda p, i, c: (0, p, 0, 0)),
        ],
        out_specs=pl.BlockSpec((blk, LANES), lambda p, i, c: (i, p)),
        scratch_shapes=[
            pltpu.VMEM((2 * blk, LANES), BF16),
            pltpu.VMEM((nblk, 2 * blk), F32),
            pltpu.VMEM((1, 2 * blk), F32),
            pltpu.VMEM((2, VT_ROWS, blk), F32),
            pltpu.VMEM((MOBA_FAR_UNROLL * blk, 2 * blk + LANES), F32),
            pltpu.VMEM((MOBA_FAR_UNROLL * blk, 2 * blk + LANES), F32),
            pltpu.VMEM((MOBA_NEAR, blk, 2 * blk), F32),
        ],
    )
    return pl.pallas_call(
        functools.partial(_moba_kernel, nblk=nblk),
        grid_spec=grid_spec,
        out_shape=jax.ShapeDtypeStruct((s, MOBA_HEADS * HEAD_DIM), BF16),
        compiler_params=pltpu.CompilerParams(dimension_semantics=("arbitrary", "arbitrary"),
                                             vmem_limit_bytes=VMEM_LIMIT),
        name="moba",
    )(cfar, qa, ka, vat, kmean, bias_t)


def _swa_kernel(sink_ref, q_ref, kprev_ref, kcur_ref, vprev_ref, vcur_ref, bias_ref, o_ref, st_scr):
    b = pl.program_id(0)
    w = SWA_WINDOW
    kall = jnp.concatenate([kprev_ref[...], kcur_ref[...]], axis=0)
    vall = jnp.concatenate([vprev_ref[...], vcur_ref[...]], axis=1)
    row = lax.broadcasted_iota(jnp.int32, (2 * w, 1), 0)
    keep = row >= jnp.where(b > 0, 0, w)
    lane = lax.broadcasted_iota(jnp.int32, (1, LANES), 1)
    col = lax.broadcasted_iota(jnp.int32, (1, 2 * w), 1)
    group = SWA_Q_HEADS // SWA_KV_HEADS
    chains = [(sb, pr) for sb in range(SWA_STEP_BLOCKS) for pr in range(SWA_Q_HEADS // 2)]
    def scores(c):
        sb, pr = chains[c]
        g = (2 * pr) // group
        qp = q_ref[sb * w:(sb + 1) * w, pr * LANES:(pr + 1) * LANES]
        zero = jnp.zeros_like(qp)
        qq = jnp.concatenate([jnp.where(lane < HEAD_DIM, qp, zero),
                              jnp.where(lane >= HEAD_DIM, qp, zero)], axis=0)
        kd = kall[sb * w:(sb + 2) * w, g * LANES:(g + 1) * LANES]
        st_scr[c, :, :2 * w] = _nt_dot(kd, qq)

    ahead = SWA_SCORES_AHEAD
    for c in range(min(ahead, len(chains))):
        scores(c)
    for c, (sb, pr) in enumerate(chains):
        if c + ahead < len(chains):
            scores(c + ahead)
        g = (2 * pr) // group
        st = st_scr[c, :, :2 * w] + bias_ref[pr]
        if sb == 0:
            st = jnp.where(keep, st, NEG_INF)
        sink = jnp.where(col < w, sink_ref[2 * pr], sink_ref[2 * pr + 1])
        m = jnp.maximum(jnp.max(st, axis=0, keepdims=True), sink)
        pt = jnp.exp2(st - m).astype(BF16)
        acc = _dot(vall[g * VT_ROWS:(g + 1) * VT_ROWS, sb * w:(sb + 2) * w], pt)
        denom = acc[HEAD_DIM:HEAD_DIM + 1] + jnp.exp2(sink - m)
        o = acc[:HEAD_DIM] / denom
        oo = jnp.concatenate([o[:, :w], o[:, w:]], axis=0)
        o_ref[sb * w:(sb + 1) * w, pr * LANES:(pr + 1) * LANES] = oo.T.astype(BF16)


def _swa_call(sinks, qb, kbd, vbt, bias_t):
    s = qb.shape[0]
    w = SWA_WINDOW
    n = SWA_STEP_BLOCKS
    assert s % (n * w) == 0
    prev_blk = lambda i: jnp.maximum(n * i - 1, 0)
    grid_spec = pltpu.PrefetchScalarGridSpec(
        num_scalar_prefetch=1,
        grid=(s // (n * w),),
        in_specs=[
            pl.BlockSpec((n * w, _W_QB), lambda i, c: (i, 0)),
            pl.BlockSpec((w, _W_KB), lambda i, c: (prev_blk(i), 0)),
            pl.BlockSpec((n * w, _W_KB), lambda i, c: (i, 0)),
            pl.BlockSpec((SWA_KV_HEADS * VT_ROWS, w), lambda i, c: (0, prev_blk(i))),
            pl.BlockSpec((SWA_KV_HEADS * VT_ROWS, n * w), lambda i, c: (0, i)),
            pl.BlockSpec(bias_t.shape, lambda i, c: (0, 0, 0)),
        ],
        out_specs=pl.BlockSpec((n * w, _W_QB), lambda i, c: (i, 0)),
        scratch_shapes=[pltpu.VMEM((n * SWA_Q_HEADS // 2, 2 * w, 2 * w + LANES), F32)],
    )
    return pl.pallas_call(
        _swa_kernel,
        grid_spec=grid_spec,
        out_shape=jax.ShapeDtypeStruct((s, _W_QB), BF16),
        compiler_params=pltpu.CompilerParams(dimension_semantics=("arbitrary",), vmem_limit_bytes=VMEM_LIMIT),
        name="swa",
    )(sinks, qb, kbd, kbd, vbt, vbt, bias_t)


def _post_kernel(x_ref, ya_ref, yb_ref, ga_ref, gb_ref, wa_ref, wb_ref, wo_ref, gffn_ref, w1_ref, w2_ref, o_ref,
                 *, tf):
    merged = ga_ref[...] * _dot(ya_ref[...], wa_ref[...]) + gb_ref[...] * _dot(yb_ref[...], wb_ref[...])
    xn = x_ref[...] + _dot(merged.astype(BF16), wo_ref[...])
    h = _rms_norm_rows(xn, gffn_ref[...]).astype(BF16)
    acc = xn
    for c in range(w1_ref.shape[1] // tf):
        u = jnp.maximum(_dot(h, w1_ref[:, c * tf:(c + 1) * tf]), 0.0)
        acc = acc + _dot((u * u).astype(BF16), w2_ref[c * tf:(c + 1) * tf, :])
    o_ref[...] = acc


def _post_call(l, x, ya, yb, ga, gb, wa, wb, wo, gffn, w1, w2, tm=512, tf=1024):
    s, d = x.shape
    full = lambda a: _layer_spec(a, l, pipeline_mode=pl.Buffered(1))
    rows = lambda a: pl.BlockSpec((tm, a.shape[1]), lambda i: (i, 0))
    return pl.pallas_call(
        functools.partial(_post_kernel, tf=tf),
        grid=(s // tm,),
        in_specs=[rows(x), rows(ya), rows(yb), rows(ga), rows(gb),
                  full(wa), full(wb), full(wo), full(gffn), full(w1), full(w2)],
        out_specs=pl.BlockSpec((tm, d), lambda i: (i, 0)),
        out_shape=jax.ShapeDtypeStruct((s, d), F32),
        compiler_params=pltpu.CompilerParams(dimension_semantics=("arbitrary",), vmem_limit_bytes=VMEM_LIMIT,
                                             allow_input_fusion=[False] * 5 + [True, True, True, False, True, True]),
        name="post",
    )(x, ya, yb, ga, gb, wa, wb, wo, gffn, w1, w2)


def _bias_by_distance(table, lo, hi, keep):
    dist = np.arange(lo, hi)
    vals = jnp.take(table, jnp.asarray(_rel_bucket_np(dist).astype(np.int32)), axis=0).T
    return jnp.where(jnp.asarray(keep(dist))[None, :], vals, NEG_INF)


def _toeplitz(w, n):
    lead = w.shape[:-1]
    w_pad = jnp.concatenate([w, jnp.zeros(lead + (1,), w.dtype)], axis=-1)
    skew = jnp.tile(w_pad, (1,) * len(lead) + (n,))[..., :n * (2 * n - 1)].reshape(lead + (n, 2 * n - 1))
    return skew[..., n - 1:]


def _bias_tables(rel_bias):
    blk, w = MOBA_BLOCK, SWA_WINDOW
    vec = _bias_by_distance(rel_bias[:, :MOBA_HEADS], 1 - blk, MOBA_NEAR * blk, lambda dist: dist >= 0)
    vec = jnp.concatenate([vec, jnp.zeros((MOBA_HEADS, 1), F32)], axis=1)
    moba = jnp.stack([jnp.roll(vec[:, d * blk:(d + 2) * blk], 1 - blk, axis=1) for d in range(MOBA_NEAR)])
    moba = moba.reshape(MOBA_NEAR, HEAD_PAIRS, 2, 2 * blk)
    cfar = rel_bias[N_REL_BUCKETS - 1, :MOBA_HEADS]

    vec = _bias_by_distance(rel_bias[:, MOBA_HEADS:], 1 - w, 3 * w, lambda dist: (dist >= 0) & (dist < w))
    swa = _toeplitz(vec, 2 * w)[:, :, :w]
    swa = swa.reshape(SWA_Q_HEADS // 2, 2, 2 * w, w).transpose(0, 2, 1, 3).reshape(SWA_Q_HEADS // 2, 2 * w, 2 * w)
    return (moba * LOG2E).astype(F32), (cfar * LOG2E).astype(F32), (swa * LOG2E).astype(F32)


def kernel(x, rel_bias, g_mix, w_in, b_gate, q_norm_a, k_norm_a, q_norm_b, k_norm_b, sinks,
           w_branch_a, w_branch_b, w_out, g_ffn, w_ff1, w_ff2):
    b, s, d = x.shape
    assert b == 1 and s % MOBA_BLOCK == 0 and s // MOBA_BLOCK >= MOBA_NEAR
    depth = w_in.shape[0]
    bias_moba, cfar, bias_swa = _bias_tables(rel_bias)
    row = lambda a: a[:, None, :]
    gains = lambda g, width: row(jnp.tile(g, (1, width // HEAD_DIM)))
    w_in_b, wa_b, wb_b, wo_b, w1_b, w2_b = (a.astype(BF16) for a in (w_in, w_branch_a, w_branch_b, w_out, w_ff1, w_ff2))
    gqa, gka = gains(q_norm_a, _W_QA), gains(k_norm_a, _W_KA)
    gqb, gkb = gains(q_norm_b, _W_QB), gains(k_norm_b, SWA_KV_HEADS * HEAD_DIM)
    sinks2 = sinks * LOG2E
    xs = x[0]
    for l in range(depth):
        qa, ka, kmean, vat, qb, kbd, vbt, ga, gb = _proj_call(l, xs, row(g_mix), w_in_b, row(b_gate), gqa, gka, gqb, gkb)
        ya = _moba_call(cfar, qa, ka, vat, kmean.reshape(kmean.shape[0], kmean.shape[2]), bias_moba)
        yb = _swa_call(sinks2[l], qb, kbd, vbt, bias_swa)
        xs = _post_call(l, xs, ya, yb, ga, gb, wa_b, wb_b, wo_b, row(g_ffn), w1_b, w2_b)
    return xs[None]
```
